```python
import math
import jax, jax.numpy as jnp
from jax import lax
import numpy as np

D_MODEL = 1024
BATCH = 8
SEQ = 4096
DEPTH = 2

EPS = 1e-6
SSD_HEAD_DIM = 64
SSD_INNER = D_MODEL
SSD_HEADS = SSD_INNER // SSD_HEAD_DIM
SSD_GROUPS = 2
SSD_STATE = 128
SSD_CONV = 4
SSD_CHUNK = 128
SSD_XBC = SSD_INNER + 2 * SSD_GROUPS * SSD_STATE
MLA_HEADS = D_MODEL // 128
MLA_NOPE = 128
MLA_ROPE = 64
MLA_V = 128
MLA_Q_LORA = D_MODEL // 2
MLA_KV_LORA = D_MODEL // 4
MLA_QBLOCK = 128
ROPE_THETA = 10000.0
GDN_HEAD_K = 128
GDN_HEAD_V = 128
GDN_V_HEADS = D_MODEL // GDN_HEAD_V
GDN_QK_HEADS = GDN_V_HEADS // 2
GDN_CONV = 4
GDN_CHUNK = 64
GDN_QK = GDN_QK_HEADS * GDN_HEAD_K
GDN_VW = GDN_V_HEADS * GDN_HEAD_V
GDN_QKV = 2 * GDN_QK + GDN_VW
N_BRANCH = 3
D_FF = 4 * D_MODEL
IN_SIZES = (SSD_INNER, SSD_XBC, SSD_HEADS, MLA_Q_LORA, MLA_KV_LORA, MLA_ROPE,
            GDN_QKV, GDN_VW, GDN_V_HEADS, GDN_V_HEADS, N_BRANCH * D_MODEL)
N_IN = (SSD_INNER + SSD_XBC + SSD_HEADS + MLA_Q_LORA + MLA_KV_LORA + MLA_ROPE
        + GDN_QKV + GDN_VW + 2 * GDN_V_HEADS + N_BRANCH * D_MODEL)

kernel_name = 'hybrid_ssd_mla_gdn_parallel_block'


def rmsnorm(x, g):
    xf = x.astype(jnp.float32)
    y = xf * lax.rsqrt(jnp.mean(xf * xf, axis=-1, keepdims=True) + EPS)
    return (y * g.astype(jnp.float32)).astype(x.dtype)


def l2norm(x):
    xf = x.astype(jnp.float32)
    return xf * lax.rsqrt(jnp.sum(xf * xf, axis=-1, keepdims=True) + EPS)


def causal_dwconv(x, w):
    K, C = w.shape
    return lax.conv_general_dilated(
        x, w[:, None, :].astype(x.dtype), window_strides=(1,), padding=[(K - 1, 0)],
        dimension_numbers=('NWC', 'WIO', 'NWC'), feature_group_count=C)


def rope_tables(positions):
    inv = ROPE_THETA ** (-jnp.arange(0, MLA_ROPE, 2, dtype=jnp.float32) / MLA_ROPE)
    ang = positions.astype(jnp.float32)[..., None] * inv
    return jnp.cos(ang), jnp.sin(ang)


def apply_rope(t, cos, sin):
    half = t.shape[-1] // 2
    tf = t.astype(jnp.float32)
    t1, t2 = tf[..., :half], tf[..., half:]
    return jnp.concatenate([t1 * cos - t2 * sin, t2 * cos + t1 * sin], axis=-1).astype(t.dtype)


def ssd_chunked(X, dA, Bm, Cm):
    b, s, g, e, p = X.shape
    n = Bm.shape[-1]
    Q = SSD_CHUNK
    c = s // Q
    X = X.reshape(b, c, Q, g, e, p)
    Bc = Bm.reshape(b, c, Q, g, n)
    Cc = Cm.reshape(b, c, Q, g, n)
    A = jnp.transpose(dA.reshape(b, c, Q, g, e), (0, 3, 4, 1, 2))
    A_cum = jnp.cumsum(A, axis=-1)
    tril = jnp.tril(jnp.ones((Q, Q), dtype=bool))
    L = jnp.exp(jnp.where(tril, A_cum[..., :, None] - A_cum[..., None, :], -jnp.inf))
    CB = jnp.einsum('bclgn,bcsgn->bcgls', Cc, Bc)
    y_diag = jnp.einsum('bcgls,bgecls,bcsgep->bclgep', CB, L, X)
    decay_states = jnp.exp(A_cum[..., -1:] - A_cum)
    states = jnp.einsum('bclgn,bgecl,bclgep->bcgepn', Bc, decay_states, X)
    chunk_decay = jnp.exp(A_cum[..., -1])

    def step(h, inp):
        st, dec = inp
        return h * dec[..., None, None] + st, h

    _, h_prev = lax.scan(step, jnp.zeros_like(states[:, 0]),
                         (jnp.moveaxis(states, 1, 0), jnp.moveaxis(chunk_decay, -1, 0)))
    h_prev = jnp.moveaxis(h_prev, 0, 1)
    y_off = jnp.einsum('bclgn,bcgepn,bgecl->bclgep', Cc, h_prev, jnp.exp(A_cum))
    return (y_diag + y_off).reshape(b, s, g, e, p)


def ssd_mixer(z, xbc, dt_raw, conv_w, conv_b, dt_bias, a_log, d_skip, norm_g):
    Bsz, S, _ = z.shape
    G, E, P, N = SSD_GROUPS, SSD_HEADS // SSD_GROUPS, SSD_HEAD_DIM, SSD_STATE
    xbc = jax.nn.silu(causal_dwconv(xbc, conv_w) + conv_b.astype(xbc.dtype))
    xs, bm, cm = jnp.split(xbc, [SSD_INNER, SSD_INNER + G * N], axis=-1)
    xs = xs.astype(jnp.float32).reshape(Bsz, S, G, E, P)
    bm = bm.astype(jnp.float32).reshape(Bsz, S, G, N)
    cm = cm.astype(jnp.float32).reshape(Bsz, S, G, N)
    dt = jax.nn.softplus(dt_raw.astype(jnp.float32) + dt_bias.astype(jnp.float32)).reshape(Bsz, S, G, E)
    a = -jnp.exp(a_log.astype(jnp.float32)).reshape(G, E)
    y = ssd_chunked(xs * dt[..., None], dt * a, bm, cm)
    y = y + d_skip.astype(jnp.float32).reshape(G, E)[:, :, None] * xs
    y = y.reshape(Bsz, S, SSD_INNER) * jax.nn.silu(z.astype(jnp.float32))
    y = rmsnorm(y.reshape(Bsz, S, G, SSD_INNER // G), norm_g.reshape(G, SSD_INNER // G))
    return y.reshape(Bsz, S, SSD_INNER).astype(z.dtype)


def mla_mixer(cq, ckv, k_rope, cos, sin, q_norm_g, w_uq, kv_norm_g, w_ukv):
    Bsz, S, _ = cq.shape
    H = MLA_HEADS
    q = (rmsnorm(cq, q_norm_g) @ w_uq).reshape(Bsz, S, H, MLA_NOPE + MLA_ROPE)
    q_nope, q_pe = jnp.split(q, [MLA_NOPE], axis=-1)
    q_pe = apply_rope(q_pe, cos[:, :, None, :], sin[:, :, None, :])
    kv = (rmsnorm(ckv, kv_norm_g) @ w_ukv).reshape(Bsz, S, H, MLA_NOPE + MLA_V)
    k_nope, v = jnp.split(kv, [MLA_NOPE], axis=-1)
    k_pe = apply_rope(k_rope, cos, sin)
    nblk = S // MLA_QBLOCK
    scale = (MLA_NOPE + MLA_ROPE) ** -0.5
    kpos = jnp.arange(S)

    def to_blocks(t):
        return jnp.moveaxis(t.reshape((Bsz, nblk, MLA_QBLOCK) + t.shape[2:]), 1, 0)

    def attend(args):
        qn, qp, blk = args
        sc = (jnp.einsum('bqhd,bkhd->bhqk', qn, k_nope, preferred_element_type=jnp.float32)
              + jnp.einsum('bqhr,bkr->bhqk', qp, k_pe, preferred_element_type=jnp.float32))
        qpos = blk * MLA_QBLOCK + jnp.arange(MLA_QBLOCK)
        sc = jnp.where(kpos[None, :] <= qpos[:, None], sc * scale, -jnp.inf)
        pr = jax.nn.softmax(sc, axis=-1)
        return jnp.einsum('bhqk,bkhd->bqhd', pr.astype(v.dtype), v)

    o = lax.map(attend, (to_blocks(q_nope), to_blocks(q_pe), jnp.arange(nblk)))
    return jnp.moveaxis(o, 0, 1).reshape(Bsz, S, H * MLA_V)


def gated_delta_chunked(q, k, v, g, beta):
    Bsz, S, H, DK = k.shape
    DV = v.shape[-1]
    L = GDN_CHUNK
    C = S // L

    def chunks(t):
        return jnp.moveaxis(t.reshape((Bsz, C, L, H) + t.shape[3:]), 3, 1)

    q = chunks(q * (DK ** -0.5))
    k = chunks(k)
    v = chunks(v)
    g = jnp.cumsum(chunks(g), axis=-1)
    beta = chunks(beta)
    incl = jnp.tril(jnp.ones((L, L), dtype=bool))
    strict = jnp.tril(jnp.ones((L, L), dtype=bool), -1)
    decay = jnp.exp(jnp.where(incl, g[..., :, None] - g[..., None, :], -jnp.inf))
    kb = k * beta[..., None]
    a_mat = jnp.where(strict, jnp.einsum('bhcld,bhcmd->bhclm', kb, k) * decay, 0.0)
    rhs = jnp.concatenate([v * beta[..., None], kb * jnp.exp(g)[..., None]], axis=-1)
    sol = lax.linalg.triangular_solve(a_mat + jnp.eye(L, dtype=jnp.float32), rhs,
                                      left_side=True, lower=True)
    u, w = jnp.split(sol, [DV], axis=-1)
    qk = jnp.where(incl, jnp.einsum('bhcld,bhcmd->bhclm', q, k) * decay, 0.0)

    def step(state, inp):
        qc, kc, uc, wc, gc, ac = inp
        v_new = uc - jnp.einsum('bhld,bhdv->bhlv', wc, state)
        o = (jnp.einsum('bhld,bhdv->bhlv', qc * jnp.exp(gc)[..., None], state)
             + jnp.einsum('bhlm,bhmv->bhlv', ac, v_new))
        g_last = gc[..., -1:]
        state = (state * jnp.exp(g_last)[..., None]
                 + jnp.einsum('bhld,bhlv->bhdv', kc * jnp.exp(g_last - gc)[..., None], v_new))
        return state, o

    xs = tuple(jnp.moveaxis(t, 2, 0) for t in (q, k, u, w, g, qk))
    _, o = lax.scan(step, jnp.zeros((Bsz, H, DK, DV), jnp.float32), xs)
    return jnp.transpose(o, (1, 0, 3, 2, 4)).reshape(Bsz, S, H, DV)


def gdn_mixer(qkv, z, b_raw, a_raw, conv_w, dt_bias, a_log, norm_g):
    Bsz, S, _ = qkv.shape
    qkv = jax.nn.silu(causal_dwconv(qkv, conv_w))
    q, k, v = jnp.split(qkv, [GDN_QK, 2 * GDN_QK], axis=-1)
    rep = GDN_V_HEADS // GDN_QK_HEADS
    q = jnp.repeat(l2norm(q.reshape(Bsz, S, GDN_QK_HEADS, GDN_HEAD_K)), rep, axis=2)
    k = jnp.repeat(l2norm(k.reshape(Bsz, S, GDN_QK_HEADS, GDN_HEAD_K)), rep, axis=2)
    v = v.reshape(Bsz, S, GDN_V_HEADS, GDN_HEAD_V).astype(jnp.float32)
    beta = jax.nn.sigmoid(b_raw.astype(jnp.float32))
    g = -jnp.exp(a_log.astype(jnp.float32)) * jax.nn.softplus(a_raw.astype(jnp.float32) + dt_bias.astype(jnp.float32))
    o = gated_delta_chunked(q, k, v, g, beta)
    o = rmsnorm(o, norm_g) * jax.nn.silu(z.astype(jnp.float32).reshape(Bsz, S, GDN_V_HEADS, GDN_HEAD_V))
    return o.reshape(Bsz, S, GDN_VW).astype(z.dtype)


def hybrid_layer(x, cos, sin, norm1_g, w_in, ssd_conv_w, ssd_conv_b, ssd_dt_bias, ssd_a_log, ssd_d,
                 ssd_norm_g, mla_q_norm_g, mla_w_uq, mla_kv_norm_g, mla_w_ukv, gdn_conv_w, gdn_dt_bias,
                 gdn_a_log, gdn_norm_g, w_ssd_out, w_mla_out, w_gdn_out, w_out, norm2_g, w_up, w_down):
    xn = rmsnorm(x, norm1_g)
    proj = xn @ w_in
    split_at = [int(i) for i in np.cumsum(IN_SIZES)[:-1]]
    (ssd_z, ssd_xbc, ssd_dt, mla_cq, mla_ckv, mla_kr, gdn_qkv, gdn_z, gdn_b, gdn_a,
     gate_logits) = jnp.split(proj, split_at, axis=-1)
    y_ssd = ssd_mixer(ssd_z, ssd_xbc, ssd_dt, ssd_conv_w, ssd_conv_b, ssd_dt_bias, ssd_a_log, ssd_d, ssd_norm_g)
    y_mla = mla_mixer(mla_cq, mla_ckv, mla_kr, cos, sin, mla_q_norm_g, mla_w_uq, mla_kv_norm_g, mla_w_ukv)
    y_gdn = gdn_mixer(gdn_qkv, gdn_z, gdn_b, gdn_a, gdn_conv_w, gdn_dt_bias, gdn_a_log, gdn_norm_g)
    gates = jax.nn.sigmoid(gate_logits.astype(jnp.float32)).astype(x.dtype)
    g_ssd, g_mla, g_gdn = jnp.split(gates, N_BRANCH, axis=-1)
    mixed = g_ssd * (y_ssd @ w_ssd_out) + g_mla * (y_mla @ w_mla_out) + g_gdn * (y_gdn @ w_gdn_out)
    h = x + mixed @ w_out
    f = jnp.square(jax.nn.relu(rmsnorm(h, norm2_g) @ w_up)) @ w_down
    return h + f


def setup_inputs(seed: int = 0) -> dict:
    key = jax.random.key(seed)
    ks = iter(jax.random.split(key, 40))

    def nrm(shape, scale):
        return jax.random.normal(next(ks), shape, jnp.float32) * scale

    def gain(shape):
        return 1.0 + nrm(shape, 0.02)

    def dt_bias_init(shape):
        dt = jnp.exp(jax.random.uniform(next(ks), shape, jnp.float32, math.log(1e-3), math.log(1e-1)))
        return dt + jnp.log(-jnp.expm1(-dt))

    def a_log_init(shape):
        return jnp.log(jax.random.uniform(next(ks), shape, jnp.float32, 1.0, 16.0))

    x = jax.random.normal(next(ks), (BATCH, SEQ, D_MODEL), jnp.float32)
    offset = jax.random.randint(next(ks), (BATCH, 1), 0, 1024, dtype=jnp.int32)
    positions = offset + jnp.arange(SEQ, dtype=jnp.int32)[None, :]
    return {
        'x': x,
        'positions': positions,
        'norm1_g': gain((DEPTH, D_MODEL)),
        'w_in': nrm((DEPTH, D_MODEL, N_IN), D_MODEL ** -0.5),
        'ssd_conv_w': nrm((DEPTH, SSD_CONV, SSD_XBC), SSD_CONV ** -0.5),
        'ssd_conv_b': nrm((DEPTH, SSD_XBC), 0.01),
        'ssd_dt_bias': dt_bias_init((DEPTH, SSD_HEADS)),
        'ssd_a_log': a_log_init((DEPTH, SSD_HEADS)),
        'ssd_d': gain((DEPTH, SSD_HEADS)),
        'ssd_norm_g': gain((DEPTH, SSD_INNER)),
        'mla_q_norm_g': gain((DEPTH, MLA_Q_LORA)),
        'mla_w_uq': nrm((DEPTH, MLA_Q_LORA, MLA_HEADS * (MLA_NOPE + MLA_ROPE)), MLA_Q_LORA ** -0.5),
        'mla_kv_norm_g': gain((DEPTH, MLA_KV_LORA)),
        'mla_w_ukv': nrm((DEPTH, MLA_KV_LORA, MLA_HEADS * (MLA_NOPE + MLA_V)), MLA_KV_LORA ** -0.5),
        'gdn_conv_w': nrm((DEPTH, GDN_CONV, GDN_QKV), GDN_CONV ** -0.5),
        'gdn_dt_bias': dt_bias_init((DEPTH, GDN_V_HEADS)),
        'gdn_a_log': a_log_init((DEPTH, GDN_V_HEADS)),
        'gdn_norm_g': gain((DEPTH, GDN_HEAD_V)),
        'w_ssd_out': nrm((DEPTH, SSD_INNER, D_MODEL), SSD_INNER ** -0.5),
        'w_mla_out': nrm((DEPTH, MLA_HEADS * MLA_V, D_MODEL), (MLA_HEADS * MLA_V) ** -0.5),
        'w_gdn_out': nrm((DEPTH, GDN_VW, D_MODEL), GDN_VW ** -0.5),
        'w_out': nrm((DEPTH, D_MODEL, D_MODEL), D_MODEL ** -0.5),
        'norm2_g': gain((DEPTH, D_MODEL)),
        'w_up': nrm((DEPTH, D_MODEL, D_FF), D_MODEL ** -0.5),
        'w_down': nrm((DEPTH, D_FF, D_MODEL), D_FF ** -0.5),
        'final_norm_g': gain((D_MODEL,)),
    }


def reference(x, positions, norm1_g, w_in, ssd_conv_w, ssd_conv_b, ssd_dt_bias, ssd_a_log, ssd_d,
              ssd_norm_g, mla_q_norm_g, mla_w_uq, mla_kv_norm_g, mla_w_ukv, gdn_conv_w, gdn_dt_bias,
              gdn_a_log, gdn_norm_g, w_ssd_out, w_mla_out, w_gdn_out, w_out, norm2_g, w_up, w_down,
              final_norm_g):
    cos, sin = rope_tables(positions)
    for l in range(DEPTH):
        x = hybrid_layer(x, cos, sin, norm1_g[l], w_in[l], ssd_conv_w[l], ssd_conv_b[l], ssd_dt_bias[l],
                         ssd_a_log[l], ssd_d[l], ssd_norm_g[l], mla_q_norm_g[l], mla_w_uq[l],
                         mla_kv_norm_g[l], mla_w_ukv[l], gdn_conv_w[l], gdn_dt_bias[l], gdn_a_log[l],
                         gdn_norm_g[l], w_ssd_out[l], w_mla_out[l], w_gdn_out[l], w_out[l], norm2_g[l],
                         w_up[l], w_down[l])
    return rmsnorm(x, final_norm_g)
```

```python
import functools
import math

import jax
import jax.numpy as jnp
import numpy as np
from jax import lax
from jax.experimental import pallas as pl
from jax.experimental.pallas import tpu as pltpu

F32 = jnp.float32
BF16 = jnp.bfloat16
HIGHEST = lax.Precision.HIGHEST

EPS = 1e-6
D_MODEL = 1024
DEPTH = 2
SSD_HEADS = 16
SSD_HEAD_DIM = 64
SSD_GROUPS = 2
SSD_STATE = 128
SSD_INNER = 1024
MLA_HEADS = 8
MLA_NOPE = 128
MLA_ROPE = 64
MLA_V = 128
MLA_Q_LORA = 512
MLA_KV_LORA = 256
ROPE_THETA = 10000.0
GDN_HEAD = 128
GDN_V_HEADS = 8
GDN_QK_HEADS = 4
GDN_QK = GDN_QK_HEADS * GDN_HEAD
GDN_QKV = 2 * GDN_QK + GDN_V_HEADS * GDN_HEAD
D_FF = 4096
CONV_K = 4

CHUNK = 128
HALO = 8
LANES = 128
VMEM_LIMIT = 56 * 1024 * 1024

W_GQKV, W_Z, W_XS, W_GZ, W_BC, W_CQ, W_CKV, W_KR, W_SM = 2048, 1024, 1024, 1024, 512, 512, 256, 128, 128
OFF_GQKV, OFF_Z, OFF_XS, OFF_GZ, OFF_BC, OFF_CQ, OFF_CKV, OFF_KR, OFF_SM = (
    0, 2048, 3072, 4096, 5120, 5632, 6144, 6400, 6528)
N_PROJ = 6656
SM_DT, SM_B, SM_A = 0, 16, 24

NEG_BIG = -1e30


def _rms(x, g):
    return x * lax.rsqrt(jnp.mean(x * x, axis=-1, keepdims=True) + EPS) * g


def _sigmoid(x):
    return 1.0 / (1.0 + jnp.exp(-x))


def _silu(x):
    return x * _sigmoid(x)


def _softplus(x):
    return jnp.maximum(x, 0.0) + jnp.log1p(jnp.exp(-jnp.abs(x)))


def _dot(a, b):
    return jnp.dot(a, b, preferred_element_type=F32)


def _dot_nt(a, b):
    return lax.dot_general(a, b, (((1,), (1,)), ((), ())), preferred_element_type=F32)


def _params(*sem):
    return pltpu.CompilerParams(dimension_semantics=sem, vmem_limit_bytes=VMEM_LIMIT)


def _rope_kernel(pos_ref, inv_ref, cos_ref, sin_ref):
    ang = pos_ref[...].astype(F32) * inv_ref[...]
    cos_ref[...] = jnp.cos(ang)
    sin_ref[...] = jnp.sin(ang)


def _rope_tables(pos_col, inv4):
    T = pos_col.shape[0]
    tm = min(T, 2048)
    return pl.pallas_call(
        _rope_kernel,
        grid=(T // tm,),
        in_specs=[pl.BlockSpec((tm, 1), lambda i: (i, 0)),
                  pl.BlockSpec((1, LANES), lambda i: (0, 0))],
        out_specs=[pl.BlockSpec((tm, LANES), lambda i: (i, 0))] * 2,
        out_shape=[jax.ShapeDtypeStruct((T, LANES), F32)] * 2,
        compiler_params=_params("parallel"),
        name="rope_tables",
    )(pos_col, inv4)


def _inproj_kernel(x_ref, g_ref, w_ref, o_ref, xn_ref):
    @pl.when(pl.program_id(1) == 0)
    def _():
        xn_ref[...] = _rms(x_ref[...], g_ref[...]).astype(BF16)

    o_ref[...] = _dot(xn_ref[...], w_ref[...])


def _inproj(x2, g, w):
    T, D = x2.shape
    N = w.shape[1]
    tm = min(T, 1024)
    tn = 512
    return pl.pallas_call(
        _inproj_kernel,
        grid=(T // tm, N // tn),
        in_specs=[pl.BlockSpec((tm, D), lambda i, j: (i, 0)),
                  pl.BlockSpec((1, D), lambda i, j: (0, 0)),
                  pl.BlockSpec((D, tn), lambda i, j: (0, j))],
        out_specs=pl.BlockSpec((tm, tn), lambda i, j: (i, j)),
        out_shape=jax.ShapeDtypeStruct((T, N), F32),
        scratch_shapes=[pltpu.VMEM((tm, D), BF16)],
        compiler_params=_params("parallel", "arbitrary"),
        name="in_proj",
    )(x2, g, w)


def _causal_conv(buf_ref, cur, w_ref):
    buf_ref[HALO:HALO + CHUNK, :] = cur
    acc = None
    for k in range(CONV_K):
        lo = HALO - (CONV_K - 1) + k
        term = w_ref[k:k + 1, :] * buf_ref[lo:lo + CHUNK, :]
        acc = term if acc is None else acc + term
    buf_ref[0:HALO, :] = buf_ref[CHUNK:CHUNK + HALO, :]
    return acc


def _ssd_kernel(z_ref, xs_ref, bc_ref, sm_ref, cwx_ref, cbx_ref, cwbc_ref, cbbc_ref, dtb_ref,
                alog_ref, dsk_ref, ng_ref, e_ref, o_ref, cx_ref, cbc_ref, h_ref):
    G, N, P = SSD_GROUPS, SSD_STATE, SSD_HEAD_DIM
    HG = SSD_HEADS // G
    GW = HG * P

    @pl.when(pl.program_id(1) == 0)
    def _():
        cx_ref[0:HALO, :] = jnp.zeros((HALO, cx_ref.shape[1]), F32)
        cbc_ref[0:HALO, :] = jnp.zeros((HALO, cbc_ref.shape[1]), F32)
        h_ref[...] = jnp.zeros(h_ref.shape, F32)

    xs = _silu(_causal_conv(cx_ref, xs_ref[...], cwx_ref) + cbx_ref[...])
    bc = _silu(_causal_conv(cbc_ref, bc_ref[...], cwbc_ref) + cbbc_ref[...])

    row = lax.broadcasted_iota(jnp.int32, (CHUNK, CHUNK), 0)
    col = lax.broadcasted_iota(jnp.int32, (CHUNK, CHUNK), 1)
    tril = row >= col

    dt = _softplus(sm_ref[...] + dtb_ref[...])
    dA = dt * (-jnp.exp(alog_ref[...]))
    a_cum = jnp.dot(tril.astype(F32), dA, precision=HIGHEST, preferred_element_type=F32)
    a_cum_t = a_cum.T
    dt_t = dt.T
    e_a = jnp.exp(a_cum)
    sdec = dt * jnp.exp(a_cum[CHUNK - 1:CHUNK, :] - a_cum)
    stack = jnp.concatenate([e_a, sdec], axis=0)
    hi = stack.astype(BF16)
    lo = (stack - hi.astype(F32)).astype(BF16)
    ex = _dot(hi, e_ref[...]) + _dot(lo, e_ref[...])
    e_a_x = ex[0:CHUNK]
    sdec_x = ex[CHUNK:2 * CHUNK]

    lane = lax.broadcasted_iota(jnp.int32, (CHUNK, LANES), 1)
    left = lane < P

    y_parts = []
    for g in range(G):
        gs = slice(g * GW, (g + 1) * GW)
        b_g = bc[:, g * N:(g + 1) * N]
        c_b = bc[:, G * N + g * N:G * N + (g + 1) * N].astype(BF16)
        cb = _dot_nt(c_b, b_g.astype(BF16))
        x_sd = (xs[:, gs] * sdec_x[:, gs]).astype(BF16)
        st = _dot(b_g.T.astype(BF16), x_sd)
        h_prev = h_ref[:, gs]
        y_off = _dot(c_b, h_prev.astype(BF16)) * e_a_x[:, gs]
        h_ref[:, gs] = h_prev * e_a_x[CHUNK - 1:CHUNK, gs] + st
        diag = []
        for pr in range(HG // 2):
            ms = []
            for e in (g * HG + 2 * pr, g * HG + 2 * pr + 1):
                seg = jnp.where(tril, a_cum[:, e:e + 1] - a_cum_t[e:e + 1, :], NEG_BIG)
                ms.append(cb * jnp.exp(seg) * dt_t[e:e + 1, :])
            m2 = jnp.concatenate(ms, axis=1).astype(BF16)
            c0 = g * GW + 2 * pr * P
            xp = xs[:, c0:c0 + 2 * P]
            bd = jnp.concatenate([jnp.where(left, xp, 0.0), jnp.where(left, 0.0, xp)],
                                 axis=0).astype(BF16)
            diag.append(_dot(m2, bd))
        y_parts.append(jnp.concatenate(diag, axis=1) + y_off)
    y = jnp.concatenate(y_parts, axis=1) + dsk_ref[...] * xs
    y = y * _silu(z_ref[...])
    outs = []
    for g in range(G):
        gs = slice(g * GW, (g + 1) * GW)
        outs.append(_rms(y[:, gs], ng_ref[:, gs]))
    o_ref[...] = jnp.concatenate(outs, axis=1).astype(o_ref.dtype)


def _ssd(proj, B, S, cwx, cbx, cwbc, cbbc, dtb, alog, dsk, ng, expand):
    T = B * S
    nc = S // CHUNK
    full = lambda a: pl.BlockSpec(a.shape, lambda b, c: (0,) * a.ndim)
    rows = lambda w, idx: pl.BlockSpec((CHUNK, w), lambda b, c: (b * nc + c, idx))
    consts = [cwx, cbx, cwbc, cbbc, dtb, alog, dsk, ng, expand]
    return pl.pallas_call(
        _ssd_kernel,
        grid=(B, nc),
        in_specs=[rows(W_Z, OFF_Z // W_Z), rows(W_XS, OFF_XS // W_XS), rows(W_BC, OFF_BC // W_BC),
                  rows(W_SM, OFF_SM // W_SM)] + [full(a) for a in consts],
        out_specs=pl.BlockSpec((CHUNK, SSD_INNER), lambda b, c: (b * nc + c, 0)),
        out_shape=jax.ShapeDtypeStruct((T, SSD_INNER), BF16),
        scratch_shapes=[pltpu.VMEM((HALO + CHUNK, W_XS), F32),
                        pltpu.VMEM((HALO + CHUNK, W_BC), F32),
                        pltpu.VMEM((SSD_STATE, SSD_INNER), F32)],
        compiler_params=_params("parallel", "arbitrary"),
        name="ssd_mixer",
    )(proj, proj, proj, proj, *consts)


def _l2norm(x):
    return x * lax.rsqrt(jnp.sum(x * x, axis=-1, keepdims=True) + EPS)


def _split(x):
    hi = x.astype(BF16)
    return hi, (x - hi.astype(F32)).astype(BF16)


def _dot3(a, b):
    (ah, al), (bh, bl) = a, b
    return _dot(jnp.concatenate([ah, al, ah], axis=1), jnp.concatenate([bh, bh, bl], axis=0))


SOLVE_BLOCK_LOG2 = 3


def _block_masks(row, col):
    n_levels = int(math.log2(CHUNK)) - SOLVE_BLOCK_LOG2
    diag = (row >> SOLVE_BLOCK_LOG2) == (col >> SOLVE_BLOCK_LOG2)
    merges = []
    for lv in range(n_levels):
        s = SOLVE_BLOCK_LOG2 + lv
        same = (row >> (s + 1)) == (col >> (s + 1))
        lower_left = jnp.where(same, ((row >> s) & 1) - ((col >> s) & 1), 0) == 1
        merges.append(lower_left)
    return diag, merges


def _unit_lower_inverse(a, eye, diag_mask, merge_masks):
    d = jnp.where(diag_mask, a, 0.0)
    ds = _split(d)
    d2 = _dot3(ds, ds)
    d2s = _split(d2)
    t = eye - d
    t = t + _dot3(_split(t), d2s)
    t = t + _dot3(_split(t), _split(_dot3(d2s, d2s)))
    for m in merge_masks:
        ts = _split(t)
        t = t - _dot3(_split(_dot3(ts, _split(jnp.where(m, a, 0.0)))), ts)
    return t


def _gdn_kernel(qkv_ref, gz_ref, sm_ref, cw_ref, dtb_ref, alog_ref, ng_ref, o_ref, cb_ref, s_ref):
    HD = GDN_HEAD

    @pl.when(pl.program_id(1) == 0)
    def _():
        cb_ref[0:HALO, :] = jnp.zeros((HALO, cb_ref.shape[1]), F32)
        s_ref[...] = jnp.zeros(s_ref.shape, F32)

    qkv = _silu(_causal_conv(cb_ref, qkv_ref[...], cw_ref))

    row = lax.broadcasted_iota(jnp.int32, (CHUNK, CHUNK), 0)
    col = lax.broadcasted_iota(jnp.int32, (CHUNK, CHUNK), 1)
    incl = row >= col
    strict = row > col
    eye = (row == col).astype(F32)

    sm_t = sm_ref[...].T
    beta_t = _sigmoid(sm_t[SM_B:SM_B + GDN_V_HEADS, :])
    g_t = -jnp.exp(alog_ref[...]) * _softplus(sm_t[SM_A:SM_A + GDN_V_HEADS, :] + dtb_ref[...])
    gc_t = jnp.dot(g_t, (row <= col).astype(F32), precision=HIGHEST, preferred_element_type=F32)
    diag_mask, merge_masks = _block_masks(row, col)
    pad = jnp.zeros((CHUNK - 2 * GDN_V_HEADS, CHUNK), F32)
    cols = jnp.concatenate([beta_t, gc_t, pad], axis=0).T

    for j in range(GDN_QK_HEADS):
        q = _l2norm(qkv[:, j * HD:(j + 1) * HD]) * (HD ** -0.5)
        k = _l2norm(qkv[:, GDN_QK + j * HD:GDN_QK + (j + 1) * HD])
        k_t = k.T.astype(BF16)
        qk_raw = _dot(q.astype(BF16), k_t)
        for h in (2 * j, 2 * j + 1):
            b_col = cols[:, h:h + 1]
            g_col = cols[:, GDN_V_HEADS + h:GDN_V_HEADS + h + 1]
            g_row = gc_t[h:h + 1, :]
            g_last = gc_t[h:h + 1, CHUNK - 1:CHUNK]
            decay = jnp.exp(jnp.where(incl, g_col - g_row, NEG_BIG))
            kb = k * b_col
            a_mat = jnp.where(strict, _dot(kb.astype(BF16), k_t) * decay, 0.0)
            qk = qk_raw * decay
            t_inv = _unit_lower_inverse(a_mat, eye, diag_mask, merge_masks)
            v = qkv[:, 2 * GDN_QK + h * HD:2 * GDN_QK + (h + 1) * HD]
            e_g = jnp.exp(g_col)
            rhs = jnp.concatenate([v * b_col, kb * e_g], axis=1)
            sol = _dot3(_split(t_inv), _split(rhs))
            u = sol[:, 0:HD]
            w = sol[:, HD:2 * HD]
            state = s_ref[h]
            ws = _dot(jnp.concatenate([w, q * e_g], axis=0).astype(BF16), state.astype(BF16))
            v_new = u - ws[0:CHUNK]
            o = ws[CHUNK:2 * CHUNK] + _dot(qk.astype(BF16), v_new.astype(BF16))
            k_dec = k * jnp.exp(g_last - g_col)
            s_ref[h] = state * jnp.exp(g_last) + _dot(k_dec.T.astype(BF16), v_new.astype(BF16))
            o = _rms(o, ng_ref[...]) * _silu(gz_ref[:, h * HD:(h + 1) * HD])
            o_ref[:, h * HD:(h + 1) * HD] = o.astype(o_ref.dtype)


def _gdn(proj, B, S, cw, dtb, alog, ng):
    T = B * S
    nc = S // CHUNK
    full = lambda a: pl.BlockSpec(a.shape, lambda b, c: (0,) * a.ndim)
    rows = lambda w, idx: pl.BlockSpec((CHUNK, w), lambda b, c: (b * nc + c, idx))
    consts = [cw, dtb, alog, ng]
    return pl.pallas_call(
        _gdn_kernel,
        grid=(B, nc),
        in_specs=[rows(W_GQKV, OFF_GQKV // W_GQKV), rows(W_GZ, OFF_GZ // W_GZ),
                  rows(W_SM, OFF_SM // W_SM)] + [full(a) for a in consts],
        out_specs=pl.BlockSpec((CHUNK, GDN_V_HEADS * GDN_HEAD), lambda b, c: (b * nc + c, 0)),
        out_shape=jax.ShapeDtypeStruct((T, GDN_V_HEADS * GDN_HEAD), BF16),
        scratch_shapes=[pltpu.VMEM((HALO + CHUNK, GDN_QKV), F32),
                        pltpu.VMEM((GDN_V_HEADS, GDN_HEAD, GDN_HEAD), F32)],
        compiler_params=_params("parallel", "arbitrary"),
        name="gdn_mixer",
    )(proj, proj, proj, *consts)


def _mla_prep_kernel(cq_ref, ckv_ref, kr_ref, cos_ref, sin_ref, qg_ref, kvg_ref, wq_ref, wkv_ref,
                     q_ref, k_ref, v_ref):
    H = MLA_HEADS
    tm = cq_ref.shape[0]
    scale = (MLA_NOPE + MLA_ROPE) ** -0.5
    cos4 = cos_ref[...]
    sin4 = sin_ref[...]
    lane = lax.broadcasted_iota(jnp.int32, (tm, LANES), 1)
    left = lane < MLA_ROPE

    qm = _dot(_rms(cq_ref[...], qg_ref[...]).astype(BF16), wq_ref[...])
    kvm = _dot(_rms(ckv_ref[...], kvg_ref[...]).astype(BF16), wkv_ref[...])

    kr = kr_ref[...]
    k_rope = (kr * jnp.where(left, cos4, sin4)
              + pltpu.roll(kr, MLA_ROPE, axis=1) * jnp.where(left, sin4, cos4)).astype(BF16)

    pe_off = H * MLA_NOPE
    rot_off = pe_off + H * MLA_ROPE
    for h in range(H):
        jb = (h // 2) * LANES
        pe = qm[:, pe_off + jb:pe_off + jb + LANES]
        rot = qm[:, rot_off + jb:rot_off + jb + LANES]
        roped = pe * cos4 + rot * sin4
        mine = left if h % 2 == 0 else jnp.logical_not(left)
        q_h = jnp.concatenate([qm[:, h * MLA_NOPE:(h + 1) * MLA_NOPE], jnp.where(mine, roped, 0.0)], axis=1)
        q_ref[h] = (q_h * scale).astype(BF16)
        base = h * (MLA_NOPE + MLA_V)
        k_ref[h] = jnp.concatenate([kvm[:, base:base + MLA_NOPE].astype(BF16), k_rope], axis=1)
        v_ref[h] = kvm[:, base + MLA_NOPE:base + MLA_NOPE + MLA_V].astype(BF16)


def _mla_prep(proj, cos4, sin4, qg, kvg, wq, wkv):
    T = proj.shape[0]
    H = MLA_HEADS
    tm = min(T, 512)
    full = lambda a: pl.BlockSpec(a.shape, lambda i: (0,) * a.ndim)
    rows = lambda w, idx: pl.BlockSpec((tm, w), lambda i: (i, idx))
    consts = [qg, kvg, wq, wkv]
    dq = 2 * LANES
    return pl.pallas_call(
        _mla_prep_kernel,
        grid=(T // tm,),
        in_specs=[rows(W_CQ, OFF_CQ // W_CQ), rows(W_CKV, OFF_CKV // W_CKV), rows(W_KR, OFF_KR // W_KR),
                  rows(LANES, 0), rows(LANES, 0)] + [full(a) for a in consts],
        out_specs=[pl.BlockSpec((H, tm, dq), lambda i: (0, i, 0)),
                   pl.BlockSpec((H, tm, dq), lambda i: (0, i, 0)),
                   pl.BlockSpec((H, tm, MLA_V), lambda i: (0, i, 0))],
        out_shape=[jax.ShapeDtypeStruct((H, T, dq), BF16),
                   jax.ShapeDtypeStruct((H, T, dq), BF16),
                   jax.ShapeDtypeStruct((H, T, MLA_V), BF16)],
        compiler_params=_params("parallel"),
        name="mla_prep",
    )(proj, proj, proj, cos4, sin4, *consts)


def _attn_kernel(q_ref, k_ref, v_ref, o_ref, m_ref, l_ref, acc_ref):
    tq = q_ref.shape[0]
    qi = pl.program_id(2)
    q = q_ref[...]
    m_ref[...] = jnp.full(m_ref.shape, NEG_BIG, F32)
    l_ref[...] = jnp.zeros(l_ref.shape, F32)
    acc_ref[...] = jnp.zeros(acc_ref.shape, F32)

    def block(ki, masked):
        start = pl.multiple_of(ki * tq, tq)
        s = _dot_nt(q, k_ref[pl.ds(start, tq), :])
        if masked:
            row = lax.broadcasted_iota(jnp.int32, (tq, tq), 0)
            col = lax.broadcasted_iota(jnp.int32, (tq, tq), 1)
            s = jnp.where(row >= col, s, NEG_BIG)
        m_old = m_ref[...]
        m_new = jnp.maximum(m_old, jnp.max(s, axis=-1, keepdims=True))
        p = jnp.exp(s - m_new)
        alpha = jnp.exp(m_old - m_new)
        l_ref[...] = alpha * l_ref[...] + jnp.sum(p, axis=-1, keepdims=True)
        acc_ref[...] = alpha * acc_ref[...] + _dot(p.astype(BF16), v_ref[pl.ds(start, tq), :])
        m_ref[...] = m_new

    def body(ki, carry):
        block(ki, False)
        return carry

    lax.fori_loop(0, qi, body, 0)
    block(qi, True)
    o_ref[...] = (acc_ref[...] / l_ref[...]).astype(o_ref.dtype)


def _attention(q, k, v, B, S):
    H, T, dq = q.shape
    tq = min(S, 512)
    nq = S // tq
    return pl.pallas_call(
        _attn_kernel,
        grid=(B, H, nq),
        in_specs=[pl.BlockSpec((None, tq, dq), lambda b, h, i: (h, b * nq + i, 0)),
                  pl.BlockSpec((None, S, dq), lambda b, h, i: (h, b, 0)),
                  pl.BlockSpec((None, S, MLA_V), lambda b, h, i: (h, b, 0))],
        out_specs=pl.BlockSpec((tq, MLA_V), lambda b, h, i: (b * nq + i, h)),
        out_shape=jax.ShapeDtypeStruct((T, H * MLA_V), BF16),
        scratch_shapes=[pltpu.VMEM((tq, 1), F32), pltpu.VMEM((tq, 1), F32), pltpu.VMEM((tq, MLA_V), F32)],
        compiler_params=_params("parallel", "parallel", "arbitrary"),
        name="mla_attention",
    )(q, k, v)


def _merge_kernel(x_ref, ys_ref, ym_ref, yg_ref, g_ref, wgate_ref, ws_ref, wm_ref, wg_ref, wo_ref, h_ref):
    D = D_MODEL
    x = x_ref[...]
    xn = _rms(x, g_ref[...]).astype(BF16)
    mixed = None
    for idx, (y_ref, w_ref) in enumerate(((ys_ref, ws_ref), (ym_ref, wm_ref), (yg_ref, wg_ref))):
        gate = _sigmoid(_dot(xn, wgate_ref[:, idx * D:(idx + 1) * D]))
        term = gate * _dot(y_ref[...], w_ref[...])
        mixed = term if mixed is None else mixed + term
    h_ref[...] = x + _dot(mixed.astype(BF16), wo_ref[...])


def _merge(x2, ys, ym, yg, g, wgate, ws, wm, wg, wo):
    T, D = x2.shape
    tm = min(T, 256)
    full = lambda a: pl.BlockSpec(a.shape, lambda i: (0,) * a.ndim)
    rows = pl.BlockSpec((tm, D), lambda i: (i, 0))
    consts = [g, wgate, ws, wm, wg, wo]
    return pl.pallas_call(
        _merge_kernel,
        grid=(T // tm,),
        in_specs=[rows] * 4 + [full(a) for a in consts],
        out_specs=rows,
        out_shape=jax.ShapeDtypeStruct((T, D), F32),
        compiler_params=_params("parallel"),
        name="merge",
    )(x2, ys, ym, yg, *consts)


def _ffn_kernel(h_ref, g_ref, wu_ref, wd_ref, fg_ref, o_ref, hn_ref, acc_ref, *, final_norm):
    j = pl.program_id(1)

    @pl.when(j == 0)
    def _():
        hn_ref[...] = _rms(h_ref[...], g_ref[...]).astype(BF16)
        acc_ref[...] = jnp.zeros(acc_ref.shape, F32)

    up = jnp.maximum(_dot(hn_ref[...], wu_ref[...]), 0.0)
    acc_ref[...] += _dot((up * up).astype(BF16), wd_ref[...])

    @pl.when(j == pl.num_programs(1) - 1)
    def _():
        out = h_ref[...] + acc_ref[...]
        if final_norm:
            out = _rms(out, fg_ref[...])
        o_ref[...] = out


def _ffn(h, g, wu, wd, fg, final_norm):
    T, D = h.shape
    F = wu.shape[1]
    tm = min(T, 1024)
    tf = min(F, 1024)
    return pl.pallas_call(
        functools.partial(_ffn_kernel, final_norm=final_norm),
        grid=(T // tm, F // tf),
        in_specs=[pl.BlockSpec((tm, D), lambda i, j: (i, 0)),
                  pl.BlockSpec((1, D), lambda i, j: (0, 0)),
                  pl.BlockSpec((D, tf), lambda i, j: (0, j)),
                  pl.BlockSpec((tf, D), lambda i, j: (j, 0)),
                  pl.BlockSpec((1, D), lambda i, j: (0, 0))],
        out_specs=pl.BlockSpec((tm, D), lambda i, j: (i, 0)),
        out_shape=jax.ShapeDtypeStruct((T, D), F32),
        scratch_shapes=[pltpu.VMEM((tm, D), BF16), pltpu.VMEM((tm, D), F32)],
        compiler_params=_params("parallel", "arbitrary"),
        name="ffn",
    )(h, g, wu, wd, fg)


def _rotate_half_cols(w):
    half = w.shape[-1] // 2
    return jnp.concatenate([-w[..., half:], w[..., :half]], axis=-1)


def _split_in_proj(w_in):
    sizes = (SSD_INNER, SSD_INNER, 2 * SSD_GROUPS * SSD_STATE, SSD_HEADS, MLA_Q_LORA, MLA_KV_LORA, MLA_ROPE,
             GDN_QKV, GDN_V_HEADS * GDN_HEAD, GDN_V_HEADS, GDN_V_HEADS, 3 * D_MODEL)
    offs = np.cumsum((0,) + sizes)
    return [w_in[:, int(offs[i]):int(offs[i + 1])] for i in range(len(sizes))]


def _layer_weights(w_in, w_uq):
    z, xs, bc, dt, cq, ckv, kr, gqkv, gz, gb, ga, gates = _split_in_proj(w_in)
    small = jnp.concatenate([dt, gb, ga, jnp.zeros((D_MODEL, W_SM - 32), F32)], axis=1)
    w_a = jnp.concatenate([gqkv, z, xs, gz, bc, cq, ckv, kr, _rotate_half_cols(kr), small], axis=1)
    H = MLA_HEADS
    wq = w_uq.reshape(MLA_Q_LORA, H, MLA_NOPE + MLA_ROPE)
    nope = wq[:, :, :MLA_NOPE].reshape(MLA_Q_LORA, H * MLA_NOPE)
    pe = wq[:, :, MLA_NOPE:]
    w_q = jnp.concatenate([nope, pe.reshape(MLA_Q_LORA, H * MLA_ROPE),
                           _rotate_half_cols(pe).reshape(MLA_Q_LORA, H * MLA_ROPE)], axis=1)
    return w_a.astype(BF16), gates.astype(BF16), w_q.astype(BF16)


def _pad_lanes(v, n=LANES):
    return jnp.concatenate([v, jnp.zeros((n - v.shape[0],), v.dtype)])[None, :]


def kernel(x, positions, norm1_g, w_in, ssd_conv_w, ssd_conv_b, ssd_dt_bias, ssd_a_log, ssd_d, ssd_norm_g,
           mla_q_norm_g, mla_w_uq, mla_kv_norm_g, mla_w_ukv, gdn_conv_w, gdn_dt_bias, gdn_a_log, gdn_norm_g,
           w_ssd_out, w_mla_out, w_gdn_out, w_out, norm2_g, w_up, w_down, final_norm_g):
    B, S, D = x.shape
    T = B * S
    x2 = x.reshape(T, D)

    inv = ROPE_THETA ** (-jnp.arange(0, MLA_ROPE, 2, dtype=F32) / MLA_ROPE)
    inv4 = jnp.tile(inv, LANES // inv.shape[0])[None, :]
    cos4, sin4 = _rope_tables(positions.reshape(T, 1), inv4)

    expand = (jnp.arange(LANES)[:, None] == (jnp.arange(SSD_INNER)[None, :] // SSD_HEAD_DIM)).astype(BF16)

    for l in range(DEPTH):
        w_a, w_gate, w_q = _layer_weights(w_in[l], mla_w_uq[l])
        proj = _inproj(x2, norm1_g[l][None, :], w_a)

        y_ssd = _ssd(proj, B, S,
                     ssd_conv_w[l][:, :SSD_INNER], ssd_conv_b[l][None, :SSD_INNER],
                     ssd_conv_w[l][:, SSD_INNER:], ssd_conv_b[l][None, SSD_INNER:],
                     _pad_lanes(ssd_dt_bias[l]), _pad_lanes(ssd_a_log[l]),
                     jnp.repeat(ssd_d[l], SSD_HEAD_DIM)[None, :], ssd_norm_g[l][None, :], expand)

        q, k, v = _mla_prep(proj, cos4, sin4, mla_q_norm_g[l][None, :], mla_kv_norm_g[l][None, :],
                            w_q, mla_w_ukv[l].astype(BF16))
        y_mla = _attention(q, k, v, B, S)

        y_gdn = _gdn(proj, B, S, gdn_conv_w[l],
                     jnp.broadcast_to(gdn_dt_bias[l][:, None], (GDN_V_HEADS, CHUNK)),
                     jnp.broadcast_to(gdn_a_log[l][:, None], (GDN_V_HEADS, CHUNK)),
                     gdn_norm_g[l][None, :])

        h = _merge(x2, y_ssd, y_mla, y_gdn, norm1_g[l][None, :], w_gate,
                   w_ssd_out[l].astype(BF16), w_mla_out[l].astype(BF16), w_gdn_out[l].astype(BF16),
                   w_out[l].astype(BF16))
        x2 = _ffn(h, norm2_g[l][None, :], w_up[l].astype(BF16), w_down[l].astype(BF16),
                  final_norm_g[None, :], final_norm=(l == DEPTH - 1))
    return x2.reshape(B, S, D)
```

```python
import functools
import math

import jax
import jax.numpy as jnp
import numpy as np
from jax import lax
from jax.experimental import pallas as pl
from jax.experimental.pallas import tpu as pltpu

F32 = jnp.float32
BF16 = jnp.bfloat16
HIGHEST = lax.Precision.HIGHEST

EPS = 1e-6
D_MODEL = 1024
DEPTH = 2
SSD_HEADS = 16
SSD_HEAD_DIM = 64
SSD_GROUPS = 2
SSD_STATE = 128
SSD_INNER = 1024
MLA_HEADS = 8
MLA_NOPE = 128
MLA_ROPE = 64
MLA_V = 128
MLA_Q_LORA = 512
MLA_KV_LORA = 256
ROPE_THETA = 10000.0
GDN_HEAD = 128
GDN_V_HEADS = 8
GDN_QK_HEADS = 4
GDN_QK = GDN_QK_HEADS * GDN_HEAD
GDN_QKV = 2 * GDN_QK + GDN_V_HEADS * GDN_HEAD
D_FF = 4096
CONV_K = 4

CHUNK = 128
HALO = 8
LANES = 128
VMEM_LIMIT = 56 * 1024 * 1024

W_GQKV, W_Z, W_XS, W_GZ, W_BC, W_CQ, W_CKV, W_KR, W_SM = 2048, 1024, 1024, 1024, 512, 512, 256, 128, 128
OFF_GQKV, OFF_Z, OFF_XS, OFF_GZ, OFF_BC, OFF_CQ, OFF_CKV, OFF_KR, OFF_SM = (
    0, 2048, 3072, 4096, 5120, 5632, 6144, 6400, 6528)
N_PROJ = 6656
SM_DT, SM_B, SM_A = 0, 16, 24

NEG_BIG = -1e30


def _rms(x, g):
    return x * lax.rsqrt(jnp.mean(x * x, axis=-1, keepdims=True) + EPS) * g


def _sigmoid(x):
    return 1.0 / (1.0 + jnp.exp(-x))


def _silu(x):
    return x * _sigmoid(x)


def _softplus(x):
    return jnp.maximum(x, 0.0) + jnp.log1p(jnp.exp(-jnp.abs(x)))


def _dot(a, b):
    return jnp.dot(a, b, preferred_element_type=F32)


def _dot_nt(a, b):
    return lax.dot_general(a, b, (((1,), (1,)), ((), ())), preferred_element_type=F32)


def _params(*sem):
    return pltpu.CompilerParams(dimension_semantics=sem, vmem_limit_bytes=VMEM_LIMIT)


def _rope_kernel(pos_ref, inv_ref, cos_ref, sin_ref):
    ang = pos_ref[...].astype(F32) * inv_ref[...]
    cos_ref[...] = jnp.cos(ang)
    sin_ref[...] = jnp.sin(ang)


def _rope_tables(pos_col, inv4):
    T = pos_col.shape[0]
    tm = min(T, 2048)
    return pl.pallas_call(
        _rope_kernel,
        grid=(T // tm,),
        in_specs=[pl.BlockSpec((tm, 1), lambda i: (i, 0)),
                  pl.BlockSpec((1, LANES), lambda i: (0, 0))],
        out_specs=[pl.BlockSpec((tm, LANES), lambda i: (i, 0))] * 2,
        out_shape=[jax.ShapeDtypeStruct((T, LANES), F32)] * 2,
        compiler_params=_params("parallel"),
        name="rope_tables",
    )(pos_col, inv4)


def _inproj_kernel(x_ref, g_ref, w_ref, o_ref, xn_ref):
    @pl.when(pl.program_id(1) == 0)
    def _():
        xn_ref[...] = _rms(x_ref[...], g_ref[...]).astype(BF16)

    o_ref[...] = _dot(xn_ref[...], w_ref[...])


def _inproj(x2, g, w):
    T, D = x2.shape
    N = w.shape[1]
    tm = min(T, 1024)
    tn = 512
    return pl.pallas_call(
        _inproj_kernel,
        grid=(T // tm, N // tn),
        in_specs=[pl.BlockSpec((tm, D), lambda i, j: (i, 0)),
                  pl.BlockSpec((1, D), lambda i, j: (0, 0)),
                  pl.BlockSpec((D, tn), lambda i, j: (0, j))],
        out_specs=pl.BlockSpec((tm, tn), lambda i, j: (i, j)),
        out_shape=jax.ShapeDtypeStruct((T, N), F32),
        scratch_shapes=[pltpu.VMEM((tm, D), BF16)],
        compiler_params=_params("parallel", "arbitrary"),
        name="in_proj",
    )(x2, g, w)


def _causal_conv(buf_ref, cur, w_ref):
    buf_ref[HALO:HALO + CHUNK, :] = cur
    acc = None
    for k in range(CONV_K):
        lo = HALO - (CONV_K - 1) + k
        term = w_ref[k:k + 1, :] * buf_ref[lo:lo + CHUNK, :]
        acc = term if acc is None else acc + term
    buf_ref[0:HALO, :] = buf_ref[CHUNK:CHUNK + HALO, :]
    return acc


def _ssd_kernel(z_ref, xs_ref, bc_ref, sm_ref, cwx_ref, cbx_ref, cwbc_ref, cbbc_ref, dtb_ref,
                alog_ref, dsk_ref, ng_ref, e_ref, o_ref, cx_ref, cbc_ref, h_ref):
    G, N, P = SSD_GROUPS, SSD_STATE, SSD_HEAD_DIM
    HG = SSD_HEADS // G
    GW = HG * P

    @pl.when(pl.program_id(1) == 0)
    def _():
        cx_ref[0:HALO, :] = jnp.zeros((HALO, cx_ref.shape[1]), F32)
        cbc_ref[0:HALO, :] = jnp.zeros((HALO, cbc_ref.shape[1]), F32)
        h_ref[...] = jnp.zeros(h_ref.shape, F32)

    xs = _silu(_causal_conv(cx_ref, xs_ref[...], cwx_ref) + cbx_ref[...])
    bc = _silu(_causal_conv(cbc_ref, bc_ref[...], cwbc_ref) + cbbc_ref[...])

    row = lax.broadcasted_iota(jnp.int32, (CHUNK, CHUNK), 0)
    col = lax.broadcasted_iota(jnp.int32, (CHUNK, CHUNK), 1)
    tril = row >= col

    dt = _softplus(sm_ref[...] + dtb_ref[...])
    dA = dt * (-jnp.exp(alog_ref[...]))
    a_cum = jnp.dot(tril.astype(F32), dA, precision=HIGHEST, preferred_element_type=F32)
    a_cum_t = a_cum.T
    dt_t = dt.T
    e_a = jnp.exp(a_cum)
    sdec = dt * jnp.exp(a_cum[CHUNK - 1:CHUNK, :] - a_cum)
    stack = jnp.concatenate([e_a, sdec], axis=0)
    hi = stack.astype(BF16)
    lo = (stack - hi.astype(F32)).astype(BF16)
    ex = _dot(hi, e_ref[...]) + _dot(lo, e_ref[...])
    e_a_x = ex[0:CHUNK]
    sdec_x = ex[CHUNK:2 * CHUNK]

    lane = lax.broadcasted_iota(jnp.int32, (CHUNK, LANES), 1)
    left = lane < P

    y_parts = []
    for g in range(G):
        gs = slice(g * GW, (g + 1) * GW)
        b_g = bc[:, g * N:(g + 1) * N]
        c_b = bc[:, G * N + g * N:G * N + (g + 1) * N].astype(BF16)
        cb = _dot_nt(c_b, b_g.astype(BF16))
        x_sd = (xs[:, gs] * sdec_x[:, gs]).astype(BF16)
        st = _dot(b_g.T.astype(BF16), x_sd)
        h_prev = h_ref[:, gs]
        y_off = _dot(c_b, h_prev.astype(BF16)) * e_a_x[:, gs]
        h_ref[:, gs] = h_prev * e_a_x[CHUNK - 1:CHUNK, gs] + st
        diag = []
        for pr in range(HG // 2):
            ms = []
            for e in (g * HG + 2 * pr, g * HG + 2 * pr + 1):
                seg = jnp.where(tril, a_cum[:, e:e + 1] - a_cum_t[e:e + 1, :], NEG_BIG)
                ms.append(cb * jnp.exp(seg) * dt_t[e:e + 1, :])
            m2 = jnp.concatenate(ms, axis=1).astype(BF16)
            c0 = g * GW + 2 * pr * P
            xp = xs[:, c0:c0 + 2 * P]
            bd = jnp.concatenate([jnp.where(left, xp, 0.0), jnp.where(left, 0.0, xp)],
                                 axis=0).astype(BF16)
            diag.append(_dot(m2, bd))
        y_parts.append(jnp.concatenate(diag, axis=1) + y_off)
    y = jnp.concatenate(y_parts, axis=1) + dsk_ref[...] * xs
    y = y * _silu(z_ref[...])
    outs = []
    for g in range(G):
        gs = slice(g * GW, (g + 1) * GW)
        outs.append(_rms(y[:, gs], ng_ref[:, gs]))
    o_ref[...] = jnp.concatenate(outs, axis=1).astype(o_ref.dtype)


def _ssd(proj, B, S, cwx, cbx, cwbc, cbbc, dtb, alog, dsk, ng, expand):
    T = B * S
    nc = S // CHUNK
    full = lambda a: pl.BlockSpec(a.shape, lambda b, c: (0,) * a.ndim)
    rows = lambda w, idx: pl.BlockSpec((CHUNK, w), lambda b, c: (b * nc + c, idx))
    consts = [cwx, cbx, cwbc, cbbc, dtb, alog, dsk, ng, expand]
    return pl.pallas_call(
        _ssd_kernel,
        grid=(B, nc),
        in_specs=[rows(W_Z, OFF_Z // W_Z), rows(W_XS, OFF_XS // W_XS), rows(W_BC, OFF_BC // W_BC),
                  rows(W_SM, OFF_SM // W_SM)] + [full(a) for a in consts],
        out_specs=pl.BlockSpec((CHUNK, SSD_INNER), lambda b, c: (b * nc + c, 0)),
        out_shape=jax.ShapeDtypeStruct((T, SSD_INNER), BF16),
        scratch_shapes=[pltpu.VMEM((HALO + CHUNK, W_XS), F32),
                        pltpu.VMEM((HALO + CHUNK, W_BC), F32),
                        pltpu.VMEM((SSD_STATE, SSD_INNER), F32)],
        compiler_params=_params("parallel", "arbitrary"),
        name="ssd_mixer",
    )(proj, proj, proj, proj, *consts)


def _l2norm(x):
    return x * lax.rsqrt(jnp.sum(x * x, axis=-1, keepdims=True) + EPS)


def _split(x):
    hi = x.astype(BF16)
    return hi, (x - hi.astype(F32)).astype(BF16)


def _dot3(a, b):
    (ah, al), (bh, bl) = a, b
    return _dot(jnp.concatenate([ah, al, ah], axis=1), jnp.concatenate([bh, bh, bl], axis=0))


SOLVE_BLOCK_LOG2 = 3


def _block_masks(row, col):
    n_levels = int(math.log2(CHUNK)) - SOLVE_BLOCK_LOG2
    diag = (row >> SOLVE_BLOCK_LOG2) == (col >> SOLVE_BLOCK_LOG2)
    merges = []
    for lv in range(n_levels):
        s = SOLVE_BLOCK_LOG2 + lv
        same = (row >> (s + 1)) == (col >> (s + 1))
        lower_left = jnp.where(same, ((row >> s) & 1) - ((col >> s) & 1), 0) == 1
        merges.append(lower_left)
    return diag, merges


def _unit_lower_inverses(mats, eye, diag_mask, merge_masks):
    a_sp = [_split(a) for a in mats]

    def as_factor(mask):
        return jnp.where(mask, 1.0, 0.0).astype(BF16)

    def masked(mb, sp):
        return sp[0] * mb, sp[1] * mb

    diag_b = as_factor(diag_mask)
    d1 = [masked(diag_b, sp) for sp in a_sp]
    d2 = [_split(_dot3(x, x)) for x in d1]
    ts = [eye - jnp.where(diag_mask, a, 0.0) for a in mats]
    ts = [t + _dot3(_split(t), x) for t, x in zip(ts, d2)]
    d4 = [_split(_dot3(x, x)) for x in d2]
    ts = [t + _dot3(_split(t), x) for t, x in zip(ts, d4)]
    for m in merge_masks:
        tsp = [_split(t) for t in ts]
        mb = as_factor(m)
        mid = [_split(_dot3(t, masked(mb, sp))) for t, sp in zip(tsp, a_sp)]
        ts = [t - _dot3(x, tp) for t, x, tp in zip(ts, mid, tsp)]
    return ts


def _gdn_kernel(qkv_ref, gz_ref, sm_ref, cw_ref, dtb_ref, alog_ref, ng_ref, o_ref, cb_ref, s_ref):
    HD = GDN_HEAD
    NH = GDN_V_HEADS
    heads = range(NH)

    @pl.when(pl.program_id(1) == 0)
    def _():
        cb_ref[0:HALO, :] = jnp.zeros((HALO, cb_ref.shape[1]), F32)
        s_ref[...] = jnp.zeros(s_ref.shape, F32)

    qkv = _silu(_causal_conv(cb_ref, qkv_ref[...], cw_ref))

    row = lax.broadcasted_iota(jnp.int32, (CHUNK, CHUNK), 0)
    col = lax.broadcasted_iota(jnp.int32, (CHUNK, CHUNK), 1)
    incl = row >= col
    strict = row > col
    eye = (row == col).astype(F32)
    diag_mask, merge_masks = _block_masks(row, col)

    sm_t = sm_ref[...].T
    beta_t = _sigmoid(sm_t[SM_B:SM_B + NH, :])
    g_t = -jnp.exp(alog_ref[...]) * _softplus(sm_t[SM_A:SM_A + NH, :] + dtb_ref[...])
    gc_t = jnp.dot(g_t, (row <= col).astype(F32), precision=HIGHEST, preferred_element_type=F32)
    pad = jnp.zeros((CHUNK - 2 * NH, CHUNK), F32)
    cols = jnp.concatenate([beta_t, gc_t, pad], axis=0).T

    q = [_l2norm(qkv[:, j * HD:(j + 1) * HD]) * (HD ** -0.5) for j in range(GDN_QK_HEADS)]
    k = [_l2norm(qkv[:, GDN_QK + j * HD:GDN_QK + (j + 1) * HD]) for j in range(GDN_QK_HEADS)]
    k_t = [x.T.astype(BF16) for x in k]
    qk_raw = [_dot(a.astype(BF16), b) for a, b in zip(q, k_t)]

    b_col = [cols[:, h:h + 1] for h in heads]
    g_col = [cols[:, NH + h:NH + h + 1] for h in heads]
    g_last = [gc_t[h:h + 1, CHUNK - 1:CHUNK] for h in heads]
    decay = [jnp.exp(jnp.where(incl, g_col[h] - gc_t[h:h + 1, :], NEG_BIG)) for h in heads]
    kb = [k[h // 2] * b_col[h] for h in heads]
    a_mat = [jnp.where(strict, _dot(kb[h].astype(BF16), k_t[h // 2]) * decay[h], 0.0) for h in heads]
    t_inv = _unit_lower_inverses(a_mat, eye, diag_mask, merge_masks)

    e_g = [jnp.exp(g_col[h]) for h in heads]
    v = [qkv[:, 2 * GDN_QK + h * HD:2 * GDN_QK + (h + 1) * HD] for h in heads]
    rhs = [jnp.concatenate([v[h] * b_col[h], kb[h] * e_g[h]], axis=1) for h in heads]
    sol = [_dot3(_split(t_inv[h]), _split(rhs[h])) for h in heads]
    state = [s_ref[h] for h in heads]
    ws = [_dot(jnp.concatenate([sol[h][:, HD:2 * HD], q[h // 2] * e_g[h]], axis=0).astype(BF16),
               state[h].astype(BF16)) for h in heads]
    v_new_b = [(sol[h][:, 0:HD] - ws[h][0:CHUNK]).astype(BF16) for h in heads]
    o = [ws[h][CHUNK:2 * CHUNK] + _dot((qk_raw[h // 2] * decay[h]).astype(BF16), v_new_b[h]) for h in heads]
    k_dec_t = [(k[h // 2] * jnp.exp(g_last[h] - g_col[h])).T.astype(BF16) for h in heads]
    for h in heads:
        s_ref[h] = state[h] * jnp.exp(g_last[h]) + _dot(k_dec_t[h], v_new_b[h])
    for h in heads:
        out = _rms(o[h], ng_ref[...]) * _silu(gz_ref[:, h * HD:(h + 1) * HD])
        o_ref[:, h * HD:(h + 1) * HD] = out.astype(o_ref.dtype)


def _gdn(proj, B, S, cw, dtb, alog, ng):
    T = B * S
    nc = S // CHUNK
    full = lambda a: pl.BlockSpec(a.shape, lambda b, c: (0,) * a.ndim)
    rows = lambda w, idx: pl.BlockSpec((CHUNK, w), lambda b, c: (b * nc + c, idx))
    consts = [cw, dtb, alog, ng]
    return pl.pallas_call(
        _gdn_kernel,
        grid=(B, nc),
        in_specs=[rows(W_GQKV, OFF_GQKV // W_GQKV), rows(W_GZ, OFF_GZ // W_GZ),
                  rows(W_SM, OFF_SM // W_SM)] + [full(a) for a in consts],
        out_specs=pl.BlockSpec((CHUNK, GDN_V_HEADS * GDN_HEAD), lambda b, c: (b * nc + c, 0)),
        out_shape=jax.ShapeDtypeStruct((T, GDN_V_HEADS * GDN_HEAD), BF16),
        scratch_shapes=[pltpu.VMEM((HALO + CHUNK, GDN_QKV), F32),
                        pltpu.VMEM((GDN_V_HEADS, GDN_HEAD, GDN_HEAD), F32)],
        compiler_params=_params("parallel", "arbitrary"),
        name="gdn_mixer",
    )(proj, proj, proj, *consts)


ATTN_BLOCK = 512
LOG2E = math.log2(math.e)
BF16_SUBLANES = 16
V_ROWS = MLA_V + BF16_SUBLANES
HEADS_PER_STEP = 2


def _mla_prep_kernel(cq_ref, ckv_ref, kr_ref, cos_ref, sin_ref, qg_ref, kvg_ref, wq_ref, wkv_ref,
                     qt_ref, k_ref, vt_ref):
    H = MLA_HEADS
    tm = cq_ref.shape[0]
    scale = (MLA_NOPE + MLA_ROPE) ** -0.5 * LOG2E
    cos4 = cos_ref[...]
    sin4 = sin_ref[...]
    lane = lax.broadcasted_iota(jnp.int32, (tm, LANES), 1)
    left = lane < MLA_ROPE

    qm = _dot(_rms(cq_ref[...], qg_ref[...]).astype(BF16), wq_ref[...])
    kvm = _dot(_rms(ckv_ref[...], kvg_ref[...]).astype(BF16), wkv_ref[...])

    kr = kr_ref[...]
    k_rope = (kr * jnp.where(left, cos4, sin4)
              + pltpu.roll(kr, MLA_ROPE, axis=1) * jnp.where(left, sin4, cos4)).astype(BF16)

    pe_off = H * MLA_NOPE
    rot_off = pe_off + H * MLA_ROPE
    for h in range(H):
        jb = (h // 2) * LANES
        pe = qm[:, pe_off + jb:pe_off + jb + LANES]
        rot = qm[:, rot_off + jb:rot_off + jb + LANES]
        roped = pe * cos4 + rot * sin4
        mine = left if h % 2 == 0 else jnp.logical_not(left)
        q_nope = qm[:, h * MLA_NOPE:(h + 1) * MLA_NOPE]
        qt_ref[h, 0, 0:LANES, :] = (q_nope * scale).T.astype(BF16)
        qt_ref[h, 0, LANES:2 * LANES, :] = (jnp.where(mine, roped, 0.0) * scale).T.astype(BF16)
        base = h * (MLA_NOPE + MLA_V)
        k_ref[h] = jnp.concatenate([kvm[:, base:base + MLA_NOPE].astype(BF16), k_rope], axis=1)
        vt_ref[h, 0, 0:MLA_V, :] = kvm[:, base + MLA_NOPE:base + MLA_NOPE + MLA_V].T.astype(BF16)
        ones_row = lax.broadcasted_iota(jnp.int32, (BF16_SUBLANES, tm), 0) == 0
        vt_ref[h, 0, MLA_V:V_ROWS, :] = jnp.where(ones_row, 1.0, 0.0).astype(BF16)


def _mla_prep(proj, cos4, sin4, qg, kvg, wq, wkv):
    T = proj.shape[0]
    H = MLA_HEADS
    tm = ATTN_BLOCK
    nt = T // tm
    full = lambda a: pl.BlockSpec(a.shape, lambda i: (0,) * a.ndim)
    rows = lambda w, idx: pl.BlockSpec((tm, w), lambda i: (i, idx))
    consts = [qg, kvg, wq, wkv]
    dq = 2 * LANES
    return pl.pallas_call(
        _mla_prep_kernel,
        grid=(nt,),
        in_specs=[rows(W_CQ, OFF_CQ // W_CQ), rows(W_CKV, OFF_CKV // W_CKV), rows(W_KR, OFF_KR // W_KR),
                  rows(LANES, 0), rows(LANES, 0)] + [full(a) for a in consts],
        out_specs=[pl.BlockSpec((H, 1, dq, tm), lambda i: (0, i, 0, 0)),
                   pl.BlockSpec((H, tm, dq), lambda i: (0, i, 0)),
                   pl.BlockSpec((H, 1, V_ROWS, tm), lambda i: (0, i, 0, 0))],
        out_shape=[jax.ShapeDtypeStruct((H, nt, dq, tm), BF16),
                   jax.ShapeDtypeStruct((H, T, dq), BF16),
                   jax.ShapeDtypeStruct((H, nt, V_ROWS, tm), BF16)],
        compiler_params=_params("parallel"),
        name="mla_prep",
    )(proj, proj, proj, cos4, sin4, *consts)


def _attn_kernel(qt_ref, k_ref, vt_ref, o_ref, m_ref, acc_ref):
    tb = ATTN_BLOCK
    hs = range(HEADS_PER_STEP)
    qi = pl.program_id(2)
    m_ref[...] = jnp.full(m_ref.shape, NEG_BIG, F32)
    acc_ref[...] = jnp.zeros(acc_ref.shape, F32)

    def block(ki, masked):
        start = pl.multiple_of(ki * tb, tb)
        s = [_dot(k_ref[h, pl.ds(start, tb), :], qt_ref[h, 0]) for h in hs]
        if masked:
            key = lax.broadcasted_iota(jnp.int32, (tb, tb), 0)
            qry = lax.broadcasted_iota(jnp.int32, (tb, tb), 1)
            s = [jnp.where(key <= qry, x, NEG_BIG) for x in s]
        m_old = [m_ref[h] for h in hs]
        m_new = [jnp.maximum(m_old[h], jnp.max(s[h], axis=0, keepdims=True)) for h in hs]
        p = [jnp.exp2(s[h] - m_new[h]).astype(BF16) for h in hs]
        pv = [_dot(vt_ref[h, ki], p[h]) for h in hs]
        for h in hs:
            acc_ref[h] = jnp.exp2(m_old[h] - m_new[h]) * acc_ref[h] + pv[h]
            m_ref[h] = m_new[h]

    def body(ki, carry):
        block(ki, False)
        return carry

    lax.fori_loop(0, qi, body, 0)
    block(qi, True)
    for h in hs:
        acc = acc_ref[h]
        o_ref[:, h * MLA_V:(h + 1) * MLA_V] = (acc[0:MLA_V] / acc[MLA_V:MLA_V + 1]).T.astype(o_ref.dtype)


def _attention(q_t, k, v_t, B, S):
    H, T, dq = k.shape
    tb = ATTN_BLOCK
    nq = S // tb
    hp = HEADS_PER_STEP
    return pl.pallas_call(
        _attn_kernel,
        grid=(B, H // hp, nq),
        in_specs=[pl.BlockSpec((hp, 1, dq, tb), lambda b, h, i: (h, b * nq + i, 0, 0)),
                  pl.BlockSpec((hp, S, dq), lambda b, h, i: (h, b, 0)),
                  pl.BlockSpec((hp, nq, V_ROWS, tb), lambda b, h, i: (h, b, 0, 0))],
        out_specs=pl.BlockSpec((tb, hp * MLA_V), lambda b, h, i: (b * nq + i, h)),
        out_shape=jax.ShapeDtypeStruct((T, H * MLA_V), BF16),
        scratch_shapes=[pltpu.VMEM((hp, 1, tb), F32), pltpu.VMEM((hp, V_ROWS, tb), F32)],
        compiler_params=_params("parallel", "parallel", "arbitrary"),
        name="mla_attention",
    )(q_t, k, v_t)


def _merge_kernel(x_ref, ys_ref, ym_ref, yg_ref, g_ref, wgate_ref, ws_ref, wm_ref, wg_ref, wo_ref, h_ref):
    D = D_MODEL
    x = x_ref[...]
    xn = _rms(x, g_ref[...]).astype(BF16)
    mixed = None
    for idx, (y_ref, w_ref) in enumerate(((ys_ref, ws_ref), (ym_ref, wm_ref), (yg_ref, wg_ref))):
        gate = _sigmoid(_dot(xn, wgate_ref[:, idx * D:(idx + 1) * D]))
        term = gate * _dot(y_ref[...], w_ref[...])
        mixed = term if mixed is None else mixed + term
    h_ref[...] = x + _dot(mixed.astype(BF16), wo_ref[...])


def _merge(x2, ys, ym, yg, g, wgate, ws, wm, wg, wo):
    T, D = x2.shape
    tm = min(T, 256)
    full = lambda a: pl.BlockSpec(a.shape, lambda i: (0,) * a.ndim)
    rows = pl.BlockSpec((tm, D), lambda i: (i, 0))
    consts = [g, wgate, ws, wm, wg, wo]
    return pl.pallas_call(
        _merge_kernel,
        grid=(T // tm,),
        in_specs=[rows] * 4 + [full(a) for a in consts],
        out_specs=rows,
        out_shape=jax.ShapeDtypeStruct((T, D), F32),
        compiler_params=_params("parallel"),
        name="merge",
    )(x2, ys, ym, yg, *consts)


def _ffn_kernel(h_ref, g_ref, wu_ref, wd_ref, fg_ref, o_ref, hn_ref, acc_ref, *, final_norm):
    j = pl.program_id(1)

    @pl.when(j == 0)
    def _():
        hn_ref[...] = _rms(h_ref[...], g_ref[...]).astype(BF16)
        acc_ref[...] = jnp.zeros(acc_ref.shape, F32)

    up = jnp.maximum(_dot(hn_ref[...], wu_ref[...]), 0.0)
    acc_ref[...] += _dot((up * up).astype(BF16), wd_ref[...])

    @pl.when(j == pl.num_programs(1) - 1)
    def _():
        out = h_ref[...] + acc_ref[...]
        if final_norm:
            out = _rms(out, fg_ref[...])
        o_ref[...] = out


def _ffn(h, g, wu, wd, fg, final_norm):
    T, D = h.shape
    F = wu.shape[1]
    tm = min(T, 1024)
    tf = min(F, 1024)
    return pl.pallas_call(
        functools.partial(_ffn_kernel, final_norm=final_norm),
        grid=(T // tm, F // tf),
        in_specs=[pl.BlockSpec((tm, D), lambda i, j: (i, 0)),
                  pl.BlockSpec((1, D), lambda i, j: (0, 0)),
                  pl.BlockSpec((D, tf), lambda i, j: (0, j)),
                  pl.BlockSpec((tf, D), lambda i, j: (j, 0)),
                  pl.BlockSpec((1, D), lambda i, j: (0, 0))],
        out_specs=pl.BlockSpec((tm, D), lambda i, j: (i, 0)),
        out_shape=jax.ShapeDtypeStruct((T, D), F32),
        scratch_shapes=[pltpu.VMEM((tm, D), BF16), pltpu.VMEM((tm, D), F32)],
        compiler_params=_params("parallel", "arbitrary"),
        name="ffn",
    )(h, g, wu, wd, fg)


def _rotate_half_cols(w):
    half = w.shape[-1] // 2
    return jnp.concatenate([-w[..., half:], w[..., :half]], axis=-1)


def _split_in_proj(w_in):
    sizes = (SSD_INNER, SSD_INNER, 2 * SSD_GROUPS * SSD_STATE, SSD_HEADS, MLA_Q_LORA, MLA_KV_LORA, MLA_ROPE,
             GDN_QKV, GDN_V_HEADS * GDN_HEAD, GDN_V_HEADS, GDN_V_HEADS, 3 * D_MODEL)
    offs = np.cumsum((0,) + sizes)
    return [w_in[:, int(offs[i]):int(offs[i + 1])] for i in range(len(sizes))]


def _layer_weights(w_in, w_uq):
    z, xs, bc, dt, cq, ckv, kr, gqkv, gz, gb, ga, gates = _split_in_proj(w_in)
    small = jnp.concatenate([dt, gb, ga, jnp.zeros((D_MODEL, W_SM - 32), F32)], axis=1)
    w_a = jnp.concatenate([gqkv, z, xs, gz, bc, cq, ckv, kr, _rotate_half_cols(kr), small], axis=1)
    H = MLA_HEADS
    wq = w_uq.reshape(MLA_Q_LORA, H, MLA_NOPE + MLA_ROPE)
    nope = wq[:, :, :MLA_NOPE].reshape(MLA_Q_LORA, H * MLA_NOPE)
    pe = wq[:, :, MLA_NOPE:]
    w_q = jnp.concatenate([nope, pe.reshape(MLA_Q_LORA, H * MLA_ROPE),
                           _rotate_half_cols(pe).reshape(MLA_Q_LORA, H * MLA_ROPE)], axis=1)
    return w_a.astype(BF16), gates.astype(BF16), w_q.astype(BF16)


def _pad_lanes(v, n=LANES):
    return jnp.concatenate([v, jnp.zeros((n - v.shape[0],), v.dtype)])[None, :]


def kernel(x, positions, norm1_g, w_in, ssd_conv_w, ssd_conv_b, ssd_dt_bias, ssd_a_log, ssd_d, ssd_norm_g,
           mla_q_norm_g, mla_w_uq, mla_kv_norm_g, mla_w_ukv, gdn_conv_w, gdn_dt_bias, gdn_a_log, gdn_norm_g,
           w_ssd_out, w_mla_out, w_gdn_out, w_out, norm2_g, w_up, w_down, final_norm_g):
    B, S, D = x.shape
    T = B * S
    x2 = x.reshape(T, D)

    inv = ROPE_THETA ** (-jnp.arange(0, MLA_ROPE, 2, dtype=F32) / MLA_ROPE)
    inv4 = jnp.tile(inv, LANES // inv.shape[0])[None, :]
    cos4, sin4 = _rope_tables(positions.reshape(T, 1), inv4)

    expand = (jnp.arange(LANES)[:, None] == (jnp.arange(SSD_INNER)[None, :] // SSD_HEAD_DIM)).astype(BF16)

    for l in range(DEPTH):
        w_a, w_gate, w_q = _layer_weights(w_in[l], mla_w_uq[l])
        proj = _inproj(x2, norm1_g[l][None, :], w_a)

        y_ssd = _ssd(proj, B, S,
                     ssd_conv_w[l][:, :SSD_INNER], ssd_conv_b[l][None, :SSD_INNER],
                     ssd_conv_w[l][:, SSD_INNER:], ssd_conv_b[l][None, SSD_INNER:],
                     _pad_lanes(ssd_dt_bias[l]), _pad_lanes(ssd_a_log[l]),
                     jnp.repeat(ssd_d[l], SSD_HEAD_DIM)[None, :], ssd_norm_g[l][None, :], expand)

        q, k, v = _mla_prep(proj, cos4, sin4, mla_q_norm_g[l][None, :], mla_kv_norm_g[l][None, :],
                            w_q, mla_w_ukv[l].astype(BF16))
        y_mla = _attention(q, k, v, B, S)

        y_gdn = _gdn(proj, B, S, gdn_conv_w[l],
                     jnp.broadcast_to(gdn_dt_bias[l][:, None], (GDN_V_HEADS, CHUNK)),
                     jnp.broadcast_to(gdn_a_log[l][:, None], (GDN_V_HEADS, CHUNK)),
                     gdn_norm_g[l][None, :])

        h = _merge(x2, y_ssd, y_mla, y_gdn, norm1_g[l][None, :], w_gate,
                   w_ssd_out[l].astype(BF16), w_mla_out[l].astype(BF16), w_gdn_out[l].astype(BF16),
                   w_out[l].astype(BF16))
        x2 = _ffn(h, norm2_g[l][None, :], w_up[l].astype(BF16), w_down[l].astype(BF16),
                  final_norm_g[None, :], final_norm=(l == DEPTH - 1))
    return x2.reshape(B, S, D)
```

```python
import functools
import math

import jax
import jax.numpy as jnp
import numpy as np
from jax import lax
from jax.experimental import pallas as pl
from jax.experimental.pallas import tpu as pltpu

F32 = jnp.float32
BF16 = jnp.bfloat16
HIGHEST = lax.Precision.HIGHEST

EPS = 1e-6
D_MODEL = 1024
DEPTH = 2
SSD_HEADS = 16
SSD_HEAD_DIM = 64
SSD_GROUPS = 2
SSD_STATE = 128
SSD_INNER = 1024
MLA_HEADS = 8
MLA_NOPE = 128
MLA_ROPE = 64
MLA_V = 128
MLA_Q_LORA = 512
MLA_KV_LORA = 256
ROPE_THETA = 10000.0
GDN_HEAD = 128
GDN_V_HEADS = 8
GDN_QK_HEADS = 4
GDN_QK = GDN_QK_HEADS * GDN_HEAD
GDN_QKV = 2 * GDN_QK + GDN_V_HEADS * GDN_HEAD
D_FF = 4096
CONV_K = 4

CHUNK = 128
HALO = 8
LANES = 128
VMEM_LIMIT = 56 * 1024 * 1024

W_GQKV, W_Z, W_XS, W_GZ, W_BC, W_CQ, W_CKV, W_KR, W_SM = 2048, 1024, 1024, 1024, 512, 512, 256, 128, 128
OFF_GQKV, OFF_Z, OFF_XS, OFF_GZ, OFF_BC, OFF_CQ, OFF_CKV, OFF_KR, OFF_SM = (
    0, 2048, 3072, 4096, 5120, 5632, 6144, 6400, 6528)
N_PROJ = 6656
SM_DT, SM_B, SM_A = 0, 16, 24

NEG_BIG = -1e30


def _rms(x, g):
    return x * lax.rsqrt(jnp.mean(x * x, axis=-1, keepdims=True) + EPS) * g


def _sigmoid(x):
    return 1.0 / (1.0 + jnp.exp(-x))


def _silu(x):
    return x * _sigmoid(x)


def _softplus(x):
    return jnp.maximum(x, 0.0) + jnp.log1p(jnp.exp(-jnp.abs(x)))


def _dot(a, b):
    return jnp.dot(a, b, preferred_element_type=F32)


def _dot_nt(a, b):
    return lax.dot_general(a, b, (((1,), (1,)), ((), ())), preferred_element_type=F32)


def _params(*sem):
    return pltpu.CompilerParams(dimension_semantics=sem, vmem_limit_bytes=VMEM_LIMIT)


def _rope_kernel(pos_ref, inv_ref, cos_ref, sin_ref):
    ang = pos_ref[...].astype(F32) * inv_ref[...]
    cos_ref[...] = jnp.cos(ang)
    sin_ref[...] = jnp.sin(ang)


def _rope_tables(pos_col, inv4):
    T = pos_col.shape[0]
    tm = min(T, 2048)
    return pl.pallas_call(
        _rope_kernel,
        grid=(T // tm,),
        in_specs=[pl.BlockSpec((tm, 1), lambda i: (i, 0)),
                  pl.BlockSpec((1, LANES), lambda i: (0, 0))],
        out_specs=[pl.BlockSpec((tm, LANES), lambda i: (i, 0))] * 2,
        out_shape=[jax.ShapeDtypeStruct((T, LANES), F32)] * 2,
        compiler_params=_params("parallel"),
        name="rope_tables",
    )(pos_col, inv4)


def _inproj_kernel(x_ref, g_ref, w_ref, o_ref, xn_ref):
    @pl.when(pl.program_id(1) == 0)
    def _():
        xn_ref[...] = _rms(x_ref[...], g_ref[...]).astype(BF16)

    o_ref[...] = _dot(xn_ref[...], w_ref[...])


def _inproj(x2, g, w):
    T, D = x2.shape
    N = w.shape[1]
    tm = min(T, 1024)
    tn = N // 4
    return pl.pallas_call(
        _inproj_kernel,
        grid=(T // tm, N // tn),
        in_specs=[pl.BlockSpec((tm, D), lambda i, j: (i, 0)),
                  pl.BlockSpec((1, D), lambda i, j: (0, 0)),
                  pl.BlockSpec((D, tn), lambda i, j: (0, j))],
        out_specs=pl.BlockSpec((tm, tn), lambda i, j: (i, j)),
        out_shape=jax.ShapeDtypeStruct((T, N), F32),
        scratch_shapes=[pltpu.VMEM((tm, D), BF16)],
        compiler_params=_params("parallel", "arbitrary"),
        name="in_proj",
    )(x2, g, w)


def _causal_conv(buf_ref, cur, w_ref):
    buf_ref[HALO:HALO + CHUNK, :] = cur
    acc = None
    for k in range(CONV_K):
        lo = HALO - (CONV_K - 1) + k
        term = w_ref[k:k + 1, :] * buf_ref[lo:lo + CHUNK, :]
        acc = term if acc is None else acc + term
    buf_ref[0:HALO, :] = buf_ref[CHUNK:CHUNK + HALO, :]
    return acc


def _ssd_kernel(z_ref, xs_ref, bc_ref, sm_ref, cwx_ref, cbx_ref, cwbc_ref, cbbc_ref, dtb_ref,
                alog_ref, dsk_ref, ng_ref, e_ref, o_ref, cx_ref, cbc_ref, h_ref):
    G, N, P = SSD_GROUPS, SSD_STATE, SSD_HEAD_DIM
    HG = SSD_HEADS // G
    GW = HG * P

    @pl.when(pl.program_id(1) == 0)
    def _():
        cx_ref[0:HALO, :] = jnp.zeros((HALO, cx_ref.shape[1]), F32)
        cbc_ref[0:HALO, :] = jnp.zeros((HALO, cbc_ref.shape[1]), F32)
        h_ref[...] = jnp.zeros(h_ref.shape, F32)

    xs = _silu(_causal_conv(cx_ref, xs_ref[...], cwx_ref) + cbx_ref[...])
    bc = _silu(_causal_conv(cbc_ref, bc_ref[...], cwbc_ref) + cbbc_ref[...])

    row = lax.broadcasted_iota(jnp.int32, (CHUNK, CHUNK), 0)
    col = lax.broadcasted_iota(jnp.int32, (CHUNK, CHUNK), 1)
    tril = row >= col

    dt = _softplus(sm_ref[...] + dtb_ref[...])
    dA = dt * (-jnp.exp(alog_ref[...]))
    a_cum = jnp.dot(tril.astype(F32), dA, precision=HIGHEST, preferred_element_type=F32)
    a_cum_t = a_cum.T
    dt_t = dt.T
    e_a = jnp.exp(a_cum)
    sdec = dt * jnp.exp(a_cum[CHUNK - 1:CHUNK, :] - a_cum)
    stack = jnp.concatenate([e_a, sdec], axis=0)
    hi = stack.astype(BF16)
    lo = (stack - hi.astype(F32)).astype(BF16)
    ex = _dot(hi, e_ref[...]) + _dot(lo, e_ref[...])
    e_a_x = ex[0:CHUNK]
    sdec_x = ex[CHUNK:2 * CHUNK]

    lane = lax.broadcasted_iota(jnp.int32, (CHUNK, LANES), 1)
    left = lane < P

    y_parts = []
    for g in range(G):
        gs = slice(g * GW, (g + 1) * GW)
        b_g = bc[:, g * N:(g + 1) * N]
        c_b = bc[:, G * N + g * N:G * N + (g + 1) * N].astype(BF16)
        cb = _dot_nt(c_b, b_g.astype(BF16))
        x_sd = (xs[:, gs] * sdec_x[:, gs]).astype(BF16)
        st = _dot(b_g.T.astype(BF16), x_sd)
        h_prev = h_ref[:, gs]
        y_off = _dot(c_b, h_prev.astype(BF16)) * e_a_x[:, gs]
        h_ref[:, gs] = h_prev * e_a_x[CHUNK - 1:CHUNK, gs] + st
        diag = []
        for pr in range(HG // 2):
            ms = []
            for e in (g * HG + 2 * pr, g * HG + 2 * pr + 1):
                seg = jnp.where(tril, a_cum[:, e:e + 1] - a_cum_t[e:e + 1, :], NEG_BIG)
                ms.append(cb * jnp.exp(seg) * dt_t[e:e + 1, :])
            m2 = jnp.concatenate(ms, axis=1).astype(BF16)
            c0 = g * GW + 2 * pr * P
            xp = xs[:, c0:c0 + 2 * P]
            bd = jnp.concatenate([jnp.where(left, xp, 0.0), jnp.where(left, 0.0, xp)],
                                 axis=0).astype(BF16)
            diag.append(_dot(m2, bd))
        y_parts.append(jnp.concatenate(diag, axis=1) + y_off)
    y = jnp.concatenate(y_parts, axis=1) + dsk_ref[...] * xs
    y = y * _silu(z_ref[...])
    outs = []
    for g in range(G):
        gs = slice(g * GW, (g + 1) * GW)
        outs.append(_rms(y[:, gs], ng_ref[:, gs]))
    o_ref[...] = jnp.concatenate(outs, axis=1).astype(o_ref.dtype)


def _ssd(proj, B, S, cwx, cbx, cwbc, cbbc, dtb, alog, dsk, ng, expand):
    T = B * S
    nc = S // CHUNK
    full = lambda a: pl.BlockSpec(a.shape, lambda b, c: (0,) * a.ndim)
    rows = lambda w, idx: pl.BlockSpec((CHUNK, w), lambda b, c: (b * nc + c, idx))
    consts = [cwx, cbx, cwbc, cbbc, dtb, alog, dsk, ng, expand]
    return pl.pallas_call(
        _ssd_kernel,
        grid=(B, nc),
        in_specs=[rows(W_Z, OFF_Z // W_Z), rows(W_XS, OFF_XS // W_XS), rows(W_BC, OFF_BC // W_BC),
                  rows(W_SM, OFF_SM // W_SM)] + [full(a) for a in consts],
        out_specs=pl.BlockSpec((CHUNK, SSD_INNER), lambda b, c: (b * nc + c, 0)),
        out_shape=jax.ShapeDtypeStruct((T, SSD_INNER), BF16),
        scratch_shapes=[pltpu.VMEM((HALO + CHUNK, W_XS), F32),
                        pltpu.VMEM((HALO + CHUNK, W_BC), F32),
                        pltpu.VMEM((SSD_STATE, SSD_INNER), F32)],
        compiler_params=_params("parallel", "arbitrary"),
        name="ssd_mixer",
    )(proj, proj, proj, proj, *consts)


def _l2norm(x):
    return x * lax.rsqrt(jnp.sum(x * x, axis=-1, keepdims=True) + EPS)


def _split(x):
    hi = x.astype(BF16)
    return hi, (x - hi.astype(F32)).astype(BF16)


def _dot3(a, b):
    (ah, al), (bh, bl) = a, b
    return _dot(jnp.concatenate([ah, al, ah], axis=1), jnp.concatenate([bh, bh, bl], axis=0))


def _dot2(a, b):
    return _dot(jnp.concatenate(a, axis=1), jnp.concatenate([b, b], axis=0))


def _dot2r(a, b):
    return _dot(jnp.concatenate([a, a], axis=1), jnp.concatenate(b, axis=0))


SOLVE_BLOCK_LOG2 = 3


def _block_masks(row, col):
    n_levels = int(math.log2(CHUNK)) - SOLVE_BLOCK_LOG2
    diag = (row >> SOLVE_BLOCK_LOG2) == (col >> SOLVE_BLOCK_LOG2)
    merges = []
    for lv in range(n_levels):
        s = SOLVE_BLOCK_LOG2 + lv
        same = (row >> (s + 1)) == (col >> (s + 1))
        lower_left = jnp.where(same, ((row >> s) & 1) - ((col >> s) & 1), 0) == 1
        merges.append(lower_left)
    return diag, merges


def _unit_lower_inverses(mats, eye, diag_mask, merge_masks):
    def as_factor(mask):
        return jnp.where(mask, 1.0, 0.0).astype(BF16)

    eye_b = eye.astype(BF16)
    diag_b = as_factor(diag_mask)
    d1 = [a * diag_b for a in mats]
    d2 = [_split(_dot(d, d)) for d in d1]
    t0 = [eye_b - d for d in d1]
    ts = [t.astype(F32) + _dot2r(t, x) for t, x in zip(t0, d2)]
    d4 = [_split(_dot3(x, x)) for x in d2]
    ts = [t + _dot3(_split(t), x) for t, x in zip(ts, d4)]
    for m in merge_masks:
        tsp = [_split(t) for t in ts]
        mb = as_factor(m)
        mid = [_split(_dot2(t, a * mb)) for t, a in zip(tsp, mats)]
        ts = [t - _dot3(x, tp) for t, x, tp in zip(ts, mid, tsp)]
    return ts


def _gdn_kernel(qkv_ref, gz_ref, sm_ref, cw_ref, dtb_ref, alog_ref, ng_ref, o_ref, cb_ref, s_ref):
    HD = GDN_HEAD
    NH = GDN_V_HEADS
    heads = range(NH)

    @pl.when(pl.program_id(1) == 0)
    def _():
        cb_ref[0:HALO, :] = jnp.zeros((HALO, cb_ref.shape[1]), F32)
        s_ref[...] = jnp.zeros(s_ref.shape, F32)

    qkv = _silu(_causal_conv(cb_ref, qkv_ref[...], cw_ref))

    row = lax.broadcasted_iota(jnp.int32, (CHUNK, CHUNK), 0)
    col = lax.broadcasted_iota(jnp.int32, (CHUNK, CHUNK), 1)
    incl = row >= col
    strict = row > col
    eye = (row == col).astype(F32)
    diag_mask, merge_masks = _block_masks(row, col)

    sm_t = sm_ref[...].T
    beta_t = _sigmoid(sm_t[SM_B:SM_B + NH, :])
    g_t = -jnp.exp(alog_ref[...]) * _softplus(sm_t[SM_A:SM_A + NH, :] + dtb_ref[...])
    gc_t = jnp.dot(g_t, (row <= col).astype(F32), precision=HIGHEST, preferred_element_type=F32)
    pad = jnp.zeros((CHUNK - 2 * NH, CHUNK), F32)
    cols = jnp.concatenate([beta_t, gc_t, pad], axis=0).T

    q = [_l2norm(qkv[:, j * HD:(j + 1) * HD]) * (HD ** -0.5) for j in range(GDN_QK_HEADS)]
    k = [_l2norm(qkv[:, GDN_QK + j * HD:GDN_QK + (j + 1) * HD]) for j in range(GDN_QK_HEADS)]
    k_t = [x.T.astype(BF16) for x in k]
    qk_raw = [_dot(a.astype(BF16), b) for a, b in zip(q, k_t)]

    b_col = [cols[:, h:h + 1] for h in heads]
    g_col = [cols[:, NH + h:NH + h + 1] for h in heads]
    g_last = [gc_t[h:h + 1, CHUNK - 1:CHUNK] for h in heads]
    decay = [jnp.exp(jnp.where(incl, g_col[h] - gc_t[h:h + 1, :], NEG_BIG)) for h in heads]
    kb = [k[h // 2] * b_col[h] for h in heads]
    a_mat = [jnp.where(strict, _dot(kb[h].astype(BF16), k_t[h // 2]) * decay[h], 0.0).astype(BF16)
             for h in heads]
    t_inv = _unit_lower_inverses(a_mat, eye, diag_mask, merge_masks)

    e_g = [jnp.exp(g_col[h]) for h in heads]
    v = [qkv[:, 2 * GDN_QK + h * HD:2 * GDN_QK + (h + 1) * HD] for h in heads]
    rhs = [jnp.concatenate([v[h] * b_col[h], kb[h] * e_g[h]], axis=1).astype(BF16) for h in heads]
    sol = [_dot2(_split(t_inv[h]), rhs[h]) for h in heads]
    state = [s_ref[h] for h in heads]
    ws = [_dot(jnp.concatenate([sol[h][:, HD:2 * HD], q[h // 2] * e_g[h]], axis=0).astype(BF16),
               state[h].astype(BF16)) for h in heads]
    v_new_b = [(sol[h][:, 0:HD] - ws[h][0:CHUNK]).astype(BF16) for h in heads]
    o = [ws[h][CHUNK:2 * CHUNK] + _dot((qk_raw[h // 2] * decay[h]).astype(BF16), v_new_b[h]) for h in heads]
    k_dec_t = [(k[h // 2] * jnp.exp(g_last[h] - g_col[h])).T.astype(BF16) for h in heads]
    for h in heads:
        s_ref[h] = state[h] * jnp.exp(g_last[h]) + _dot(k_dec_t[h], v_new_b[h])
    for h in heads:
        out = _rms(o[h], ng_ref[...]) * _silu(gz_ref[:, h * HD:(h + 1) * HD])
        o_ref[:, h * HD:(h + 1) * HD] = out.astype(o_ref.dtype)


def _gdn(proj, B, S, cw, dtb, alog, ng):
    T = B * S
    nc = S // CHUNK
    full = lambda a: pl.BlockSpec(a.shape, lambda b, c: (0,) * a.ndim)
    rows = lambda w, idx: pl.BlockSpec((CHUNK, w), lambda b, c: (b * nc + c, idx))
    consts = [cw, dtb, alog, ng]
    return pl.pallas_call(
        _gdn_kernel,
        grid=(B, nc),
        in_specs=[rows(W_GQKV, OFF_GQKV // W_GQKV), rows(W_GZ, OFF_GZ // W_GZ),
                  rows(W_SM, OFF_SM // W_SM)] + [full(a) for a in consts],
        out_specs=pl.BlockSpec((CHUNK, GDN_V_HEADS * GDN_HEAD), lambda b, c: (b * nc + c, 0)),
        out_shape=jax.ShapeDtypeStruct((T, GDN_V_HEADS * GDN_HEAD), BF16),
        scratch_shapes=[pltpu.VMEM((HALO + CHUNK, GDN_QKV), F32),
                        pltpu.VMEM((GDN_V_HEADS, GDN_HEAD, GDN_HEAD), F32)],
        compiler_params=_params("parallel", "arbitrary"),
        name="gdn_mixer",
    )(proj, proj, proj, *consts)


ATTN_BLOCK = 512
LOG2E = math.log2(math.e)
BF16_SUBLANES = 16
V_ROWS = MLA_V + BF16_SUBLANES
HEADS_PER_STEP = 4


def _mla_prep_kernel(cq_ref, ckv_ref, kr_ref, cos_ref, sin_ref, qg_ref, kvg_ref, wq_ref, wkv_ref,
                     qt_ref, k_ref, vt_ref):
    H = MLA_HEADS
    tm = cq_ref.shape[0]
    scale = (MLA_NOPE + MLA_ROPE) ** -0.5 * LOG2E
    cos4 = cos_ref[...]
    sin4 = sin_ref[...]
    lane = lax.broadcasted_iota(jnp.int32, (tm, LANES), 1)
    left = lane < MLA_ROPE

    qm = _dot(_rms(cq_ref[...], qg_ref[...]).astype(BF16), wq_ref[...])
    kvm = _dot(_rms(ckv_ref[...], kvg_ref[...]).astype(BF16), wkv_ref[...])

    kr = kr_ref[...]
    k_rope = (kr * jnp.where(left, cos4, sin4)
              + pltpu.roll(kr, MLA_ROPE, axis=1) * jnp.where(left, sin4, cos4)).astype(BF16)

    pe_off = H * MLA_NOPE
    rot_off = pe_off + H * MLA_ROPE
    for h in range(H):
        jb = (h // 2) * LANES
        pe = qm[:, pe_off + jb:pe_off + jb + LANES]
        rot = qm[:, rot_off + jb:rot_off + jb + LANES]
        roped = pe * cos4 + rot * sin4
        mine = left if h % 2 == 0 else jnp.logical_not(left)
        q_nope = qm[:, h * MLA_NOPE:(h + 1) * MLA_NOPE]
        qt_ref[h, 0, 0:LANES, :] = (q_nope * scale).T.astype(BF16)
        qt_ref[h, 0, LANES:2 * LANES, :] = (jnp.where(mine, roped, 0.0) * scale).T.astype(BF16)
        base = h * (MLA_NOPE + MLA_V)
        k_ref[h] = jnp.concatenate([kvm[:, base:base + MLA_NOPE].astype(BF16), k_rope], axis=1)
        vt_ref[h, 0, 0:MLA_V, :] = kvm[:, base + MLA_NOPE:base + MLA_NOPE + MLA_V].T.astype(BF16)
        ones_row = lax.broadcasted_iota(jnp.int32, (BF16_SUBLANES, tm), 0) == 0
        vt_ref[h, 0, MLA_V:V_ROWS, :] = jnp.where(ones_row, 1.0, 0.0).astype(BF16)


def _mla_prep(proj, cos4, sin4, qg, kvg, wq, wkv):
    T = proj.shape[0]
    H = MLA_HEADS
    tm = ATTN_BLOCK
    nt = T // tm
    full = lambda a: pl.BlockSpec(a.shape, lambda i: (0,) * a.ndim)
    rows = lambda w, idx: pl.BlockSpec((tm, w), lambda i: (i, idx))
    consts = [qg, kvg, wq, wkv]
    dq = 2 * LANES
    return pl.pallas_call(
        _mla_prep_kernel,
        grid=(nt,),
        in_specs=[rows(W_CQ, OFF_CQ // W_CQ), rows(W_CKV, OFF_CKV // W_CKV), rows(W_KR, OFF_KR // W_KR),
                  rows(LANES, 0), rows(LANES, 0)] + [full(a) for a in consts],
        out_specs=[pl.BlockSpec((H, 1, dq, tm), lambda i: (0, i, 0, 0)),
                   pl.BlockSpec((H, tm, dq), lambda i: (0, i, 0)),
                   pl.BlockSpec((H, 1, V_ROWS, tm), lambda i: (0, i, 0, 0))],
        out_shape=[jax.ShapeDtypeStruct((H, nt, dq, tm), BF16),
                   jax.ShapeDtypeStruct((H, T, dq), BF16),
                   jax.ShapeDtypeStruct((H, nt, V_ROWS, tm), BF16)],
        compiler_params=_params("parallel"),
        name="mla_prep",
    )(proj, proj, proj, cos4, sin4, *consts)


def _attn_kernel(qt_ref, k_ref, vt_ref, o_ref, m_ref, acc_ref):
    tb = ATTN_BLOCK
    hs = range(HEADS_PER_STEP)
    qi = pl.program_id(2)
    m_ref[...] = jnp.full(m_ref.shape, NEG_BIG, F32)
    acc_ref[...] = jnp.zeros(acc_ref.shape, F32)

    def block(ki, masked):
        start = pl.multiple_of(ki * tb, tb)
        s = [_dot(k_ref[h, pl.ds(start, tb), :], qt_ref[h, 0]) for h in hs]
        if masked:
            key = lax.broadcasted_iota(jnp.int32, (tb, tb), 0)
            qry = lax.broadcasted_iota(jnp.int32, (tb, tb), 1)
            s = [jnp.where(key <= qry, x, NEG_BIG) for x in s]
        m_old = [m_ref[h] for h in hs]
        m_new = [jnp.maximum(m_old[h], jnp.max(s[h], axis=0, keepdims=True)) for h in hs]
        p = [jnp.exp2(s[h] - m_new[h]).astype(BF16) for h in hs]
        pv = [_dot(vt_ref[h, ki], p[h]) for h in hs]
        for h in hs:
            acc_ref[h] = jnp.exp2(m_old[h] - m_new[h]) * acc_ref[h] + pv[h]
            m_ref[h] = m_new[h]

    def body(ki, carry):
        block(ki, False)
        return carry

    lax.fori_loop(0, qi, body, 0)
    block(qi, True)
    for h in hs:
        acc = acc_ref[h]
        o_ref[:, h * MLA_V:(h + 1) * MLA_V] = (acc[0:MLA_V] / acc[MLA_V:MLA_V + 1]).T.astype(o_ref.dtype)


def _attention(q_t, k, v_t, B, S):
    H, T, dq = k.shape
    tb = ATTN_BLOCK
    nq = S // tb
    hp = HEADS_PER_STEP
    return pl.pallas_call(
        _attn_kernel,
        grid=(B, H // hp, nq),
        in_specs=[pl.BlockSpec((hp, 1, dq, tb), lambda b, h, i: (h, b * nq + i, 0, 0)),
                  pl.BlockSpec((hp, S, dq), lambda b, h, i: (h, b, 0)),
                  pl.BlockSpec((hp, nq, V_ROWS, tb), lambda b, h, i: (h, b, 0, 0))],
        out_specs=pl.BlockSpec((tb, hp * MLA_V), lambda b, h, i: (b * nq + i, h)),
        out_shape=jax.ShapeDtypeStruct((T, H * MLA_V), BF16),
        scratch_shapes=[pltpu.VMEM((hp, 1, tb), F32), pltpu.VMEM((hp, V_ROWS, tb), F32)],
        compiler_params=_params("parallel", "parallel", "arbitrary"),
        name="mla_attention",
    )(q_t, k, v_t)


def _merge_kernel(x_ref, ys_ref, ym_ref, yg_ref, g_ref, wgate_ref, ws_ref, wm_ref, wg_ref, wo_ref, h_ref):
    D = D_MODEL
    x = x_ref[...]
    xn = _rms(x, g_ref[...]).astype(BF16)
    mixed = None
    for idx, (y_ref, w_ref) in enumerate(((ys_ref, ws_ref), (ym_ref, wm_ref), (yg_ref, wg_ref))):
        gate = _sigmoid(_dot(xn, wgate_ref[:, idx * D:(idx + 1) * D]))
        term = gate * _dot(y_ref[...], w_ref[...])
        mixed = term if mixed is None else mixed + term
    h_ref[...] = x + _dot(mixed.astype(BF16), wo_ref[...])


def _merge(x2, ys, ym, yg, g, wgate, ws, wm, wg, wo):
    T, D = x2.shape
    tm = min(T, 256)
    full = lambda a: pl.BlockSpec(a.shape, lambda i: (0,) * a.ndim)
    rows = pl.BlockSpec((tm, D), lambda i: (i, 0))
    consts = [g, wgate, ws, wm, wg, wo]
    return pl.pallas_call(
        _merge_kernel,
        grid=(T // tm,),
        in_specs=[rows] * 4 + [full(a) for a in consts],
        out_specs=rows,
        out_shape=jax.ShapeDtypeStruct((T, D), F32),
        compiler_params=_params("parallel"),
        name="merge",
    )(x2, ys, ym, yg, *consts)


def _ffn_kernel(h_ref, g_ref, wu_ref, wd_ref, fg_ref, o_ref, hn_ref, acc_ref, *, final_norm):
    j = pl.program_id(1)

    @pl.when(j == 0)
    def _():
        hn_ref[...] = _rms(h_ref[...], g_ref[...]).astype(BF16)
        acc_ref[...] = jnp.zeros(acc_ref.shape, F32)

    up = jnp.maximum(_dot(hn_ref[...], wu_ref[...]), 0.0)
    acc_ref[...] += _dot((up * up).astype(BF16), wd_ref[...])

    @pl.when(j == pl.num_programs(1) - 1)
    def _():
        out = h_ref[...] + acc_ref[...]
        if final_norm:
            out = _rms(out, fg_ref[...])
        o_ref[...] = out


def _ffn(h, g, wu, wd, fg, final_norm):
    T, D = h.shape
    F = wu.shape[1]
    tm = min(T, 1024)
    tf = min(F, 1024)
    return pl.pallas_call(
        functools.partial(_ffn_kernel, final_norm=final_norm),
        grid=(T // tm, F // tf),
        in_specs=[pl.BlockSpec((tm, D), lambda i, j: (i, 0)),
                  pl.BlockSpec((1, D), lambda i, j: (0, 0)),
                  pl.BlockSpec((D, tf), lambda i, j: (0, j)),
                  pl.BlockSpec((tf, D), lambda i, j: (j, 0)),
                  pl.BlockSpec((1, D), lambda i, j: (0, 0))],
        out_specs=pl.BlockSpec((tm, D), lambda i, j: (i, 0)),
        out_shape=jax.ShapeDtypeStruct((T, D), F32),
        scratch_shapes=[pltpu.VMEM((tm, D), BF16), pltpu.VMEM((tm, D), F32)],
        compiler_params=_params("parallel", "arbitrary"),
        name="ffn",
    )(h, g, wu, wd, fg)


def _rotate_half_cols(w):
    half = w.shape[-1] // 2
    return jnp.concatenate([-w[..., half:], w[..., :half]], axis=-1)


def _split_in_proj(w_in):
    sizes = (SSD_INNER, SSD_INNER, 2 * SSD_GROUPS * SSD_STATE, SSD_HEADS, MLA_Q_LORA, MLA_KV_LORA, MLA_ROPE,
             GDN_QKV, GDN_V_HEADS * GDN_HEAD, GDN_V_HEADS, GDN_V_HEADS, 3 * D_MODEL)
    offs = np.cumsum((0,) + sizes)
    return [w_in[:, int(offs[i]):int(offs[i + 1])] for i in range(len(sizes))]


def _layer_weights(w_in, w_uq):
    z, xs, bc, dt, cq, ckv, kr, gqkv, gz, gb, ga, gates = _split_in_proj(w_in)
    small = jnp.concatenate([dt, gb, ga, jnp.zeros((D_MODEL, W_SM - 32), F32)], axis=1)
    w_a = jnp.concatenate([gqkv, z, xs, gz, bc, cq, ckv, kr, _rotate_half_cols(kr), small], axis=1)
    H = MLA_HEADS
    wq = w_uq.reshape(MLA_Q_LORA, H, MLA_NOPE + MLA_ROPE)
    nope = wq[:, :, :MLA_NOPE].reshape(MLA_Q_LORA, H * MLA_NOPE)
    pe = wq[:, :, MLA_NOPE:]
    w_q = jnp.concatenate([nope, pe.reshape(MLA_Q_LORA, H * MLA_ROPE),
                           _rotate_half_cols(pe).reshape(MLA_Q_LORA, H * MLA_ROPE)], axis=1)
    return w_a.astype(BF16), gates.astype(BF16), w_q.astype(BF16)


def _pad_lanes(v, n=LANES):
    return jnp.concatenate([v, jnp.zeros((n - v.shape[0],), v.dtype)])[None, :]


def kernel(x, positions, norm1_g, w_in, ssd_conv_w, ssd_conv_b, ssd_dt_bias, ssd_a_log, ssd_d, ssd_norm_g,
           mla_q_norm_g, mla_w_uq, mla_kv_norm_g, mla_w_ukv, gdn_conv_w, gdn_dt_bias, gdn_a_log, gdn_norm_g,
           w_ssd_out, w_mla_out, w_gdn_out, w_out, norm2_g, w_up, w_down, final_norm_g):
    B, S, D = x.shape
    T = B * S
    x2 = x.reshape(T, D)

    inv = ROPE_THETA ** (-jnp.arange(0, MLA_ROPE, 2, dtype=F32) / MLA_ROPE)
    inv4 = jnp.tile(inv, LANES // inv.shape[0])[None, :]
    cos4, sin4 = _rope_tables(positions.reshape(T, 1), inv4)

    expand = (jnp.arange(LANES)[:, None] == (jnp.arange(SSD_INNER)[None, :] // SSD_HEAD_DIM)).astype(BF16)

    for l in range(DEPTH):
        w_a, w_gate, w_q = _layer_weights(w_in[l], mla_w_uq[l])
        proj = _inproj(x2, norm1_g[l][None, :], w_a)

        y_ssd = _ssd(proj, B, S,
                     ssd_conv_w[l][:, :SSD_INNER], ssd_conv_b[l][None, :SSD_INNER],
                     ssd_conv_w[l][:, SSD_INNER:], ssd_conv_b[l][None, SSD_INNER:],
                     _pad_lanes(ssd_dt_bias[l]), _pad_lanes(ssd_a_log[l]),
                     jnp.repeat(ssd_d[l], SSD_HEAD_DIM)[None, :], ssd_norm_g[l][None, :], expand)

        q, k, v = _mla_prep(proj, cos4, sin4, mla_q_norm_g[l][None, :], mla_kv_norm_g[l][None, :],
                            w_q, mla_w_ukv[l].astype(BF16))
        y_mla = _attention(q, k, v, B, S)

        y_gdn = _gdn(proj, B, S, gdn_conv_w[l],
                     jnp.broadcast_to(gdn_dt_bias[l][:, None], (GDN_V_HEADS, CHUNK)),
                     jnp.broadcast_to(gdn_a_log[l][:, None], (GDN_V_HEADS, CHUNK)),
                     gdn_norm_g[l][None, :])

        h = _merge(x2, y_ssd, y_mla, y_gdn, norm1_g[l][None, :], w_gate,
                   w_ssd_out[l].astype(BF16), w_mla_out[l].astype(BF16), w_gdn_out[l].astype(BF16),
                   w_out[l].astype(BF16))
        x2 = _ffn(h, norm2_g[l][None, :], w_up[l].astype(BF16), w_down[l].astype(BF16),
                  final_norm_g[None, :], final_norm=(l == DEPTH - 1))
    return x2.reshape(B, S, D)
```

```python
import functools
import math

import jax
import jax.numpy as jnp
import numpy as np
from jax import lax
from jax.experimental import pallas as pl
from jax.experimental.pallas import tpu as pltpu

F32 = jnp.float32
BF16 = jnp.bfloat16
HIGHEST = lax.Precision.HIGHEST

EPS = 1e-6
D_MODEL = 1024
DEPTH = 2
SSD_HEADS = 16
SSD_HEAD_DIM = 64
SSD_GROUPS = 2
SSD_STATE = 128
SSD_INNER = 1024
MLA_HEADS = 8
MLA_NOPE = 128
MLA_ROPE = 64
MLA_V = 128
MLA_Q_LORA = 512
MLA_KV_LORA = 256
ROPE_THETA = 10000.0
GDN_HEAD = 128
GDN_V_HEADS = 8
GDN_QK_HEADS = 4
GDN_QK = GDN_QK_HEADS * GDN_HEAD
GDN_QKV = 2 * GDN_QK + GDN_V_HEADS * GDN_HEAD
D_FF = 4096
CONV_K = 4

CHUNK = 128
HALO = 16
LANES = 128
VMEM_LIMIT = 56 * 1024 * 1024

W_GQKV, W_Z, W_XS, W_GZ, W_BC, W_CQ, W_CKV, W_KR, W_SM = 2048, 1024, 1024, 1024, 512, 512, 256, 128, 128
OFF_GQKV, OFF_Z, OFF_XS, OFF_GZ, OFF_BC, OFF_CQ, OFF_CKV, OFF_KR, OFF_SM = (
    0, 2048, 3072, 4096, 5120, 5632, 6144, 6400, 6528)
N_PROJ = 6656
SM_DT, SM_B, SM_A = 0, 16, 24

NEG_BIG = -1e30


def _rms(x, g):
    return x * lax.rsqrt(jnp.mean(x * x, axis=-1, keepdims=True) + EPS) * g


def _sigmoid(x):
    return 1.0 / (1.0 + jnp.exp(-x))


def _silu(x):
    h = 0.5 * x
    return h + h * jnp.tanh(h)


def _softplus(x):
    return jnp.maximum(x, 0.0) + jnp.log1p(jnp.exp(-jnp.abs(x)))


def _dot(a, b):
    return jnp.dot(a, b, preferred_element_type=F32)


def _dot_nt(a, b):
    return lax.dot_general(a, b, (((1,), (1,)), ((), ())), preferred_element_type=F32)


def _params(*sem):
    return pltpu.CompilerParams(dimension_semantics=sem, vmem_limit_bytes=VMEM_LIMIT)


def _rope_kernel(pos_ref, inv_ref, cos_ref, sin_ref):
    ang = pos_ref[...].astype(F32) * inv_ref[...]
    cos_ref[...] = jnp.cos(ang)
    sin_ref[...] = jnp.sin(ang)


def _rope_tables(pos_col, inv4):
    T = pos_col.shape[0]
    tm = min(T, 2048)
    return pl.pallas_call(
        _rope_kernel,
        grid=(T // tm,),
        in_specs=[pl.BlockSpec((tm, 1), lambda i: (i, 0)),
                  pl.BlockSpec((1, LANES), lambda i: (0, 0))],
        out_specs=[pl.BlockSpec((tm, LANES), lambda i: (i, 0))] * 2,
        out_shape=[jax.ShapeDtypeStruct((T, LANES), F32)] * 2,
        compiler_params=_params("parallel"),
        name="rope_tables",
    )(pos_col, inv4)


def _inproj_kernel(x_ref, g_ref, w_ref, o_ref, xn_ref):
    @pl.when(pl.program_id(1) == 0)
    def _():
        xn_ref[...] = _rms(x_ref[...], g_ref[...]).astype(BF16)

    o_ref[...] = _dot(xn_ref[...], w_ref[...])


def _inproj(x2, g, w):
    T, D = x2.shape
    N = w.shape[1]
    tm = min(T, 1024)
    tn = N // 4
    return pl.pallas_call(
        _inproj_kernel,
        grid=(T // tm, N // tn),
        in_specs=[pl.BlockSpec((tm, D), lambda i, j: (i, 0)),
                  pl.BlockSpec((1, D), lambda i, j: (0, 0)),
                  pl.BlockSpec((D, tn), lambda i, j: (0, j))],
        out_specs=pl.BlockSpec((tm, tn), lambda i, j: (i, j)),
        out_shape=jax.ShapeDtypeStruct((T, N), F32),
        scratch_shapes=[pltpu.VMEM((tm, D), BF16)],
        compiler_params=_params("parallel", "arbitrary"),
        name="in_proj",
    )(x2, g, w)


def _shift_matrix():
    r = np.arange((CONV_K - 1) * CHUNK)[:, None]
    c = np.arange(HALO + CHUNK)[None, :]
    return jnp.asarray(c == HALO + (r % CHUNK) - (r // CHUNK + 1), BF16)


def _causal_conv(tail_ref, cur, w_ref, shift_ref):
    cur_b = cur.astype(BF16)
    delayed = _dot(shift_ref[...], jnp.concatenate([tail_ref[...], cur_b], axis=0))
    tail_ref[...] = cur_b[CHUNK - HALO:CHUNK, :]
    acc = w_ref[CONV_K - 1:CONV_K, :] * cur
    for j in range(1, CONV_K):
        acc = acc + w_ref[CONV_K - 1 - j:CONV_K - j, :] * delayed[(j - 1) * CHUNK:j * CHUNK]
    return acc


def _ssd_kernel(z_ref, xs_ref, bc_ref, sm_ref, cw_ref, cb_ref, shift_ref, dtb_ref,
                alog_ref, dsk_ref, ng_ref, e_ref, o_ref, tail_ref, h_ref):
    G, N, P = SSD_GROUPS, SSD_STATE, SSD_HEAD_DIM
    HG = SSD_HEADS // G
    GW = HG * P

    @pl.when(pl.program_id(1) == 0)
    def _():
        tail_ref[...] = jnp.zeros(tail_ref.shape, tail_ref.dtype)
        h_ref[...] = jnp.zeros(h_ref.shape, F32)

    raw = jnp.concatenate([xs_ref[...], bc_ref[...]], axis=1)
    xbc = _silu(_causal_conv(tail_ref, raw, cw_ref, shift_ref) + cb_ref[...])
    xs = xbc[:, 0:SSD_INNER]
    bc = xbc[:, SSD_INNER:]

    row = lax.broadcasted_iota(jnp.int32, (CHUNK, CHUNK), 0)
    col = lax.broadcasted_iota(jnp.int32, (CHUNK, CHUNK), 1)
    tril = row >= col

    dt = _softplus(sm_ref[...] + dtb_ref[...])
    dA = dt * (-jnp.exp(alog_ref[...]))
    a_cum = jnp.dot(tril.astype(F32), dA, precision=HIGHEST, preferred_element_type=F32)
    a_cum_t = a_cum.T
    dt_t = dt.T
    e_a = jnp.exp(a_cum)
    sdec = dt * jnp.exp(a_cum[CHUNK - 1:CHUNK, :] - a_cum)
    stack = jnp.concatenate([e_a, sdec], axis=0)
    hi = stack.astype(BF16)
    lo = (stack - hi.astype(F32)).astype(BF16)
    ex = _dot(hi, e_ref[...]) + _dot(lo, e_ref[...])
    e_a_x = ex[0:CHUNK]
    sdec_x = ex[CHUNK:2 * CHUNK]

    lane = lax.broadcasted_iota(jnp.int32, (CHUNK, LANES), 1)
    left = lane < P

    y_parts = []
    for g in range(G):
        gs = slice(g * GW, (g + 1) * GW)
        b_g = bc[:, g * N:(g + 1) * N]
        c_b = bc[:, G * N + g * N:G * N + (g + 1) * N].astype(BF16)
        cb = _dot_nt(c_b, b_g.astype(BF16))
        x_sd = (xs[:, gs] * sdec_x[:, gs]).astype(BF16)
        st = _dot(b_g.T.astype(BF16), x_sd)
        h_prev = h_ref[:, gs]
        y_off = _dot(c_b, h_prev.astype(BF16)) * e_a_x[:, gs]
        h_ref[:, gs] = h_prev * e_a_x[CHUNK - 1:CHUNK, gs] + st
        diag = []
        for pr in range(HG // 2):
            ms = []
            for e in (g * HG + 2 * pr, g * HG + 2 * pr + 1):
                seg = jnp.where(tril, a_cum[:, e:e + 1] - a_cum_t[e:e + 1, :], NEG_BIG)
                ms.append(cb * jnp.exp(seg) * dt_t[e:e + 1, :])
            m2 = jnp.concatenate(ms, axis=1).astype(BF16)
            c0 = g * GW + 2 * pr * P
            xp = xs[:, c0:c0 + 2 * P]
            bd = jnp.concatenate([jnp.where(left, xp, 0.0), jnp.where(left, 0.0, xp)],
                                 axis=0).astype(BF16)
            diag.append(_dot(m2, bd))
        y_parts.append(jnp.concatenate(diag, axis=1) + y_off)
    y = jnp.concatenate(y_parts, axis=1) + dsk_ref[...] * xs
    y = y * _silu(z_ref[...])
    outs = []
    for g in range(G):
        gs = slice(g * GW, (g + 1) * GW)
        outs.append(_rms(y[:, gs], ng_ref[:, gs]))
    o_ref[...] = jnp.concatenate(outs, axis=1).astype(o_ref.dtype)


def _ssd(proj, B, S, cw, cb, shift, dtb, alog, dsk, ng, expand):
    T = B * S
    nc = S // CHUNK
    full = lambda a: pl.BlockSpec(a.shape, lambda b, c: (0,) * a.ndim)
    rows = lambda w, idx: pl.BlockSpec((CHUNK, w), lambda b, c: (b * nc + c, idx))
    consts = [cw, cb, shift, dtb, alog, dsk, ng, expand]
    return pl.pallas_call(
        _ssd_kernel,
        grid=(B, nc),
        in_specs=[rows(W_Z, OFF_Z // W_Z), rows(W_XS, OFF_XS // W_XS), rows(W_BC, OFF_BC // W_BC),
                  rows(W_SM, OFF_SM // W_SM)] + [full(a) for a in consts],
        out_specs=pl.BlockSpec((CHUNK, SSD_INNER), lambda b, c: (b * nc + c, 0)),
        out_shape=jax.ShapeDtypeStruct((T, SSD_INNER), BF16),
        scratch_shapes=[pltpu.VMEM((HALO, W_XS + W_BC), BF16),
                        pltpu.VMEM((SSD_STATE, SSD_INNER), F32)],
        compiler_params=_params("parallel", "arbitrary"),
        name="ssd_mixer",
    )(proj, proj, proj, proj, *consts)


def _l2norm(x):
    return x * lax.rsqrt(jnp.sum(x * x, axis=-1, keepdims=True) + EPS)


def _split(x):
    hi = x.astype(BF16)
    return hi, (x - hi.astype(F32)).astype(BF16)


def _dot3(a, b):
    (ah, al), (bh, bl) = a, b
    return _dot(jnp.concatenate([ah, al, ah], axis=1), jnp.concatenate([bh, bh, bl], axis=0))


def _dot2(a, b):
    return _dot(jnp.concatenate(a, axis=1), jnp.concatenate([b, b], axis=0))


def _dot2r(a, b):
    return _dot(jnp.concatenate([a, a], axis=1), jnp.concatenate(b, axis=0))


SOLVE_BLOCK_LOG2 = 3


def _block_masks(row, col):
    n_levels = int(math.log2(CHUNK)) - SOLVE_BLOCK_LOG2
    diag = (row >> SOLVE_BLOCK_LOG2) == (col >> SOLVE_BLOCK_LOG2)
    merges = []
    for lv in range(n_levels):
        s = SOLVE_BLOCK_LOG2 + lv
        same = (row >> (s + 1)) == (col >> (s + 1))
        lower_left = jnp.where(same, ((row >> s) & 1) - ((col >> s) & 1), 0) == 1
        merges.append(lower_left)
    return diag, merges


def _unit_lower_inverses(mats, eye, diag_mask, merge_masks):
    def as_factor(mask):
        return jnp.where(mask, 1.0, 0.0).astype(BF16)

    eye_b = eye.astype(BF16)
    diag_b = as_factor(diag_mask)
    d1 = [a * diag_b for a in mats]
    d2 = [_split(_dot(d, d)) for d in d1]
    t0 = [eye_b - d for d in d1]
    ts = [t.astype(F32) + _dot2r(t, x) for t, x in zip(t0, d2)]
    d4 = [_split(_dot3(x, x)) for x in d2]
    ts = [t + _dot3(_split(t), x) for t, x in zip(ts, d4)]
    for m in merge_masks:
        tsp = [_split(t) for t in ts]
        mb = as_factor(m)
        mid = [_split(_dot2(t, a * mb)) for t, a in zip(tsp, mats)]
        ts = [t - _dot3(x, tp) for t, x, tp in zip(ts, mid, tsp)]
    return ts


def _gdn_kernel(qkv_ref, gz_ref, sm_ref, cw_ref, shift_ref, dtb_ref, alog_ref, ng_ref, o_ref, tail_ref, s_ref):
    HD = GDN_HEAD
    NH = GDN_V_HEADS
    heads = range(NH)

    @pl.when(pl.program_id(1) == 0)
    def _():
        tail_ref[...] = jnp.zeros(tail_ref.shape, tail_ref.dtype)
        s_ref[...] = jnp.zeros(s_ref.shape, F32)

    qkv = _silu(_causal_conv(tail_ref, qkv_ref[...], cw_ref, shift_ref))

    row = lax.broadcasted_iota(jnp.int32, (CHUNK, CHUNK), 0)
    col = lax.broadcasted_iota(jnp.int32, (CHUNK, CHUNK), 1)
    incl = row >= col
    strict = row > col
    eye = (row == col).astype(F32)
    diag_mask, merge_masks = _block_masks(row, col)

    sm_t = sm_ref[...].T
    beta_t = _sigmoid(sm_t[SM_B:SM_B + NH, :])
    g_t = -jnp.exp(alog_ref[...]) * _softplus(sm_t[SM_A:SM_A + NH, :] + dtb_ref[...])
    gc_t = jnp.dot(g_t, (row <= col).astype(F32), precision=HIGHEST, preferred_element_type=F32)
    pad = jnp.zeros((CHUNK - 2 * NH, CHUNK), F32)
    cols = jnp.concatenate([beta_t, gc_t, pad], axis=0).T

    q = [_l2norm(qkv[:, j * HD:(j + 1) * HD]) * (HD ** -0.5) for j in range(GDN_QK_HEADS)]
    k = [_l2norm(qkv[:, GDN_QK + j * HD:GDN_QK + (j + 1) * HD]) for j in range(GDN_QK_HEADS)]
    k_t = [x.T.astype(BF16) for x in k]
    qk_raw = [_dot(a.astype(BF16), b) for a, b in zip(q, k_t)]

    b_col = [cols[:, h:h + 1] for h in heads]
    g_col = [cols[:, NH + h:NH + h + 1] for h in heads]
    g_last = [gc_t[h:h + 1, CHUNK - 1:CHUNK] for h in heads]
    decay = [jnp.exp(jnp.where(incl, g_col[h] - gc_t[h:h + 1, :], NEG_BIG)) for h in heads]
    kb = [k[h // 2] * b_col[h] for h in heads]
    a_mat = [jnp.where(strict, _dot(kb[h].astype(BF16), k_t[h // 2]) * decay[h], 0.0).astype(BF16)
             for h in heads]
    t_inv = _unit_lower_inverses(a_mat, eye, diag_mask, merge_masks)

    e_g = [jnp.exp(g_col[h]) for h in heads]
    v = [qkv[:, 2 * GDN_QK + h * HD:2 * GDN_QK + (h + 1) * HD] for h in heads]
    rhs = [jnp.concatenate([v[h] * b_col[h], kb[h] * e_g[h]], axis=1).astype(BF16) for h in heads]
    sol = [_dot2(_split(t_inv[h]), rhs[h]) for h in heads]
    state = [s_ref[h] for h in heads]
    ws = [_dot(jnp.concatenate([sol[h][:, HD:2 * HD], q[h // 2] * e_g[h]], axis=0).astype(BF16),
               state[h].astype(BF16)) for h in heads]
    v_new_b = [(sol[h][:, 0:HD] - ws[h][0:CHUNK]).astype(BF16) for h in heads]
    o = [ws[h][CHUNK:2 * CHUNK] + _dot((qk_raw[h // 2] * decay[h]).astype(BF16), v_new_b[h]) for h in heads]
    k_dec_t = [(k[h // 2] * jnp.exp(g_last[h] - g_col[h])).T.astype(BF16) for h in heads]
    for h in heads:
        s_ref[h] = state[h] * jnp.exp(g_last[h]) + _dot(k_dec_t[h], v_new_b[h])
    for h in heads:
        out = _rms(o[h], ng_ref[...]) * _silu(gz_ref[:, h * HD:(h + 1) * HD])
        o_ref[:, h * HD:(h + 1) * HD] = out.astype(o_ref.dtype)


def _gdn(proj, B, S, cw, shift, dtb, alog, ng):
    T = B * S
    nc = S // CHUNK
    full = lambda a: pl.BlockSpec(a.shape, lambda b, c: (0,) * a.ndim)
    rows = lambda w, idx: pl.BlockSpec((CHUNK, w), lambda b, c: (b * nc + c, idx))
    consts = [cw, shift, dtb, alog, ng]
    return pl.pallas_call(
        _gdn_kernel,
        grid=(B, nc),
        in_specs=[rows(W_GQKV, OFF_GQKV // W_GQKV), rows(W_GZ, OFF_GZ // W_GZ),
                  rows(W_SM, OFF_SM // W_SM)] + [full(a) for a in consts],
        out_specs=pl.BlockSpec((CHUNK, GDN_V_HEADS * GDN_HEAD), lambda b, c: (b * nc + c, 0)),
        out_shape=jax.ShapeDtypeStruct((T, GDN_V_HEADS * GDN_HEAD), BF16),
        scratch_shapes=[pltpu.VMEM((HALO, GDN_QKV), BF16),
                        pltpu.VMEM((GDN_V_HEADS, GDN_HEAD, GDN_HEAD), F32)],
        compiler_params=_params("parallel", "arbitrary"),
        name="gdn_mixer",
    )(proj, proj, proj, *consts)


ATTN_BLOCK = 512
LOG2E = math.log2(math.e)
BF16_SUBLANES = 16
V_ROWS = MLA_V + BF16_SUBLANES
HEADS_PER_STEP = 4


def _mla_prep_kernel(cq_ref, ckv_ref, kr_ref, cos_ref, sin_ref, qg_ref, kvg_ref, wq_ref, wkv_ref,
                     qt_ref, k_ref, vt_ref):
    H = MLA_HEADS
    tm = cq_ref.shape[0]
    scale = (MLA_NOPE + MLA_ROPE) ** -0.5 * LOG2E
    cos4 = cos_ref[...]
    sin4 = sin_ref[...]
    lane = lax.broadcasted_iota(jnp.int32, (tm, LANES), 1)
    left = lane < MLA_ROPE

    qm = _dot(_rms(cq_ref[...], qg_ref[...]).astype(BF16), wq_ref[...])
    kvm = _dot(_rms(ckv_ref[...], kvg_ref[...]).astype(BF16), wkv_ref[...])

    kr = kr_ref[...]
    k_rope = (kr * jnp.where(left, cos4, sin4)
              + pltpu.roll(kr, MLA_ROPE, axis=1) * jnp.where(left, sin4, cos4)).astype(BF16)

    pe_off = H * MLA_NOPE
    rot_off = pe_off + H * MLA_ROPE
    for h in range(H):
        jb = (h // 2) * LANES
        pe = qm[:, pe_off + jb:pe_off + jb + LANES]
        rot = qm[:, rot_off + jb:rot_off + jb + LANES]
        roped = pe * cos4 + rot * sin4
        mine = left if h % 2 == 0 else jnp.logical_not(left)
        q_nope = qm[:, h * MLA_NOPE:(h + 1) * MLA_NOPE]
        qt_ref[h, 0, 0:LANES, :] = (q_nope * scale).T.astype(BF16)
        qt_ref[h, 0, LANES:2 * LANES, :] = (jnp.where(mine, roped, 0.0) * scale).T.astype(BF16)
        base = h * (MLA_NOPE + MLA_V)
        k_ref[h] = jnp.concatenate([kvm[:, base:base + MLA_NOPE].astype(BF16), k_rope], axis=1)
        vt_ref[h, 0, 0:MLA_V, :] = kvm[:, base + MLA_NOPE:base + MLA_NOPE + MLA_V].T.astype(BF16)
        ones_row = lax.broadcasted_iota(jnp.int32, (BF16_SUBLANES, tm), 0) == 0
        vt_ref[h, 0, MLA_V:V_ROWS, :] = jnp.where(ones_row, 1.0, 0.0).astype(BF16)


def _mla_prep(proj, cos4, sin4, qg, kvg, wq, wkv):
    T = proj.shape[0]
    H = MLA_HEADS
    tm = ATTN_BLOCK
    nt = T // tm
    full = lambda a: pl.BlockSpec(a.shape, lambda i: (0,) * a.ndim)
    rows = lambda w, idx: pl.BlockSpec((tm, w), lambda i: (i, idx))
    consts = [qg, kvg, wq, wkv]
    dq = 2 * LANES
    return pl.pallas_call(
        _mla_prep_kernel,
        grid=(nt,),
        in_specs=[rows(W_CQ, OFF_CQ // W_CQ), rows(W_CKV, OFF_CKV // W_CKV), rows(W_KR, OFF_KR // W_KR),
                  rows(LANES, 0), rows(LANES, 0)] + [full(a) for a in consts],
        out_specs=[pl.BlockSpec((H, 1, dq, tm), lambda i: (0, i, 0, 0)),
                   pl.BlockSpec((H, tm, dq), lambda i: (0, i, 0)),
                   pl.BlockSpec((H, 1, V_ROWS, tm), lambda i: (0, i, 0, 0))],
        out_shape=[jax.ShapeDtypeStruct((H, nt, dq, tm), BF16),
                   jax.ShapeDtypeStruct((H, T, dq), BF16),
                   jax.ShapeDtypeStruct((H, nt, V_ROWS, tm), BF16)],
        compiler_params=_params("parallel"),
        name="mla_prep",
    )(proj, proj, proj, cos4, sin4, *consts)


def _attn_kernel(qt_ref, k_ref, vt_ref, o_ref, m_ref, acc_ref):
    tb = ATTN_BLOCK
    hs = range(HEADS_PER_STEP)
    qi = pl.program_id(2)
    m_ref[...] = jnp.full(m_ref.shape, NEG_BIG, F32)
    acc_ref[...] = jnp.zeros(acc_ref.shape, F32)

    def block(ki, masked):
        start = pl.multiple_of(ki * tb, tb)
        s = [_dot(k_ref[h, pl.ds(start, tb), :], qt_ref[h, 0]) for h in hs]
        if masked:
            key = lax.broadcasted_iota(jnp.int32, (tb, tb), 0)
            qry = lax.broadcasted_iota(jnp.int32, (tb, tb), 1)
            s = [jnp.where(key <= qry, x, NEG_BIG) for x in s]
        m_old = [m_ref[h] for h in hs]
        m_new = [jnp.maximum(m_old[h], jnp.max(s[h], axis=0, keepdims=True)) for h in hs]
        p = [jnp.exp2(s[h] - m_new[h]).astype(BF16) for h in hs]
        pv = [_dot(vt_ref[h, ki], p[h]) for h in hs]
        for h in hs:
            acc_ref[h] = jnp.exp2(m_old[h] - m_new[h]) * acc_ref[h] + pv[h]
            m_ref[h] = m_new[h]

    def body(ki, carry):
        block(ki, False)
        return carry

    lax.fori_loop(0, qi, body, 0)
    block(qi, True)
    for h in hs:
        acc = acc_ref[h]
        o_ref[:, h * MLA_V:(h + 1) * MLA_V] = (acc[0:MLA_V] / acc[MLA_V:MLA_V + 1]).T.astype(o_ref.dtype)


def _attention(q_t, k, v_t, B, S):
    H, T, dq = k.shape
    tb = ATTN_BLOCK
    nq = S // tb
    hp = HEADS_PER_STEP
    return pl.pallas_call(
        _attn_kernel,
        grid=(B, H // hp, nq),
        in_specs=[pl.BlockSpec((hp, 1, dq, tb), lambda b, h, i: (h, b * nq + i, 0, 0)),
                  pl.BlockSpec((hp, S, dq), lambda b, h, i: (h, b, 0)),
                  pl.BlockSpec((hp, nq, V_ROWS, tb), lambda b, h, i: (h, b, 0, 0))],
        out_specs=pl.BlockSpec((tb, hp * MLA_V), lambda b, h, i: (b * nq + i, h)),
        out_shape=jax.ShapeDtypeStruct((T, H * MLA_V), BF16),
        scratch_shapes=[pltpu.VMEM((hp, 1, tb), F32), pltpu.VMEM((hp, V_ROWS, tb), F32)],
        compiler_params=_params("parallel", "parallel", "arbitrary"),
        name="mla_attention",
    )(q_t, k, v_t)


def _merge_kernel(x_ref, ys_ref, ym_ref, yg_ref, g_ref, wgate_ref, ws_ref, wm_ref, wg_ref, wo_ref, h_ref):
    D = D_MODEL
    x = x_ref[...]
    xn = _rms(x, g_ref[...]).astype(BF16)
    mixed = None
    for idx, (y_ref, w_ref) in enumerate(((ys_ref, ws_ref), (ym_ref, wm_ref), (yg_ref, wg_ref))):
        gate = _sigmoid(_dot(xn, wgate_ref[:, idx * D:(idx + 1) * D]))
        term = gate * _dot(y_ref[...], w_ref[...])
        mixed = term if mixed is None else mixed + term
    h_ref[...] = x + _dot(mixed.astype(BF16), wo_ref[...])


def _merge(x2, ys, ym, yg, g, wgate, ws, wm, wg, wo):
    T, D = x2.shape
    tm = min(T, 256)
    full = lambda a: pl.BlockSpec(a.shape, lambda i: (0,) * a.ndim)
    rows = pl.BlockSpec((tm, D), lambda i: (i, 0))
    consts = [g, wgate, ws, wm, wg, wo]
    return pl.pallas_call(
        _merge_kernel,
        grid=(T // tm,),
        in_specs=[rows] * 4 + [full(a) for a in consts],
        out_specs=rows,
        out_shape=jax.ShapeDtypeStruct((T, D), F32),
        compiler_params=_params("parallel"),
        name="merge",
    )(x2, ys, ym, yg, *consts)


def _ffn_kernel(h_ref, g_ref, wu_ref, wd_ref, fg_ref, o_ref, hn_ref, acc_ref, *, final_norm):
    j = pl.program_id(1)

    @pl.when(j == 0)
    def _():
        hn_ref[...] = _rms(h_ref[...], g_ref[...]).astype(BF16)
        acc_ref[...] = jnp.zeros(acc_ref.shape, F32)

    up = jnp.maximum(_dot(hn_ref[...], wu_ref[...]), 0.0)
    acc_ref[...] += _dot((up * up).astype(BF16), wd_ref[...])

    @pl.when(j == pl.num_programs(1) - 1)
    def _():
        out = h_ref[...] + acc_ref[...]
        if final_norm:
            out = _rms(out, fg_ref[...])
        o_ref[...] = out


def _ffn(h, g, wu, wd, fg, final_norm):
    T, D = h.shape
    F = wu.shape[1]
    tm = min(T, 1024)
    tf = min(F, 1024)
    return pl.pallas_call(
        functools.partial(_ffn_kernel, final_norm=final_norm),
        grid=(T // tm, F // tf),
        in_specs=[pl.BlockSpec((tm, D), lambda i, j: (i, 0)),
                  pl.BlockSpec((1, D), lambda i, j: (0, 0)),
                  pl.BlockSpec((D, tf), lambda i, j: (0, j)),
                  pl.BlockSpec((tf, D), lambda i, j: (j, 0)),
                  pl.BlockSpec((1, D), lambda i, j: (0, 0))],
        out_specs=pl.BlockSpec((tm, D), lambda i, j: (i, 0)),
        out_shape=jax.ShapeDtypeStruct((T, D), F32),
        scratch_shapes=[pltpu.VMEM((tm, D), BF16), pltpu.VMEM((tm, D), F32)],
        compiler_params=_params("parallel", "arbitrary"),
        name="ffn",
    )(h, g, wu, wd, fg)


def _rotate_half_cols(w):
    half = w.shape[-1] // 2
    return jnp.concatenate([-w[..., half:], w[..., :half]], axis=-1)


def _split_in_proj(w_in):
    sizes = (SSD_INNER, SSD_INNER, 2 * SSD_GROUPS * SSD_STATE, SSD_HEADS, MLA_Q_LORA, MLA_KV_LORA, MLA_ROPE,
             GDN_QKV, GDN_V_HEADS * GDN_HEAD, GDN_V_HEADS, GDN_V_HEADS, 3 * D_MODEL)
    offs = np.cumsum((0,) + sizes)
    return [w_in[:, int(offs[i]):int(offs[i + 1])] for i in range(len(sizes))]


def _layer_weights(w_in, w_uq):
    z, xs, bc, dt, cq, ckv, kr, gqkv, gz, gb, ga, gates = _split_in_proj(w_in)
    small = jnp.concatenate([dt, gb, ga, jnp.zeros((D_MODEL, W_SM - 32), F32)], axis=1)
    w_a = jnp.concatenate([gqkv, z, xs, gz, bc, cq, ckv, kr, _rotate_half_cols(kr), small], axis=1)
    H = MLA_HEADS
    wq = w_uq.reshape(MLA_Q_LORA, H, MLA_NOPE + MLA_ROPE)
    nope = wq[:, :, :MLA_NOPE].reshape(MLA_Q_LORA, H * MLA_NOPE)
    pe = wq[:, :, MLA_NOPE:]
    w_q = jnp.concatenate([nope, pe.reshape(MLA_Q_LORA, H * MLA_ROPE),
                           _rotate_half_cols(pe).reshape(MLA_Q_LORA, H * MLA_ROPE)], axis=1)
    return w_a.astype(BF16), gates.astype(BF16), w_q.astype(BF16)


def _pad_lanes(v, n=LANES):
    return jnp.concatenate([v, jnp.zeros((n - v.shape[0],), v.dtype)])[None, :]


def kernel(x, positions, norm1_g, w_in, ssd_conv_w, ssd_conv_b, ssd_dt_bias, ssd_a_log, ssd_d, ssd_norm_g,
           mla_q_norm_g, mla_w_uq, mla_kv_norm_g, mla_w_ukv, gdn_conv_w, gdn_dt_bias, gdn_a_log, gdn_norm_g,
           w_ssd_out, w_mla_out, w_gdn_out, w_out, norm2_g, w_up, w_down, final_norm_g):
    B, S, D = x.shape
    T = B * S
    x2 = x.reshape(T, D)

    inv = ROPE_THETA ** (-jnp.arange(0, MLA_ROPE, 2, dtype=F32) / MLA_ROPE)
    inv4 = jnp.tile(inv, LANES // inv.shape[0])[None, :]
    cos4, sin4 = _rope_tables(positions.reshape(T, 1), inv4)

    shift = _shift_matrix()
    expand = (jnp.arange(LANES)[:, None] == (jnp.arange(SSD_INNER)[None, :] // SSD_HEAD_DIM)).astype(BF16)

    for l in range(DEPTH):
        w_a, w_gate, w_q = _layer_weights(w_in[l], mla_w_uq[l])
        proj = _inproj(x2, norm1_g[l][None, :], w_a)

        y_ssd = _ssd(proj, B, S, ssd_conv_w[l], ssd_conv_b[l][None, :], shift,
                     _pad_lanes(ssd_dt_bias[l]), _pad_lanes(ssd_a_log[l]),
                     jnp.repeat(ssd_d[l], SSD_HEAD_DIM)[None, :], ssd_norm_g[l][None, :], expand)

        q, k, v = _mla_prep(proj, cos4, sin4, mla_q_norm_g[l][None, :], mla_kv_norm_g[l][None, :],
                            w_q, mla_w_ukv[l].astype(BF16))
        y_mla = _attention(q, k, v, B, S)

        y_gdn = _gdn(proj, B, S, gdn_conv_w[l], shift,
                     jnp.broadcast_to(gdn_dt_bias[l][:, None], (GDN_V_HEADS, CHUNK)),
                     jnp.broadcast_to(gdn_a_log[l][:, None], (GDN_V_HEADS, CHUNK)),
                     gdn_norm_g[l][None, :])

        h = _merge(x2, y_ssd, y_mla, y_gdn, norm1_g[l][None, :], w_gate,
                   w_ssd_out[l].astype(BF16), w_mla_out[l].astype(BF16), w_gdn_out[l].astype(BF16),
                   w_out[l].astype(BF16))
        x2 = _ffn(h, norm2_g[l][None, :], w_up[l].astype(BF16), w_down[l].astype(BF16),
                  final_norm_g[None, :], final_norm=(l == DEPTH - 1))
    return x2.reshape(B, S, D)
```

```python
import functools
import math

import jax
import jax.numpy as jnp
import numpy as np
from jax import lax
from jax.experimental import pallas as pl
from jax.experimental.pallas import tpu as pltpu

F32 = jnp.float32
BF16 = jnp.bfloat16
HIGHEST = lax.Precision.HIGHEST

EPS = 1e-6
D_MODEL = 1024
DEPTH = 2
SSD_HEADS = 16
SSD_HEAD_DIM = 64
SSD_GROUPS = 2
SSD_STATE = 128
SSD_INNER = 1024
MLA_HEADS = 8
MLA_NOPE = 128
MLA_ROPE = 64
MLA_V = 128
MLA_Q_LORA = 512
MLA_KV_LORA = 256
ROPE_THETA = 10000.0
GDN_HEAD = 128
GDN_V_HEADS = 8
GDN_QK_HEADS = 4
GDN_QK = GDN_QK_HEADS * GDN_HEAD
GDN_QKV = 2 * GDN_QK + GDN_V_HEADS * GDN_HEAD
D_FF = 4096
CONV_K = 4

CHUNK = 128
HALO = 16
LANES = 128
VMEM_LIMIT = 56 * 1024 * 1024

W_GQKV, W_Z, W_XS, W_GZ, W_BC, W_CQ, W_CKV, W_KR, W_SM = 2048, 1024, 1024, 1024, 512, 512, 256, 128, 128
OFF_GQKV, OFF_Z, OFF_XS, OFF_GZ, OFF_BC, OFF_CQ, OFF_CKV, OFF_KR, OFF_SM = (
    0, 2048, 3072, 4096, 5120, 5632, 6144, 6400, 6528)
N_PROJ = 6656
SM_DT, SM_B, SM_A = 0, 16, 24

NEG_BIG = -1e30


def _rms(x, g):
    return x * lax.rsqrt(jnp.mean(x * x, axis=-1, keepdims=True) + EPS) * g


def _sigmoid(x):
    return 1.0 / (1.0 + jnp.exp(-x))


def _silu(x):
    h = 0.5 * x
    return h + h * jnp.tanh(h)


def _softplus(x):
    return jnp.maximum(x, 0.0) + jnp.log1p(jnp.exp(-jnp.abs(x)))


def _dot(a, b):
    return jnp.dot(a, b, preferred_element_type=F32)


def _dot_nt(a, b):
    return lax.dot_general(a, b, (((1,), (1,)), ((), ())), preferred_element_type=F32)


def _params(*sem):
    return pltpu.CompilerParams(dimension_semantics=sem, vmem_limit_bytes=VMEM_LIMIT)


def _rope_kernel(pos_ref, inv_ref, cos_ref, sin_ref):
    ang = pos_ref[...].astype(F32) * inv_ref[...]
    cos_ref[...] = jnp.cos(ang)
    sin_ref[...] = jnp.sin(ang)


def _rope_tables(pos_col, inv4):
    T = pos_col.shape[0]
    tm = min(T, 2048)
    return pl.pallas_call(
        _rope_kernel,
        grid=(T // tm,),
        in_specs=[pl.BlockSpec((tm, 1), lambda i: (i, 0)),
                  pl.BlockSpec((1, LANES), lambda i: (0, 0))],
        out_specs=[pl.BlockSpec((tm, LANES), lambda i: (i, 0))] * 2,
        out_shape=[jax.ShapeDtypeStruct((T, LANES), F32)] * 2,
        compiler_params=_params("parallel"),
        name="rope_tables",
    )(pos_col, inv4)


def _inproj_kernel(x_ref, g_ref, w_ref, o_ref, xn_ref):
    @pl.when(pl.program_id(1) == 0)
    def _():
        xn_ref[...] = _rms(x_ref[...], g_ref[...]).astype(BF16)

    o_ref[...] = _dot(xn_ref[...], w_ref[...])


def _inproj(x2, g, w):
    T, D = x2.shape
    N = w.shape[1]
    tm = min(T, 1024)
    tn = N // 4
    return pl.pallas_call(
        _inproj_kernel,
        grid=(T // tm, N // tn),
        in_specs=[pl.BlockSpec((tm, D), lambda i, j: (i, 0)),
                  pl.BlockSpec((1, D), lambda i, j: (0, 0)),
                  pl.BlockSpec((D, tn), lambda i, j: (0, j))],
        out_specs=pl.BlockSpec((tm, tn), lambda i, j: (i, j)),
        out_shape=jax.ShapeDtypeStruct((T, N), F32),
        scratch_shapes=[pltpu.VMEM((tm, D), BF16)],
        compiler_params=_params("parallel", "arbitrary"),
        name="in_proj",
    )(x2, g, w)


def _shift_matrix():
    r = np.arange((CONV_K - 1) * CHUNK)[:, None]
    c = np.arange(HALO + CHUNK)[None, :]
    return jnp.asarray(c == HALO + (r % CHUNK) - (r // CHUNK + 1), BF16)


def _causal_conv(tail_ref, cur, w_ref, shift_ref):
    cur_b = cur.astype(BF16)
    delayed = _dot(shift_ref[...], jnp.concatenate([tail_ref[...], cur_b], axis=0))
    tail_ref[...] = cur_b[CHUNK - HALO:CHUNK, :]
    acc = w_ref[CONV_K - 1:CONV_K, :] * cur
    for j in range(1, CONV_K):
        acc = acc + w_ref[CONV_K - 1 - j:CONV_K - j, :] * delayed[(j - 1) * CHUNK:j * CHUNK]
    return acc


def _ssd_kernel(z_ref, xs_ref, bc_ref, sm_ref, cw_ref, cb_ref, shift_ref, dtb_ref,
                alog_ref, dsk_ref, ng_ref, e_ref, o_ref, tail_ref, h_ref):
    G, N, P = SSD_GROUPS, SSD_STATE, SSD_HEAD_DIM
    HG = SSD_HEADS // G
    GW = HG * P

    @pl.when(pl.program_id(1) == 0)
    def _():
        tail_ref[...] = jnp.zeros(tail_ref.shape, tail_ref.dtype)
        h_ref[...] = jnp.zeros(h_ref.shape, F32)

    raw = jnp.concatenate([xs_ref[...], bc_ref[...]], axis=1)
    xbc = _silu(_causal_conv(tail_ref, raw, cw_ref, shift_ref) + cb_ref[...])
    xs = xbc[:, 0:SSD_INNER]
    bc = xbc[:, SSD_INNER:]

    row = lax.broadcasted_iota(jnp.int32, (CHUNK, CHUNK), 0)
    col = lax.broadcasted_iota(jnp.int32, (CHUNK, CHUNK), 1)
    tril = row >= col

    dt = _softplus(sm_ref[...] + dtb_ref[...])
    dA = dt * (-jnp.exp(alog_ref[...]))
    a_cum = jnp.dot(tril.astype(F32), dA, precision=HIGHEST, preferred_element_type=F32)
    a_cum_t = a_cum.T
    dt_t = dt.T
    e_a = jnp.exp(a_cum)
    sdec = dt * jnp.exp(a_cum[CHUNK - 1:CHUNK, :] - a_cum)
    stack = jnp.concatenate([e_a, sdec], axis=0)
    hi = stack.astype(BF16)
    lo = (stack - hi.astype(F32)).astype(BF16)
    ex = _dot(hi, e_ref[...]) + _dot(lo, e_ref[...])
    e_a_x = ex[0:CHUNK]
    sdec_x = ex[CHUNK:2 * CHUNK]

    lane = lax.broadcasted_iota(jnp.int32, (CHUNK, LANES), 1)
    left = lane < P

    groups = range(G)
    heads = range(SSD_HEADS)
    pairs = range(SSD_HEADS // 2)
    gsl = [slice(g * GW, (g + 1) * GW) for g in groups]
    b_g = [bc[:, g * N:(g + 1) * N] for g in groups]
    c_b = [bc[:, (G + g) * N:(G + g + 1) * N].astype(BF16) for g in groups]
    cb = [_dot_nt(c_b[g], b_g[g].astype(BF16)) for g in groups]
    x_sd = [(xs[:, gsl[g]] * sdec_x[:, gsl[g]]).astype(BF16) for g in groups]
    st = [_dot(b_g[g].T.astype(BF16), x_sd[g]) for g in groups]
    h_prev = [h_ref[:, gsl[g]] for g in groups]
    y_off = [_dot(c_b[g], h_prev[g].astype(BF16)) * e_a_x[:, gsl[g]] for g in groups]
    for g in groups:
        h_ref[:, gsl[g]] = h_prev[g] * e_a_x[CHUNK - 1:CHUNK, gsl[g]] + st[g]
    seg = [jnp.where(tril, a_cum[:, e:e + 1] - a_cum_t[e:e + 1, :], NEG_BIG) for e in heads]
    m = [cb[e // HG] * jnp.exp(seg[e]) * dt_t[e:e + 1, :] for e in heads]
    m2 = [jnp.concatenate([m[2 * p], m[2 * p + 1]], axis=1).astype(BF16) for p in pairs]
    xp = [xs[:, 2 * p * P:(2 * p + 2) * P] for p in pairs]
    bd = [jnp.concatenate([jnp.where(left, xp[p], 0.0), jnp.where(left, 0.0, xp[p])], axis=0).astype(BF16)
          for p in pairs]
    diag = [_dot(m2[p], bd[p]) for p in pairs]
    y = jnp.concatenate(diag, axis=1) + jnp.concatenate(y_off, axis=1) + dsk_ref[...] * xs
    y = y * _silu(z_ref[...])
    outs = []
    for g in range(G):
        gs = slice(g * GW, (g + 1) * GW)
        outs.append(_rms(y[:, gs], ng_ref[:, gs]))
    o_ref[...] = jnp.concatenate(outs, axis=1).astype(o_ref.dtype)


def _ssd(proj, B, S, cw, cb, shift, dtb, alog, dsk, ng, expand):
    T = B * S
    nc = S // CHUNK
    full = lambda a: pl.BlockSpec(a.shape, lambda b, c: (0,) * a.ndim)
    rows = lambda w, idx: pl.BlockSpec((CHUNK, w), lambda b, c: (b * nc + c, idx))
    consts = [cw, cb, shift, dtb, alog, dsk, ng, expand]
    return pl.pallas_call(
        _ssd_kernel,
        grid=(B, nc),
        in_specs=[rows(W_Z, OFF_Z // W_Z), rows(W_XS, OFF_XS // W_XS), rows(W_BC, OFF_BC // W_BC),
                  rows(W_SM, OFF_SM // W_SM)] + [full(a) for a in consts],
        out_specs=pl.BlockSpec((CHUNK, SSD_INNER), lambda b, c: (b * nc + c, 0)),
        out_shape=jax.ShapeDtypeStruct((T, SSD_INNER), BF16),
        scratch_shapes=[pltpu.VMEM((HALO, W_XS + W_BC), BF16),
                        pltpu.VMEM((SSD_STATE, SSD_INNER), F32)],
        compiler_params=_params("parallel", "arbitrary"),
        name="ssd_mixer",
    )(proj, proj, proj, proj, *consts)


def _l2norm(x):
    return x * lax.rsqrt(jnp.sum(x * x, axis=-1, keepdims=True) + EPS)


def _split(x):
    hi = x.astype(BF16)
    return hi, (x - hi.astype(F32)).astype(BF16)


def _dot3(a, b):
    (ah, al), (bh, bl) = a, b
    return _dot(jnp.concatenate([ah, al, ah], axis=1), jnp.concatenate([bh, bh, bl], axis=0))


def _dot2(a, b):
    return _dot(jnp.concatenate(a, axis=1), jnp.concatenate([b, b], axis=0))


def _dot2r(a, b):
    return _dot(jnp.concatenate([a, a], axis=1), jnp.concatenate(b, axis=0))


SOLVE_BLOCK_LOG2 = 3


def _block_masks(row, col):
    n_levels = int(math.log2(CHUNK)) - SOLVE_BLOCK_LOG2
    diag = (row >> SOLVE_BLOCK_LOG2) == (col >> SOLVE_BLOCK_LOG2)
    merges = []
    for lv in range(n_levels):
        s = SOLVE_BLOCK_LOG2 + lv
        same = (row >> (s + 1)) == (col >> (s + 1))
        lower_left = jnp.where(same, ((row >> s) & 1) - ((col >> s) & 1), 0) == 1
        merges.append(lower_left)
    return diag, merges


def _unit_lower_inverses(mats, eye, diag_mask, merge_masks):
    def as_factor(mask):
        return jnp.where(mask, 1.0, 0.0).astype(BF16)

    eye_b = eye.astype(BF16)
    diag_b = as_factor(diag_mask)
    d1 = [a * diag_b for a in mats]
    d2 = [_split(_dot(d, d)) for d in d1]
    t0 = [eye_b - d for d in d1]
    ts = [t.astype(F32) + _dot2r(t, x) for t, x in zip(t0, d2)]
    d4 = [_split(_dot3(x, x)) for x in d2]
    ts = [t + _dot3(_split(t), x) for t, x in zip(ts, d4)]
    for lv, m in enumerate(merge_masks):
        n = 1 << (SOLVE_BLOCK_LOG2 + lv)
        pairs = range(CHUNK // (2 * n))

        def take(x):
            return jnp.concatenate([x[(2 * b + 1) * n:(2 * b + 2) * n] for b in pairs], axis=0)

        def put(x, part):
            pieces = []
            for b in pairs:
                pieces += [x[2 * b * n:(2 * b + 1) * n], part[b * n:(b + 1) * n]]
            return jnp.concatenate(pieces, axis=0)

        tsp = [_split(t) for t in ts]
        mb = as_factor(m)
        low = [take(t) for t in ts]
        mid = [_split(_dot2(_split(x), a * mb)) for x, a in zip(low, mats)]
        ts = [put(t, x - _dot3(md, tp)) for t, x, md, tp in zip(ts, low, mid, tsp)]
    return ts


def _gdn_kernel(qkv_ref, gz_ref, sm_ref, cw_ref, shift_ref, dtb_ref, alog_ref, ng_ref, o_ref, tail_ref, s_ref):
    HD = GDN_HEAD
    NH = GDN_V_HEADS
    NB = qkv_ref.shape[0]
    rows_ = range(NB)
    items = [(r, h) for r in rows_ for h in range(NH)]

    @pl.when(pl.program_id(1) == 0)
    def _():
        tail_ref[...] = jnp.zeros(tail_ref.shape, tail_ref.dtype)
        s_ref[...] = jnp.zeros(s_ref.shape, F32)

    qkv = [_silu(_causal_conv(tail_ref.at[r], qkv_ref[r], cw_ref, shift_ref)) for r in rows_]

    row = lax.broadcasted_iota(jnp.int32, (CHUNK, CHUNK), 0)
    col = lax.broadcasted_iota(jnp.int32, (CHUNK, CHUNK), 1)
    incl = row >= col
    strict = row > col
    eye = (row == col).astype(F32)
    diag_mask, merge_masks = _block_masks(row, col)
    upper = (row <= col).astype(F32)
    pad = jnp.zeros((CHUNK - 2 * NH, CHUNK), F32)

    sm_t = [sm_ref[r].T for r in rows_]
    beta_t = [_sigmoid(x[SM_B:SM_B + NH, :]) for x in sm_t]
    g_t = [-jnp.exp(alog_ref[...]) * _softplus(x[SM_A:SM_A + NH, :] + dtb_ref[...]) for x in sm_t]
    gc_t = [jnp.dot(x, upper, precision=HIGHEST, preferred_element_type=F32) for x in g_t]
    cols = [jnp.concatenate([beta_t[r], gc_t[r], pad], axis=0).T for r in rows_]

    q = [[_l2norm(qkv[r][:, j * HD:(j + 1) * HD]) * (HD ** -0.5) for j in range(GDN_QK_HEADS)] for r in rows_]
    k = [[_l2norm(qkv[r][:, GDN_QK + j * HD:GDN_QK + (j + 1) * HD]) for j in range(GDN_QK_HEADS)] for r in rows_]
    k_t = [[x.T.astype(BF16) for x in k[r]] for r in rows_]
    qk_raw = [[_dot(a.astype(BF16), b) for a, b in zip(q[r], k_t[r])] for r in rows_]

    b_col = [cols[r][:, h:h + 1] for r, h in items]
    g_col = [cols[r][:, NH + h:NH + h + 1] for r, h in items]
    g_last = [gc_t[r][h:h + 1, CHUNK - 1:CHUNK] for r, h in items]
    decay = [jnp.exp(jnp.where(incl, g_col[i] - gc_t[r][h:h + 1, :], NEG_BIG)) for i, (r, h) in enumerate(items)]
    kb = [k[r][h // 2] * b_col[i] for i, (r, h) in enumerate(items)]
    a_mat = [jnp.where(strict, _dot(kb[i].astype(BF16), k_t[r][h // 2]) * decay[i], 0.0).astype(BF16)
             for i, (r, h) in enumerate(items)]
    t_inv = _unit_lower_inverses(a_mat, eye, diag_mask, merge_masks)

    e_g = [jnp.exp(x) for x in g_col]
    v = [qkv[r][:, 2 * GDN_QK + h * HD:2 * GDN_QK + (h + 1) * HD] for r, h in items]
    rhs = [jnp.concatenate([v[i] * b_col[i], kb[i] * e_g[i]], axis=1).astype(BF16) for i in range(len(items))]
    sol = [_dot2(_split(t_inv[i]), rhs[i]) for i in range(len(items))]
    state = [s_ref[i] for i in range(len(items))]
    ws = [_dot(jnp.concatenate([sol[i][:, HD:2 * HD], q[r][h // 2] * e_g[i]], axis=0).astype(BF16),
               state[i].astype(BF16)) for i, (r, h) in enumerate(items)]
    v_new_b = [(sol[i][:, 0:HD] - ws[i][0:CHUNK]).astype(BF16) for i in range(len(items))]
    o = [ws[i][CHUNK:2 * CHUNK] + _dot((qk_raw[r][h // 2] * decay[i]).astype(BF16), v_new_b[i])
         for i, (r, h) in enumerate(items)]
    k_dec_t = [(k[r][h // 2] * jnp.exp(g_last[i] - g_col[i])).T.astype(BF16) for i, (r, h) in enumerate(items)]
    for i in range(len(items)):
        s_ref[i] = state[i] * jnp.exp(g_last[i]) + _dot(k_dec_t[i], v_new_b[i])
    for i, (r, h) in enumerate(items):
        out = _rms(o[i], ng_ref[...]) * _silu(gz_ref[r, :, h * HD:(h + 1) * HD])
        o_ref[r, :, h * HD:(h + 1) * HD] = out.astype(o_ref.dtype)


GDN_BATCH_PER_STEP = 2


def _gdn(proj, B, S, cw, shift, dtb, alog, ng):
    nc = S // CHUNK
    nb = GDN_BATCH_PER_STEP
    width = GDN_V_HEADS * GDN_HEAD
    proj3 = proj.reshape(B, S, proj.shape[1])
    full = lambda a: pl.BlockSpec(a.shape, lambda b, c: (0,) * a.ndim)
    rows = lambda w, idx: pl.BlockSpec((nb, CHUNK, w), lambda b, c: (b, c, idx))
    consts = [cw, shift, dtb, alog, ng]
    out = pl.pallas_call(
        _gdn_kernel,
        grid=(B // nb, nc),
        in_specs=[rows(W_GQKV, OFF_GQKV // W_GQKV), rows(W_GZ, OFF_GZ // W_GZ),
                  rows(W_SM, OFF_SM // W_SM)] + [full(a) for a in consts],
        out_specs=pl.BlockSpec((nb, CHUNK, width), lambda b, c: (b, c, 0)),
        out_shape=jax.ShapeDtypeStruct((B, S, width), BF16),
        scratch_shapes=[pltpu.VMEM((nb, HALO, GDN_QKV), BF16),
                        pltpu.VMEM((nb * GDN_V_HEADS, GDN_HEAD, GDN_HEAD), F32)],
        compiler_params=_params("parallel", "arbitrary"),
        name="gdn_mixer",
    )(proj3, proj3, proj3, *consts)
    return out.reshape(B * S, width)


ATTN_BLOCK = 512
LOG2E = math.log2(math.e)
BF16_SUBLANES = 16
V_ROWS = MLA_V + BF16_SUBLANES
HEADS_PER_STEP = 4


def _mla_prep_kernel(cq_ref, ckv_ref, kr_ref, cos_ref, sin_ref, qg_ref, kvg_ref, wq_ref, wkv_ref,
                     qt_ref, k_ref, vt_ref):
    H = MLA_HEADS
    tm = cq_ref.shape[0]
    scale = (MLA_NOPE + MLA_ROPE) ** -0.5 * LOG2E
    cos4 = cos_ref[...]
    sin4 = sin_ref[...]
    lane = lax.broadcasted_iota(jnp.int32, (tm, LANES), 1)
    left = lane < MLA_ROPE

    qm = _dot(_rms(cq_ref[...], qg_ref[...]).astype(BF16), wq_ref[...])
    kvm = _dot(_rms(ckv_ref[...], kvg_ref[...]).astype(BF16), wkv_ref[...])

    kr = kr_ref[...]
    k_rope = (kr * jnp.where(left, cos4, sin4)
              + pltpu.roll(kr, MLA_ROPE, axis=1) * jnp.where(left, sin4, cos4)).astype(BF16)

    pe_off = H * MLA_NOPE
    rot_off = pe_off + H * MLA_ROPE
    for h in range(H):
        jb = (h // 2) * LANES
        pe = qm[:, pe_off + jb:pe_off + jb + LANES]
        rot = qm[:, rot_off + jb:rot_off + jb + LANES]
        roped = pe * cos4 + rot * sin4
        mine = left if h % 2 == 0 else jnp.logical_not(left)
        q_nope = qm[:, h * MLA_NOPE:(h + 1) * MLA_NOPE]
        qt_ref[h, 0, 0:LANES, :] = (q_nope * scale).T.astype(BF16)
        qt_ref[h, 0, LANES:2 * LANES, :] = (jnp.where(mine, roped, 0.0) * scale).T.astype(BF16)
        base = h * (MLA_NOPE + MLA_V)
        k_ref[h] = jnp.concatenate([kvm[:, base:base + MLA_NOPE].astype(BF16), k_rope], axis=1)
        vt_ref[h, 0, 0:MLA_V, :] = kvm[:, base + MLA_NOPE:base + MLA_NOPE + MLA_V].T.astype(BF16)
        ones_row = lax.broadcasted_iota(jnp.int32, (BF16_SUBLANES, tm), 0) == 0
        vt_ref[h, 0, MLA_V:V_ROWS, :] = jnp.where(ones_row, 1.0, 0.0).astype(BF16)


def _mla_prep(proj, cos4, sin4, qg, kvg, wq, wkv):
    T = proj.shape[0]
    H = MLA_HEADS
    tm = ATTN_BLOCK
    nt = T // tm
    full = lambda a: pl.BlockSpec(a.shape, lambda i: (0,) * a.ndim)
    rows = lambda w, idx: pl.BlockSpec((tm, w), lambda i: (i, idx))
    consts = [qg, kvg, wq, wkv]
    dq = 2 * LANES
    return pl.pallas_call(
        _mla_prep_kernel,
        grid=(nt,),
        in_specs=[rows(W_CQ, OFF_CQ // W_CQ), rows(W_CKV, OFF_CKV // W_CKV), rows(W_KR, OFF_KR // W_KR),
                  rows(LANES, 0), rows(LANES, 0)] + [full(a) for a in consts],
        out_specs=[pl.BlockSpec((H, 1, dq, tm), lambda i: (0, i, 0, 0)),
                   pl.BlockSpec((H, tm, dq), lambda i: (0, i, 0)),
                   pl.BlockSpec((H, 1, V_ROWS, tm), lambda i: (0, i, 0, 0))],
        out_shape=[jax.ShapeDtypeStruct((H, nt, dq, tm), BF16),
                   jax.ShapeDtypeStruct((H, T, dq), BF16),
                   jax.ShapeDtypeStruct((H, nt, V_ROWS, tm), BF16)],
        compiler_params=_params("parallel"),
        name="mla_prep",
    )(proj, proj, proj, cos4, sin4, *consts)


def _attn_kernel(qt_ref, k_ref, vt_ref, o_ref, m_ref, acc_ref):
    tb = ATTN_BLOCK
    hs = range(HEADS_PER_STEP)
    qi = pl.program_id(2)
    m_ref[...] = jnp.full(m_ref.shape, NEG_BIG, F32)
    acc_ref[...] = jnp.zeros(acc_ref.shape, F32)

    def update(kis, last_is_diagonal):
        nb = range(len(kis))
        s = [[_dot(k_ref[h, pl.ds(pl.multiple_of(ki * tb, tb), tb), :], qt_ref[h, 0]) for ki in kis]
             for h in hs]
        if last_is_diagonal:
            key = lax.broadcasted_iota(jnp.int32, (tb, tb), 0)
            qry = lax.broadcasted_iota(jnp.int32, (tb, tb), 1)
            for h in hs:
                s[h][-1] = jnp.where(key <= qry, s[h][-1], NEG_BIG)
        m_old = [m_ref[h] for h in hs]
        m_new = list(m_old)
        for h in hs:
            for j in nb:
                m_new[h] = jnp.maximum(m_new[h], jnp.max(s[h][j], axis=0, keepdims=True))
        p = [[jnp.exp2(s[h][j] - m_new[h]).astype(BF16) for j in nb] for h in hs]
        pv = [[_dot(vt_ref[h, kis[j]], p[h][j]) for j in nb] for h in hs]
        for h in hs:
            acc_ref[h] = jnp.exp2(m_old[h] - m_new[h]) * acc_ref[h] + sum(pv[h][1:], pv[h][0])
            m_ref[h] = m_new[h]

    def pair(i, carry):
        update([2 * i, 2 * i + 1], False)
        return carry

    lax.fori_loop(0, lax.shift_right_logical(qi, 1), pair, 0)
    odd = (qi & 1) == 1

    @pl.when(odd)
    def _():
        update([qi - 1, qi], True)

    @pl.when(jnp.logical_not(odd))
    def _():
        update([qi], True)

    for h in hs:
        acc = acc_ref[h]
        o_ref[:, h * MLA_V:(h + 1) * MLA_V] = (acc[0:MLA_V] / acc[MLA_V:MLA_V + 1]).T.astype(o_ref.dtype)


def _attention(q_t, k, v_t, B, S):
    H, T, dq = k.shape
    tb = ATTN_BLOCK
    nq = S // tb
    hp = HEADS_PER_STEP
    return pl.pallas_call(
        _attn_kernel,
        grid=(B, H // hp, nq),
        in_specs=[pl.BlockSpec((hp, 1, dq, tb), lambda b, h, i: (h, b * nq + i, 0, 0)),
                  pl.BlockSpec((hp, S, dq), lambda b, h, i: (h, b, 0)),
                  pl.BlockSpec((hp, nq, V_ROWS, tb), lambda b, h, i: (h, b, 0, 0))],
        out_specs=pl.BlockSpec((tb, hp * MLA_V), lambda b, h, i: (b * nq + i, h)),
        out_shape=jax.ShapeDtypeStruct((T, H * MLA_V), BF16),
        scratch_shapes=[pltpu.VMEM((hp, 1, tb), F32), pltpu.VMEM((hp, V_ROWS, tb), F32)],
        compiler_params=_params("parallel", "parallel", "arbitrary"),
        name="mla_attention",
    )(q_t, k, v_t)


def _merge_kernel(x_ref, ys_ref, ym_ref, yg_ref, g_ref, wgate_ref, ws_ref, wm_ref, wg_ref, wo_ref, h_ref):
    D = D_MODEL
    x = x_ref[...]
    xn = _rms(x, g_ref[...]).astype(BF16)
    mixed = None
    for idx, (y_ref, w_ref) in enumerate(((ys_ref, ws_ref), (ym_ref, wm_ref), (yg_ref, wg_ref))):
        gate = _sigmoid(_dot(xn, wgate_ref[:, idx * D:(idx + 1) * D]))
        term = gate * _dot(y_ref[...], w_ref[...])
        mixed = term if mixed is None else mixed + term
    h_ref[...] = x + _dot(mixed.astype(BF16), wo_ref[...])


def _merge(x2, ys, ym, yg, g, wgate, ws, wm, wg, wo):
    T, D = x2.shape
    tm = min(T, 256)
    full = lambda a: pl.BlockSpec(a.shape, lambda i: (0,) * a.ndim)
    rows = pl.BlockSpec((tm, D), lambda i: (i, 0))
    consts = [g, wgate, ws, wm, wg, wo]
    return pl.pallas_call(
        _merge_kernel,
        grid=(T // tm,),
        in_specs=[rows] * 4 + [full(a) for a in consts],
        out_specs=rows,
        out_shape=jax.ShapeDtypeStruct((T, D), F32),
        compiler_params=_params("parallel"),
        name="merge",
    )(x2, ys, ym, yg, *consts)


def _ffn_kernel(h_ref, g_ref, wu_ref, wd_ref, fg_ref, o_ref, hn_ref, acc_ref, *, final_norm):
    j = pl.program_id(1)

    @pl.when(j == 0)
    def _():
        hn_ref[...] = _rms(h_ref[...], g_ref[...]).astype(BF16)
        acc_ref[...] = jnp.zeros(acc_ref.shape, F32)

    up = jnp.maximum(_dot(hn_ref[...], wu_ref[...]), 0.0)
    acc_ref[...] += _dot((up * up).astype(BF16), wd_ref[...])

    @pl.when(j == pl.num_programs(1) - 1)
    def _():
        out = h_ref[...] + acc_ref[...]
        if final_norm:
            out = _rms(out, fg_ref[...])
        o_ref[...] = out


def _ffn(h, g, wu, wd, fg, final_norm):
    T, D = h.shape
    F = wu.shape[1]
    tm = min(T, 1024)
    tf = min(F, 1024)
    return pl.pallas_call(
        functools.partial(_ffn_kernel, final_norm=final_norm),
        grid=(T // tm, F // tf),
        in_specs=[pl.BlockSpec((tm, D), lambda i, j: (i, 0)),
                  pl.BlockSpec((1, D), lambda i, j: (0, 0)),
                  pl.BlockSpec((D, tf), lambda i, j: (0, j)),
                  pl.BlockSpec((tf, D), lambda i, j: (j, 0)),
                  pl.BlockSpec((1, D), lambda i, j: (0, 0))],
        out_specs=pl.BlockSpec((tm, D), lambda i, j: (i, 0)),
        out_shape=jax.ShapeDtypeStruct((T, D), F32),
        scratch_shapes=[pltpu.VMEM((tm, D), BF16), pltpu.VMEM((tm, D), F32)],
        compiler_params=_params("parallel", "arbitrary"),
        name="ffn",
    )(h, g, wu, wd, fg)


def _rotate_half_cols(w):
    half = w.shape[-1] // 2
    return jnp.concatenate([-w[..., half:], w[..., :half]], axis=-1)


def _split_in_proj(w_in):
    sizes = (SSD_INNER, SSD_INNER, 2 * SSD_GROUPS * SSD_STATE, SSD_HEADS, MLA_Q_LORA, MLA_KV_LORA, MLA_ROPE,
             GDN_QKV, GDN_V_HEADS * GDN_HEAD, GDN_V_HEADS, GDN_V_HEADS, 3 * D_MODEL)
    offs = np.cumsum((0,) + sizes)
    return [w_in[:, int(offs[i]):int(offs[i + 1])] for i in range(len(sizes))]


def _layer_weights(w_in, w_uq):
    z, xs, bc, dt, cq, ckv, kr, gqkv, gz, gb, ga, gates = _split_in_proj(w_in)
    small = jnp.concatenate([dt, gb, ga, jnp.zeros((D_MODEL, W_SM - 32), F32)], axis=1)
    w_a = jnp.concatenate([gqkv, z, xs, gz, bc, cq, ckv, kr, _rotate_half_cols(kr), small], axis=1)
    H = MLA_HEADS
    wq = w_uq.reshape(MLA_Q_LORA, H, MLA_NOPE + MLA_ROPE)
    nope = wq[:, :, :MLA_NOPE].reshape(MLA_Q_LORA, H * MLA_NOPE)
    pe = wq[:, :, MLA_NOPE:]
    w_q = jnp.concatenate([nope, pe.reshape(MLA_Q_LORA, H * MLA_ROPE),
                           _rotate_half_cols(pe).reshape(MLA_Q_LORA, H * MLA_ROPE)], axis=1)
    return w_a.astype(BF16), gates.astype(BF16), w_q.astype(BF16)


def _pad_lanes(v, n=LANES):
    return jnp.concatenate([v, jnp.zeros((n - v.shape[0],), v.dtype)])[None, :]


def kernel(x, positions, norm1_g, w_in, ssd_conv_w, ssd_conv_b, ssd_dt_bias, ssd_a_log, ssd_d, ssd_norm_g,
           mla_q_norm_g, mla_w_uq, mla_kv_norm_g, mla_w_ukv, gdn_conv_w, gdn_dt_bias, gdn_a_log, gdn_norm_g,
           w_ssd_out, w_mla_out, w_gdn_out, w_out, norm2_g, w_up, w_down, final_norm_g):
    B, S, D = x.shape
    T = B * S
    x2 = x.reshape(T, D)

    inv = ROPE_THETA ** (-jnp.arange(0, MLA_ROPE, 2, dtype=F32) / MLA_ROPE)
    inv4 = jnp.tile(inv, LANES // inv.shape[0])[None, :]
    cos4, sin4 = _rope_tables(positions.reshape(T, 1), inv4)

    shift = _shift_matrix()
    expand = (jnp.arange(LANES)[:, None] == (jnp.arange(SSD_INNER)[None, :] // SSD_HEAD_DIM)).astype(BF16)

    for l in range(DEPTH):
        w_a, w_gate, w_q = _layer_weights(w_in[l], mla_w_uq[l])
        proj = _inproj(x2, norm1_g[l][None, :], w_a)

        y_ssd = _ssd(proj, B, S, ssd_conv_w[l], ssd_conv_b[l][None, :], shift,
                     _pad_lanes(ssd_dt_bias[l]), _pad_lanes(ssd_a_log[l]),
                     jnp.repeat(ssd_d[l], SSD_HEAD_DIM)[None, :], ssd_norm_g[l][None, :], expand)

        q, k, v = _mla_prep(proj, cos4, sin4, mla_q_norm_g[l][None, :], mla_kv_norm_g[l][None, :],
                            w_q, mla_w_ukv[l].astype(BF16))
        y_mla = _attention(q, k, v, B, S)

        y_gdn = _gdn(proj, B, S, gdn_conv_w[l], shift,
                     jnp.broadcast_to(gdn_dt_bias[l][:, None], (GDN_V_HEADS, CHUNK)),
                     jnp.broadcast_to(gdn_a_log[l][:, None], (GDN_V_HEADS, CHUNK)),
                     gdn_norm_g[l][None, :])

        h = _merge(x2, y_ssd, y_mla, y_gdn, norm1_g[l][None, :], w_gate,
                   w_ssd_out[l].astype(BF16), w_mla_out[l].astype(BF16), w_gdn_out[l].astype(BF16),
                   w_out[l].astype(BF16))
        x2 = _ffn(h, norm2_g[l][None, :], w_up[l].astype(BF16), w_down[l].astype(BF16),
                  final_norm_g[None, :], final_norm=(l == DEPTH - 1))
    return x2.reshape(B, S, D)
```

```python
import functools
import math

import jax
import jax.numpy as jnp
import numpy as np
from jax import lax
from jax.experimental import pallas as pl
from jax.experimental.pallas import tpu as pltpu

F32 = jnp.float32
BF16 = jnp.bfloat16
HIGHEST = lax.Precision.HIGHEST

EPS = 1e-6
D_MODEL = 1024
DEPTH = 2
SSD_HEADS = 16
SSD_HEAD_DIM = 64
SSD_GROUPS = 2
SSD_STATE = 128
SSD_INNER = 1024
MLA_HEADS = 8
MLA_NOPE = 128
MLA_ROPE = 64
MLA_V = 128
MLA_Q_LORA = 512
MLA_KV_LORA = 256
ROPE_THETA = 10000.0
GDN_HEAD = 128
GDN_V_HEADS = 8
GDN_QK_HEADS = 4
GDN_QK = GDN_QK_HEADS * GDN_HEAD
GDN_QKV = 2 * GDN_QK + GDN_V_HEADS * GDN_HEAD
D_FF = 4096
CONV_K = 4

CHUNK = 128
HALO = 16
LANES = 128
VMEM_LIMIT = 56 * 1024 * 1024

W_GQKV, W_Z, W_XS, W_GZ, W_BC, W_CQ, W_CKV, W_KR, W_SM = 2048, 1024, 1024, 1024, 512, 512, 256, 128, 128
OFF_GQKV, OFF_Z, OFF_XS, OFF_GZ, OFF_BC, OFF_CQ, OFF_CKV, OFF_KR, OFF_SM = (
    0, 2048, 3072, 4096, 5120, 5632, 6144, 6400, 6528)
N_PROJ = 6656
SM_DT, SM_B, SM_A = 0, 16, 24

NEG_BIG = -1e30


def _rms(x, g):
    return x * lax.rsqrt(jnp.mean(x * x, axis=-1, keepdims=True) + EPS) * g


def _sigmoid(x):
    return 1.0 / (1.0 + jnp.exp(-x))


def _silu(x):
    h = 0.5 * x
    return h + h * jnp.tanh(h)


def _softplus(x):
    return jnp.maximum(x, 0.0) + jnp.log1p(jnp.exp(-jnp.abs(x)))


def _dot(a, b):
    return jnp.dot(a, b, preferred_element_type=F32)


def _dot_nt(a, b):
    return lax.dot_general(a, b, (((1,), (1,)), ((), ())), preferred_element_type=F32)


def _params(*sem):
    return pltpu.CompilerParams(dimension_semantics=sem, vmem_limit_bytes=VMEM_LIMIT)


def _rope_kernel(pos_ref, inv_ref, cos_ref, sin_ref):
    ang = pos_ref[...].astype(F32) * inv_ref[...]
    cos_ref[...] = jnp.cos(ang)
    sin_ref[...] = jnp.sin(ang)


def _rope_tables(pos, inv4):
    R = pos.shape[0]
    tm = min(R, 2048)
    return pl.pallas_call(
        _rope_kernel,
        grid=(R // tm,),
        in_specs=[pl.BlockSpec((tm, LANES), lambda i: (i, 0)),
                  pl.BlockSpec((1, LANES), lambda i: (0, 0))],
        out_specs=[pl.BlockSpec((tm, LANES), lambda i: (i, 0))] * 2,
        out_shape=[jax.ShapeDtypeStruct((R, LANES), F32)] * 2,
        compiler_params=_params("parallel"),
        name="rope_tables",
    )(pos, inv4)


def _inproj_kernel(x_ref, g_ref, w_ref, o_ref, xn_ref):
    @pl.when(pl.program_id(1) == 0)
    def _():
        xn_ref[...] = _rms(x_ref[...], g_ref[...]).astype(BF16)

    o_ref[...] = _dot(xn_ref[...], w_ref[...])


def _inproj(x2, g, w):
    T, D = x2.shape
    N = w.shape[1]
    tm = min(T, 1024)
    tn = N // 4
    return pl.pallas_call(
        _inproj_kernel,
        grid=(T // tm, N // tn),
        in_specs=[pl.BlockSpec((tm, D), lambda i, j: (i, 0)),
                  pl.BlockSpec((1, D), lambda i, j: (0, 0)),
                  pl.BlockSpec((D, tn), lambda i, j: (0, j))],
        out_specs=pl.BlockSpec((tm, tn), lambda i, j: (i, j)),
        out_shape=jax.ShapeDtypeStruct((T, N), F32),
        scratch_shapes=[pltpu.VMEM((tm, D), BF16)],
        compiler_params=_params("parallel", "arbitrary"),
        name="in_proj",
    )(x2, g, w)


def _shift_matrix():
    r = np.arange((CONV_K - 1) * CHUNK)[:, None]
    c = np.arange(HALO + CHUNK)[None, :]
    return jnp.asarray(c == HALO + (r % CHUNK) - (r // CHUNK + 1), BF16)


def _causal_conv(tail_ref, cur, w_ref, shift_ref):
    cur_b = cur.astype(BF16)
    delayed = _dot(shift_ref[...], jnp.concatenate([tail_ref[...], cur_b], axis=0))
    tail_ref[...] = cur_b[CHUNK - HALO:CHUNK, :]
    acc = w_ref[CONV_K - 1:CONV_K, :] * cur
    for j in range(1, CONV_K):
        acc = acc + w_ref[CONV_K - 1 - j:CONV_K - j, :] * delayed[(j - 1) * CHUNK:j * CHUNK]
    return acc


def _ssd_kernel(z_ref, xs_ref, bc_ref, sm_ref, cw_ref, cb_ref, shift_ref, dtb_ref,
                alog_ref, dsk_ref, ng_ref, e_ref, o_ref, tail_ref, h_ref):
    G, N, P = SSD_GROUPS, SSD_STATE, SSD_HEAD_DIM
    HG = SSD_HEADS // G
    GW = HG * P
    NP = SSD_HEADS // 2
    rows_ = range(z_ref.shape[0])
    gsl = [slice(g * GW, (g + 1) * GW) for g in range(G)]
    row_groups = [(r, g) for r in rows_ for g in range(G)]
    row_heads = [(r, e) for r in rows_ for e in range(SSD_HEADS)]
    row_pairs = [(r, p) for r in rows_ for p in range(NP)]

    @pl.when(pl.program_id(1) == 0)
    def _():
        tail_ref[...] = jnp.zeros(tail_ref.shape, tail_ref.dtype)
        h_ref[...] = jnp.zeros(h_ref.shape, F32)

    raw = [jnp.concatenate([xs_ref[r], bc_ref[r]], axis=1) for r in rows_]
    xbc = [_silu(_causal_conv(tail_ref.at[r], raw[r], cw_ref, shift_ref) + cb_ref[...]) for r in rows_]
    xs = [x[:, 0:SSD_INNER] for x in xbc]
    bc = [x[:, SSD_INNER:] for x in xbc]

    row = lax.broadcasted_iota(jnp.int32, (CHUNK, CHUNK), 0)
    col = lax.broadcasted_iota(jnp.int32, (CHUNK, CHUNK), 1)
    tril = row >= col
    tril_f = tril.astype(F32)
    lane = lax.broadcasted_iota(jnp.int32, (CHUNK, LANES), 1)
    left = lane < P

    neg_a = -jnp.exp(alog_ref[...])
    dt = [_softplus(sm_ref[r] + dtb_ref[...]) for r in rows_]
    a_cum = [jnp.dot(tril_f, d * neg_a, precision=HIGHEST, preferred_element_type=F32) for d in dt]
    a_cum_t = [a.T for a in a_cum]
    dt_t = [d.T for d in dt]
    stack = [jnp.concatenate([jnp.exp(a_cum[r]), dt[r] * jnp.exp(a_cum[r][CHUNK - 1:CHUNK, :] - a_cum[r])], axis=0)
             for r in rows_]
    hi = [x.astype(BF16) for x in stack]
    lo = [(x - h.astype(F32)).astype(BF16) for x, h in zip(stack, hi)]
    ex = [_dot(h, e_ref[...]) + _dot(l, e_ref[...]) for h, l in zip(hi, lo)]
    e_a_x = [x[0:CHUNK] for x in ex]
    sdec_x = [x[CHUNK:2 * CHUNK] for x in ex]

    b_g = [bc[r][:, g * N:(g + 1) * N] for r, g in row_groups]
    c_b = [bc[r][:, (G + g) * N:(G + g + 1) * N].astype(BF16) for r, g in row_groups]
    cb = [_dot_nt(c, b.astype(BF16)) for c, b in zip(c_b, b_g)]
    x_sd = [(xs[r][:, gsl[g]] * sdec_x[r][:, gsl[g]]).astype(BF16) for r, g in row_groups]
    st = [_dot(b.T.astype(BF16), x) for b, x in zip(b_g, x_sd)]
    h_prev = [h_ref[r, :, gsl[g]] for r, g in row_groups]
    y_off = [_dot(c_b[i], h_prev[i].astype(BF16)) * e_a_x[r][:, gsl[g]] for i, (r, g) in enumerate(row_groups)]
    for i, (r, g) in enumerate(row_groups):
        h_ref[r, :, gsl[g]] = h_prev[i] * e_a_x[r][CHUNK - 1:CHUNK, gsl[g]] + st[i]
    seg = [jnp.where(tril, a_cum[r][:, e:e + 1] - a_cum_t[r][e:e + 1, :], NEG_BIG) for r, e in row_heads]
    m = [cb[r * G + e // HG] * jnp.exp(seg[i]) * dt_t[r][e:e + 1, :] for i, (r, e) in enumerate(row_heads)]
    m2 = [jnp.concatenate([m[r * SSD_HEADS + 2 * p], m[r * SSD_HEADS + 2 * p + 1]], axis=1).astype(BF16)
          for r, p in row_pairs]
    xp = [xs[r][:, 2 * p * P:(2 * p + 2) * P] for r, p in row_pairs]
    bd = [jnp.concatenate([jnp.where(left, x, 0.0), jnp.where(left, 0.0, x)], axis=0).astype(BF16)
          for x in xp]
    diag = [_dot(a, b) for a, b in zip(m2, bd)]
    for r in rows_:
        y = (jnp.concatenate(diag[r * NP:(r + 1) * NP], axis=1) + jnp.concatenate(y_off[r * G:(r + 1) * G], axis=1)
             + dsk_ref[...] * xs[r])
        y = y * _silu(z_ref[r])
        o_ref[r] = jnp.concatenate([_rms(y[:, gs], ng_ref[:, gs]) for gs in gsl], axis=1).astype(o_ref.dtype)


SSD_BATCH_PER_STEP = 2


def _ssd(proj, B, S, cw, cb, shift, dtb, alog, dsk, ng, expand):
    nc = S // CHUNK
    nb = SSD_BATCH_PER_STEP
    proj3 = proj.reshape(B, S, proj.shape[1])
    full = lambda a: pl.BlockSpec(a.shape, lambda b, c: (0,) * a.ndim)
    rows = lambda w, idx: pl.BlockSpec((nb, CHUNK, w), lambda b, c: (b, c, idx))
    consts = [cw, cb, shift, dtb, alog, dsk, ng, expand]
    out = pl.pallas_call(
        _ssd_kernel,
        grid=(B // nb, nc),
        in_specs=[rows(W_Z, OFF_Z // W_Z), rows(W_XS, OFF_XS // W_XS), rows(W_BC, OFF_BC // W_BC),
                  rows(W_SM, OFF_SM // W_SM)] + [full(a) for a in consts],
        out_specs=pl.BlockSpec((nb, CHUNK, SSD_INNER), lambda b, c: (b, c, 0)),
        out_shape=jax.ShapeDtypeStruct((B, S, SSD_INNER), BF16),
        scratch_shapes=[pltpu.VMEM((nb, HALO, W_XS + W_BC), BF16),
                        pltpu.VMEM((nb, SSD_STATE, SSD_INNER), F32)],
        compiler_params=_params("parallel", "arbitrary"),
        name="ssd_mixer",
    )(proj3, proj3, proj3, proj3, *consts)
    return out.reshape(B * S, SSD_INNER)


def _l2norm(x):
    return x * lax.rsqrt(jnp.sum(x * x, axis=-1, keepdims=True) + EPS)


def _split(x):
    hi = x.astype(BF16)
    return hi, (x - hi.astype(F32)).astype(BF16)


def _dot3(a, b):
    (ah, al), (bh, bl) = a, b
    return _dot(jnp.concatenate([ah, al, ah], axis=1), jnp.concatenate([bh, bh, bl], axis=0))


def _dot2(a, b):
    return _dot(jnp.concatenate(a, axis=1), jnp.concatenate([b, b], axis=0))


def _dot2r(a, b):
    return _dot(jnp.concatenate([a, a], axis=1), jnp.concatenate(b, axis=0))


SOLVE_BLOCK_LOG2 = 3


def _block_masks(row, col):
    n_levels = int(math.log2(CHUNK)) - SOLVE_BLOCK_LOG2
    diag = (row >> SOLVE_BLOCK_LOG2) == (col >> SOLVE_BLOCK_LOG2)
    merges = []
    for lv in range(n_levels):
        s = SOLVE_BLOCK_LOG2 + lv
        same = (row >> (s + 1)) == (col >> (s + 1))
        lower_left = jnp.where(same, ((row >> s) & 1) - ((col >> s) & 1), 0) == 1
        merges.append(lower_left)
    return diag, merges


def _unit_lower_inverses(mats, eye, diag_mask, merge_masks):
    def as_factor(mask):
        return jnp.where(mask, 1.0, 0.0).astype(BF16)

    eye_b = eye.astype(BF16)
    diag_b = as_factor(diag_mask)
    d1 = [a * diag_b for a in mats]
    d2 = [_split(_dot(d, d)) for d in d1]
    t0 = [eye_b - d for d in d1]
    ts = [t.astype(F32) + _dot2r(t, x) for t, x in zip(t0, d2)]
    d4 = [_split(_dot3(x, x)) for x in d2]
    ts = [t + _dot3(_split(t), x) for t, x in zip(ts, d4)]
    for lv, m in enumerate(merge_masks):
        n = 1 << (SOLVE_BLOCK_LOG2 + lv)
        pairs = range(CHUNK // (2 * n))

        def take(x):
            return jnp.concatenate([x[(2 * b + 1) * n:(2 * b + 2) * n] for b in pairs], axis=0)

        def put(x, part):
            pieces = []
            for b in pairs:
                pieces += [x[2 * b * n:(2 * b + 1) * n], part[b * n:(b + 1) * n]]
            return jnp.concatenate(pieces, axis=0)

        tsp = [_split(t) for t in ts]
        mb = as_factor(m)
        low = [take(t) for t in ts]
        mid = [_split(_dot2(_split(x), a * mb)) for x, a in zip(low, mats)]
        ts = [put(t, x - _dot3(md, tp)) for t, x, md, tp in zip(ts, low, mid, tsp)]
    return ts


def _gdn_kernel(qkv_ref, gz_ref, sm_ref, cw_ref, shift_ref, dtb_ref, alog_ref, ng_ref, o_ref, tail_ref, s_ref):
    HD = GDN_HEAD
    NH = GDN_V_HEADS
    NB = qkv_ref.shape[0]
    rows_ = range(NB)
    items = [(r, h) for r in rows_ for h in range(NH)]

    @pl.when(pl.program_id(1) == 0)
    def _():
        tail_ref[...] = jnp.zeros(tail_ref.shape, tail_ref.dtype)
        s_ref[...] = jnp.zeros(s_ref.shape, F32)

    qkv = [_silu(_causal_conv(tail_ref.at[r], qkv_ref[r], cw_ref, shift_ref)) for r in rows_]

    row = lax.broadcasted_iota(jnp.int32, (CHUNK, CHUNK), 0)
    col = lax.broadcasted_iota(jnp.int32, (CHUNK, CHUNK), 1)
    incl = row >= col
    strict = row > col
    eye = (row == col).astype(F32)
    diag_mask, merge_masks = _block_masks(row, col)
    upper = (row <= col).astype(F32)
    pad = jnp.zeros((CHUNK - 2 * NH, CHUNK), F32)

    sm_t = [sm_ref[r].T for r in rows_]
    beta_t = [_sigmoid(x[SM_B:SM_B + NH, :]) for x in sm_t]
    g_t = [-jnp.exp(alog_ref[...]) * _softplus(x[SM_A:SM_A + NH, :] + dtb_ref[...]) for x in sm_t]
    gc_t = [jnp.dot(x, upper, precision=HIGHEST, preferred_element_type=F32) for x in g_t]
    cols = [jnp.concatenate([beta_t[r], gc_t[r], pad], axis=0).T for r in rows_]

    q = [[_l2norm(qkv[r][:, j * HD:(j + 1) * HD]) * (HD ** -0.5) for j in range(GDN_QK_HEADS)] for r in rows_]
    k = [[_l2norm(qkv[r][:, GDN_QK + j * HD:GDN_QK + (j + 1) * HD]) for j in range(GDN_QK_HEADS)] for r in rows_]
    k_t = [[x.T.astype(BF16) for x in k[r]] for r in rows_]
    qk_raw = [[_dot(a.astype(BF16), b) for a, b in zip(q[r], k_t[r])] for r in rows_]

    b_col = [cols[r][:, h:h + 1] for r, h in items]
    g_col = [cols[r][:, NH + h:NH + h + 1] for r, h in items]
    g_last = [gc_t[r][h:h + 1, CHUNK - 1:CHUNK] for r, h in items]
    decay = [jnp.exp(jnp.where(incl, g_col[i] - gc_t[r][h:h + 1, :], NEG_BIG)) for i, (r, h) in enumerate(items)]
    kb = [k[r][h // 2] * b_col[i] for i, (r, h) in enumerate(items)]
    a_mat = [jnp.where(strict, _dot(kb[i].astype(BF16), k_t[r][h // 2]) * decay[i], 0.0).astype(BF16)
             for i, (r, h) in enumerate(items)]
    t_inv = _unit_lower_inverses(a_mat, eye, diag_mask, merge_masks)

    e_g = [jnp.exp(x) for x in g_col]
    v = [qkv[r][:, 2 * GDN_QK + h * HD:2 * GDN_QK + (h + 1) * HD] for r, h in items]
    rhs = [jnp.concatenate([v[i] * b_col[i], kb[i] * e_g[i]], axis=1).astype(BF16) for i in range(len(items))]
    sol = [_dot2(_split(t_inv[i]), rhs[i]) for i in range(len(items))]
    state = [s_ref[i] for i in range(len(items))]
    ws = [_dot(jnp.concatenate([sol[i][:, HD:2 * HD], q[r][h // 2] * e_g[i]], axis=0).astype(BF16),
               state[i].astype(BF16)) for i, (r, h) in enumerate(items)]
    v_new_b = [(sol[i][:, 0:HD] - ws[i][0:CHUNK]).astype(BF16) for i in range(len(items))]
    o = [ws[i][CHUNK:2 * CHUNK] + _dot((qk_raw[r][h // 2] * decay[i]).astype(BF16), v_new_b[i])
         for i, (r, h) in enumerate(items)]
    k_dec_t = [(k[r][h // 2] * jnp.exp(g_last[i] - g_col[i])).T.astype(BF16) for i, (r, h) in enumerate(items)]
    for i in range(len(items)):
        s_ref[i] = state[i] * jnp.exp(g_last[i]) + _dot(k_dec_t[i], v_new_b[i])
    for i, (r, h) in enumerate(items):
        out = _rms(o[i], ng_ref[...]) * _silu(gz_ref[r, :, h * HD:(h + 1) * HD])
        o_ref[r, :, h * HD:(h + 1) * HD] = out.astype(o_ref.dtype)


GDN_BATCH_PER_STEP = 2


def _gdn(proj, B, S, cw, shift, dtb, alog, ng):
    nc = S // CHUNK
    nb = GDN_BATCH_PER_STEP
    width = GDN_V_HEADS * GDN_HEAD
    proj3 = proj.reshape(B, S, proj.shape[1])
    full = lambda a: pl.BlockSpec(a.shape, lambda b, c: (0,) * a.ndim)
    rows = lambda w, idx: pl.BlockSpec((nb, CHUNK, w), lambda b, c: (b, c, idx))
    consts = [cw, shift, dtb, alog, ng]
    out = pl.pallas_call(
        _gdn_kernel,
        grid=(B // nb, nc),
        in_specs=[rows(W_GQKV, OFF_GQKV // W_GQKV), rows(W_GZ, OFF_GZ // W_GZ),
                  rows(W_SM, OFF_SM // W_SM)] + [full(a) for a in consts],
        out_specs=pl.BlockSpec((nb, CHUNK, width), lambda b, c: (b, c, 0)),
        out_shape=jax.ShapeDtypeStruct((B, S, width), BF16),
        scratch_shapes=[pltpu.VMEM((nb, HALO, GDN_QKV), BF16),
                        pltpu.VMEM((nb * GDN_V_HEADS, GDN_HEAD, GDN_HEAD), F32)],
        compiler_params=_params("parallel", "arbitrary"),
        name="gdn_mixer",
    )(proj3, proj3, proj3, *consts)
    return out.reshape(B * S, width)


ATTN_BLOCK = 512
LOG2E = math.log2(math.e)
BF16_SUBLANES = 16
V_ROWS = MLA_V + BF16_SUBLANES
HEADS_PER_STEP = 4


def _mla_prep_kernel(cq_ref, ckv_ref, kr_ref, cos_ref, sin_ref, qg_ref, kvg_ref, wq_ref, wkv_ref,
                     qt_ref, k_ref, vt_ref):
    H = MLA_HEADS
    tm = cq_ref.shape[0]
    scale = (MLA_NOPE + MLA_ROPE) ** -0.5 * LOG2E
    cos4 = cos_ref[...]
    sin4 = sin_ref[...]
    lane = lax.broadcasted_iota(jnp.int32, (tm, LANES), 1)
    left = lane < MLA_ROPE

    qm = _dot(_rms(cq_ref[...], qg_ref[...]).astype(BF16), wq_ref[...])
    kvm = _dot(_rms(ckv_ref[...], kvg_ref[...]).astype(BF16), wkv_ref[...])

    kr = kr_ref[...]
    k_rope = (kr * jnp.where(left, cos4, sin4)
              + pltpu.roll(kr, MLA_ROPE, axis=1) * jnp.where(left, sin4, cos4)).astype(BF16)

    pe_off = H * MLA_NOPE
    rot_off = pe_off + H * MLA_ROPE
    for h in range(H):
        jb = (h // 2) * LANES
        pe = qm[:, pe_off + jb:pe_off + jb + LANES]
        rot = qm[:, rot_off + jb:rot_off + jb + LANES]
        roped = pe * cos4 + rot * sin4
        mine = left if h % 2 == 0 else jnp.logical_not(left)
        q_nope = qm[:, h * MLA_NOPE:(h + 1) * MLA_NOPE]
        qt_ref[h, 0, 0:LANES, :] = (q_nope * scale).T.astype(BF16)
        qt_ref[h, 0, LANES:2 * LANES, :] = (jnp.where(mine, roped, 0.0) * scale).T.astype(BF16)
        base = h * (MLA_NOPE + MLA_V)
        k_ref[h] = jnp.concatenate([kvm[:, base:base + MLA_NOPE].astype(BF16), k_rope], axis=1)
        vt_ref[h, 0, 0:MLA_V, :] = kvm[:, base + MLA_NOPE:base + MLA_NOPE + MLA_V].T.astype(BF16)
        ones_row = lax.broadcasted_iota(jnp.int32, (BF16_SUBLANES, tm), 0) == 0
        vt_ref[h, 0, MLA_V:V_ROWS, :] = jnp.where(ones_row, 1.0, 0.0).astype(BF16)


def _mla_prep(proj, cos4, sin4, qg, kvg, wq, wkv):
    T = proj.shape[0]
    H = MLA_HEADS
    tm = ATTN_BLOCK
    nt = T // tm
    full = lambda a: pl.BlockSpec(a.shape, lambda i: (0,) * a.ndim)
    rows = lambda w, idx: pl.BlockSpec((tm, w), lambda i: (i, idx))
    consts = [qg, kvg, wq, wkv]
    dq = 2 * LANES
    return pl.pallas_call(
        _mla_prep_kernel,
        grid=(nt,),
        in_specs=[rows(W_CQ, OFF_CQ // W_CQ), rows(W_CKV, OFF_CKV // W_CKV), rows(W_KR, OFF_KR // W_KR),
                  rows(LANES, 0), rows(LANES, 0)] + [full(a) for a in consts],
        out_specs=[pl.BlockSpec((H, 1, dq, tm), lambda i: (0, i, 0, 0)),
                   pl.BlockSpec((H, tm, dq), lambda i: (0, i, 0)),
                   pl.BlockSpec((H, 1, V_ROWS, tm), lambda i: (0, i, 0, 0))],
        out_shape=[jax.ShapeDtypeStruct((H, nt, dq, tm), BF16),
                   jax.ShapeDtypeStruct((H, T, dq), BF16),
                   jax.ShapeDtypeStruct((H, nt, V_ROWS, tm), BF16)],
        compiler_params=_params("parallel"),
        name="mla_prep",
    )(proj, proj, proj, cos4, sin4, *consts)


def _attn_kernel(qt_ref, k_ref, vt_ref, o_ref, m_ref, acc_ref):
    tb = ATTN_BLOCK
    hs = range(HEADS_PER_STEP)
    qi = pl.program_id(2)
    m_ref[...] = jnp.full(m_ref.shape, NEG_BIG, F32)
    acc_ref[...] = jnp.zeros(acc_ref.shape, F32)

    def update(kis, last_is_diagonal):
        nb = range(len(kis))
        s = [[_dot(k_ref[h, pl.ds(pl.multiple_of(ki * tb, tb), tb), :], qt_ref[h, 0]) for ki in kis]
             for h in hs]
        if last_is_diagonal:
            key = lax.broadcasted_iota(jnp.int32, (tb, tb), 0)
            qry = lax.broadcasted_iota(jnp.int32, (tb, tb), 1)
            for h in hs:
                s[h][-1] = jnp.where(key <= qry, s[h][-1], NEG_BIG)
        m_old = [m_ref[h] for h in hs]
        m_new = list(m_old)
        for h in hs:
            for j in nb:
                m_new[h] = jnp.maximum(m_new[h], jnp.max(s[h][j], axis=0, keepdims=True))
        p = [[jnp.exp2(s[h][j] - m_new[h]).astype(BF16) for j in nb] for h in hs]
        pv = [[_dot(vt_ref[h, kis[j]], p[h][j]) for j in nb] for h in hs]
        for h in hs:
            acc_ref[h] = jnp.exp2(m_old[h] - m_new[h]) * acc_ref[h] + sum(pv[h][1:], pv[h][0])
            m_ref[h] = m_new[h]

    def pair(i, carry):
        update([2 * i, 2 * i + 1], False)
        return carry

    lax.fori_loop(0, lax.shift_right_logical(qi, 1), pair, 0)
    odd = (qi & 1) == 1

    @pl.when(odd)
    def _():
        update([qi - 1, qi], True)

    @pl.when(jnp.logical_not(odd))
    def _():
        update([qi], True)

    for h in hs:
        acc = acc_ref[h]
        o_ref[:, h * MLA_V:(h + 1) * MLA_V] = (acc[0:MLA_V] / acc[MLA_V:MLA_V + 1]).T.astype(o_ref.dtype)


def _attention(q_t, k, v_t, B, S):
    H, T, dq = k.shape
    tb = ATTN_BLOCK
    nq = S // tb
    hp = HEADS_PER_STEP
    return pl.pallas_call(
        _attn_kernel,
        grid=(B, H // hp, nq),
        in_specs=[pl.BlockSpec((hp, 1, dq, tb), lambda b, h, i: (h, b * nq + i, 0, 0)),
                  pl.BlockSpec((hp, S, dq), lambda b, h, i: (h, b, 0)),
                  pl.BlockSpec((hp, nq, V_ROWS, tb), lambda b, h, i: (h, b, 0, 0))],
        out_specs=pl.BlockSpec((tb, hp * MLA_V), lambda b, h, i: (b * nq + i, h)),
        out_shape=jax.ShapeDtypeStruct((T, H * MLA_V), BF16),
        scratch_shapes=[pltpu.VMEM((hp, 1, tb), F32), pltpu.VMEM((hp, V_ROWS, tb), F32)],
        compiler_params=_params("parallel", "parallel", "arbitrary"),
        name="mla_attention",
    )(q_t, k, v_t)


def _merge_kernel(x_ref, ys_ref, ym_ref, yg_ref, g_ref, wgate_ref, ws_ref, wm_ref, wg_ref, wo_ref, h_ref):
    D = D_MODEL
    x = x_ref[...]
    xn = _rms(x, g_ref[...]).astype(BF16)
    mixed = None
    for idx, (y_ref, w_ref) in enumerate(((ys_ref, ws_ref), (ym_ref, wm_ref), (yg_ref, wg_ref))):
        gate = _sigmoid(_dot(xn, wgate_ref[:, idx * D:(idx + 1) * D]))
        term = gate * _dot(y_ref[...], w_ref[...])
        mixed = term if mixed is None else mixed + term
    h_ref[...] = x + _dot(mixed.astype(BF16), wo_ref[...])


def _merge(x2, ys, ym, yg, g, wgate, ws, wm, wg, wo):
    T, D = x2.shape
    tm = min(T, 256)
    full = lambda a: pl.BlockSpec(a.shape, lambda i: (0,) * a.ndim)
    rows = pl.BlockSpec((tm, D), lambda i: (i, 0))
    consts = [g, wgate, ws, wm, wg, wo]
    return pl.pallas_call(
        _merge_kernel,
        grid=(T // tm,),
        in_specs=[rows] * 4 + [full(a) for a in consts],
        out_specs=rows,
        out_shape=jax.ShapeDtypeStruct((T, D), F32),
        compiler_params=_params("parallel"),
        name="merge",
    )(x2, ys, ym, yg, *consts)


def _ffn_kernel(h_ref, g_ref, wu_ref, wd_ref, fg_ref, o_ref, hn_ref, acc_ref, *, final_norm):
    j = pl.program_id(1)

    @pl.when(j == 0)
    def _():
        hn_ref[...] = _rms(h_ref[...], g_ref[...]).astype(BF16)
        acc_ref[...] = jnp.zeros(acc_ref.shape, F32)

    up = jnp.maximum(_dot(hn_ref[...], wu_ref[...]), 0.0)
    acc_ref[...] += _dot((up * up).astype(BF16), wd_ref[...])

    @pl.when(j == pl.num_programs(1) - 1)
    def _():
        out = h_ref[...] + acc_ref[...]
        if final_norm:
            out = _rms(out, fg_ref[...])
        o_ref[...] = out


def _ffn(h, g, wu, wd, fg, final_norm):
    T, D = h.shape
    F = wu.shape[1]
    tm = min(T, 1024)
    tf = min(F, 1024)
    return pl.pallas_call(
        functools.partial(_ffn_kernel, final_norm=final_norm),
        grid=(T // tm, F // tf),
        in_specs=[pl.BlockSpec((tm, D), lambda i, j: (i, 0)),
                  pl.BlockSpec((1, D), lambda i, j: (0, 0)),
                  pl.BlockSpec((D, tf), lambda i, j: (0, j)),
                  pl.BlockSpec((tf, D), lambda i, j: (j, 0)),
                  pl.BlockSpec((1, D), lambda i, j: (0, 0))],
        out_specs=pl.BlockSpec((tm, D), lambda i, j: (i, 0)),
        out_shape=jax.ShapeDtypeStruct((T, D), F32),
        scratch_shapes=[pltpu.VMEM((tm, D), BF16), pltpu.VMEM((tm, D), F32)],
        compiler_params=_params("parallel", "arbitrary"),
        name="ffn",
    )(h, g, wu, wd, fg)


def _rotate_half_cols(w):
    half = w.shape[-1] // 2
    return jnp.concatenate([-w[..., half:], w[..., :half]], axis=-1)


def _split_in_proj(w_in):
    sizes = (SSD_INNER, SSD_INNER, 2 * SSD_GROUPS * SSD_STATE, SSD_HEADS, MLA_Q_LORA, MLA_KV_LORA, MLA_ROPE,
             GDN_QKV, GDN_V_HEADS * GDN_HEAD, GDN_V_HEADS, GDN_V_HEADS, 3 * D_MODEL)
    offs = np.cumsum((0,) + sizes)
    return [w_in[:, int(offs[i]):int(offs[i + 1])] for i in range(len(sizes))]


def _layer_weights(w_in, w_uq):
    z, xs, bc, dt, cq, ckv, kr, gqkv, gz, gb, ga, gates = _split_in_proj(w_in)
    small = jnp.concatenate([dt, gb, ga, jnp.zeros((D_MODEL, W_SM - 32), w_in.dtype)], axis=1)
    w_a = jnp.concatenate([gqkv, z, xs, gz, bc, cq, ckv, kr, _rotate_half_cols(kr), small], axis=1)
    H = MLA_HEADS
    wq = w_uq.reshape(MLA_Q_LORA, H, MLA_NOPE + MLA_ROPE)
    nope = wq[:, :, :MLA_NOPE].reshape(MLA_Q_LORA, H * MLA_NOPE)
    pe = wq[:, :, MLA_NOPE:]
    w_q = jnp.concatenate([nope, pe.reshape(MLA_Q_LORA, H * MLA_ROPE),
                           _rotate_half_cols(pe).reshape(MLA_Q_LORA, H * MLA_ROPE)], axis=1)
    return w_a, gates, w_q


def _pad_lanes(v, n=LANES):
    return jnp.concatenate([v, jnp.zeros((n - v.shape[0],), v.dtype)])[None, :]


def kernel(x, positions, norm1_g, w_in, ssd_conv_w, ssd_conv_b, ssd_dt_bias, ssd_a_log, ssd_d, ssd_norm_g,
           mla_q_norm_g, mla_w_uq, mla_kv_norm_g, mla_w_ukv, gdn_conv_w, gdn_dt_bias, gdn_a_log, gdn_norm_g,
           w_ssd_out, w_mla_out, w_gdn_out, w_out, norm2_g, w_up, w_down, final_norm_g):
    B, S, D = x.shape
    T = B * S
    x2 = x.reshape(T, D)

    inv = ROPE_THETA ** (-jnp.arange(0, MLA_ROPE, 2, dtype=F32) / MLA_ROPE)
    n_ang = inv.shape[0]
    rep = LANES // n_ang
    inv4 = jnp.tile(inv, rep)[None, :]
    pos_rows = jnp.repeat(positions.reshape(T // rep, rep), n_ang, axis=1)
    cos_t, sin_t = _rope_tables(pos_rows, inv4)
    cos4 = jnp.tile(cos_t.reshape(T, n_ang), (1, rep))
    sin4 = jnp.tile(sin_t.reshape(T, n_ang), (1, rep))

    shift = _shift_matrix()
    w_in_b = w_in.astype(BF16)
    w_uq_b = mla_w_uq.astype(BF16)
    expand = (jnp.arange(LANES)[:, None] == (jnp.arange(SSD_INNER)[None, :] // SSD_HEAD_DIM)).astype(BF16)

    for l in range(DEPTH):
        w_a, w_gate, w_q = _layer_weights(w_in_b[l], w_uq_b[l])
        proj = _inproj(x2, norm1_g[l][None, :], w_a)

        y_ssd = _ssd(proj, B, S, ssd_conv_w[l], ssd_conv_b[l][None, :], shift,
                     _pad_lanes(ssd_dt_bias[l]), _pad_lanes(ssd_a_log[l]),
                     jnp.repeat(ssd_d[l], SSD_HEAD_DIM)[None, :], ssd_norm_g[l][None, :], expand)

        q, k, v = _mla_prep(proj, cos4, sin4, mla_q_norm_g[l][None, :], mla_kv_norm_g[l][None, :],
                            w_q, mla_w_ukv[l].astype(BF16))
        y_mla = _attention(q, k, v, B, S)

        y_gdn = _gdn(proj, B, S, gdn_conv_w[l], shift,
                     jnp.broadcast_to(gdn_dt_bias[l][:, None], (GDN_V_HEADS, CHUNK)),
                     jnp.broadcast_to(gdn_a_log[l][:, None], (GDN_V_HEADS, CHUNK)),
                     gdn_norm_g[l][None, :])

        h = _merge(x2, y_ssd, y_mla, y_gdn, norm1_g[l][None, :], w_gate,
                   w_ssd_out[l].astype(BF16), w_mla_out[l].astype(BF16), w_gdn_out[l].astype(BF16),
                   w_out[l].astype(BF16))
        x2 = _ffn(h, norm2_g[l][None, :], w_up[l].astype(BF16), w_down[l].astype(BF16),
                  final_norm_g[None, :], final_norm=(l == DEPTH - 1))
    return x2.reshape(B, S, D)
```

```python
import functools
import math

import jax
import jax.numpy as jnp
import numpy as np
from jax import lax
from jax.experimental import pallas as pl
from jax.experimental.pallas import tpu as pltpu

F32 = jnp.float32
BF16 = jnp.bfloat16
HIGHEST = lax.Precision.HIGHEST

EPS = 1e-6
D_MODEL = 1024
DEPTH = 2
SSD_HEADS = 16
SSD_HEAD_DIM = 64
SSD_GROUPS = 2
SSD_STATE = 128
SSD_INNER = 1024
MLA_HEADS = 8
MLA_NOPE = 128
MLA_ROPE = 64
MLA_V = 128
MLA_Q_LORA = 512
MLA_KV_LORA = 256
ROPE_THETA = 10000.0
GDN_HEAD = 128
GDN_V_HEADS = 8
GDN_QK_HEADS = 4
GDN_QK = GDN_QK_HEADS * GDN_HEAD
GDN_QKV = 2 * GDN_QK + GDN_V_HEADS * GDN_HEAD
D_FF = 4096
CONV_K = 4

CHUNK = 128
HALO = 16
LANES = 128
VMEM_LIMIT = 56 * 1024 * 1024

W_GQKV, W_Z, W_XS, W_GZ, W_BC, W_CQ, W_CKV, W_KR, W_SM = 2048, 1024, 1024, 1024, 512, 512, 256, 128, 128
OFF_GQKV, OFF_Z, OFF_XS, OFF_GZ, OFF_BC, OFF_CQ, OFF_CKV, OFF_KR, OFF_SM = (
    0, 2048, 3072, 4096, 5120, 5632, 6144, 6400, 6528)
N_PROJ = 6656
SM_DT, SM_B, SM_A = 0, 16, 24

NEG_BIG = -1e30


def _rms(x, g):
    return x * lax.rsqrt(jnp.mean(x * x, axis=-1, keepdims=True) + EPS) * g


def _sigmoid(x):
    return 1.0 / (1.0 + jnp.exp(-x))


def _silu(x):
    h = 0.5 * x
    return h + h * jnp.tanh(h)


def _softplus(x):
    return jnp.maximum(x, 0.0) + jnp.log1p(jnp.exp(-jnp.abs(x)))


def _dot(a, b):
    return jnp.dot(a, b, preferred_element_type=F32)


def _dot_nt(a, b):
    return lax.dot_general(a, b, (((1,), (1,)), ((), ())), preferred_element_type=F32)


def _params(*sem):
    return pltpu.CompilerParams(dimension_semantics=sem, vmem_limit_bytes=VMEM_LIMIT)


def _rope_kernel(pos_ref, inv_ref, cos_ref, sin_ref):
    ang = pos_ref[...].astype(F32) * inv_ref[...]
    cos_ref[...] = jnp.cos(ang)
    sin_ref[...] = jnp.sin(ang)


def _rope_tables(pos, inv4):
    R = pos.shape[0]
    tm = min(R, 2048)
    return pl.pallas_call(
        _rope_kernel,
        grid=(R // tm,),
        in_specs=[pl.BlockSpec((tm, 1), lambda i: (i, 0)),
                  pl.BlockSpec((1, LANES), lambda i: (0, 0))],
        out_specs=[pl.BlockSpec((tm, LANES), lambda i: (i, 0))] * 2,
        out_shape=[jax.ShapeDtypeStruct((R, LANES), F32)] * 2,
        compiler_params=_params("parallel"),
        name="rope_tables",
    )(pos, inv4)


def _inproj_kernel(x_ref, g_ref, w_ref, o_ref, xn_ref):
    @pl.when(pl.program_id(1) == 0)
    def _():
        xn_ref[...] = _rms(x_ref[...], g_ref[...]).astype(BF16)

    o_ref[...] = _dot(xn_ref[...], w_ref[...])


def _inproj(x2, g, w):
    T, D = x2.shape
    N = w.shape[1]
    tm = min(T, 1024)
    tn = N // 4
    return pl.pallas_call(
        _inproj_kernel,
        grid=(T // tm, N // tn),
        in_specs=[pl.BlockSpec((tm, D), lambda i, j: (i, 0)),
                  pl.BlockSpec((1, D), lambda i, j: (0, 0)),
                  pl.BlockSpec((D, tn), lambda i, j: (0, j))],
        out_specs=pl.BlockSpec((tm, tn), lambda i, j: (i, j)),
        out_shape=jax.ShapeDtypeStruct((T, N), F32),
        scratch_shapes=[pltpu.VMEM((tm, D), BF16)],
        compiler_params=_params("parallel", "arbitrary"),
        name="in_proj",
    )(x2, g, w)


def _shift_matrix():
    r = np.arange((CONV_K - 1) * CHUNK)[:, None]
    c = np.arange(HALO + CHUNK)[None, :]
    return jnp.asarray(c == HALO + (r % CHUNK) - (r // CHUNK + 1), BF16)


def _causal_conv(tail_ref, cur, w_ref, shift_ref):
    cur_b = cur.astype(BF16)
    delayed = _dot(shift_ref[...], jnp.concatenate([tail_ref[...], cur_b], axis=0))
    tail_ref[...] = cur_b[CHUNK - HALO:CHUNK, :]
    acc = w_ref[CONV_K - 1:CONV_K, :] * cur
    for j in range(1, CONV_K):
        acc = acc + w_ref[CONV_K - 1 - j:CONV_K - j, :] * delayed[(j - 1) * CHUNK:j * CHUNK]
    return acc


def _ssd_kernel(z_ref, xs_ref, bc_ref, sm_ref, cw_ref, cb_ref, shift_ref, dtb_ref,
                alog_ref, dsk_ref, ng_ref, e_ref, o_ref, tail_ref, h_ref):
    G, N, P = SSD_GROUPS, SSD_STATE, SSD_HEAD_DIM
    HG = SSD_HEADS // G
    GW = HG * P
    NP = SSD_HEADS // 2
    rows_ = range(z_ref.shape[0])
    gsl = [slice(g * GW, (g + 1) * GW) for g in range(G)]
    row_groups = [(r, g) for r in rows_ for g in range(G)]
    row_heads = [(r, e) for r in rows_ for e in range(SSD_HEADS)]
    row_pairs = [(r, p) for r in rows_ for p in range(NP)]

    @pl.when(pl.program_id(1) == 0)
    def _():
        tail_ref[...] = jnp.zeros(tail_ref.shape, tail_ref.dtype)
        h_ref[...] = jnp.zeros(h_ref.shape, F32)

    raw = [jnp.concatenate([xs_ref[r], bc_ref[r]], axis=1) for r in rows_]
    xbc = [_silu(_causal_conv(tail_ref.at[r], raw[r], cw_ref, shift_ref) + cb_ref[...]) for r in rows_]
    xs = [x[:, 0:SSD_INNER] for x in xbc]
    bc = [x[:, SSD_INNER:] for x in xbc]

    row = lax.broadcasted_iota(jnp.int32, (CHUNK, CHUNK), 0)
    col = lax.broadcasted_iota(jnp.int32, (CHUNK, CHUNK), 1)
    tril = row >= col
    tril_f = tril.astype(F32)
    lane = lax.broadcasted_iota(jnp.int32, (CHUNK, LANES), 1)
    left = lane < P

    neg_a = -jnp.exp(alog_ref[...])
    dt = [_softplus(sm_ref[r] + dtb_ref[...]) for r in rows_]
    a_cum = [jnp.dot(tril_f, d * neg_a, precision=HIGHEST, preferred_element_type=F32) for d in dt]
    a_cum_t = [a.T for a in a_cum]
    dt_t = [d.T for d in dt]
    stack = [jnp.concatenate([jnp.exp(a_cum[r]), dt[r] * jnp.exp(a_cum[r][CHUNK - 1:CHUNK, :] - a_cum[r])], axis=0)
             for r in rows_]
    hi = [x.astype(BF16) for x in stack]
    lo = [(x - h.astype(F32)).astype(BF16) for x, h in zip(stack, hi)]
    ex = [_dot(h, e_ref[...]) + _dot(l, e_ref[...]) for h, l in zip(hi, lo)]
    e_a_x = [x[0:CHUNK] for x in ex]
    sdec_x = [x[CHUNK:2 * CHUNK] for x in ex]

    b_g = [bc[r][:, g * N:(g + 1) * N] for r, g in row_groups]
    c_b = [bc[r][:, (G + g) * N:(G + g + 1) * N].astype(BF16) for r, g in row_groups]
    cb = [_dot_nt(c, b.astype(BF16)) for c, b in zip(c_b, b_g)]
    x_sd = [(xs[r][:, gsl[g]] * sdec_x[r][:, gsl[g]]).astype(BF16) for r, g in row_groups]
    st = [_dot(b.T.astype(BF16), x) for b, x in zip(b_g, x_sd)]
    h_prev = [h_ref[r, :, gsl[g]] for r, g in row_groups]
    y_off = [_dot(c_b[i], h_prev[i].astype(BF16)) * e_a_x[r][:, gsl[g]] for i, (r, g) in enumerate(row_groups)]
    for i, (r, g) in enumerate(row_groups):
        h_ref[r, :, gsl[g]] = h_prev[i] * e_a_x[r][CHUNK - 1:CHUNK, gsl[g]] + st[i]
    seg = [jnp.where(tril, a_cum[r][:, e:e + 1] - a_cum_t[r][e:e + 1, :], NEG_BIG) for r, e in row_heads]
    m = [cb[r * G + e // HG] * jnp.exp(seg[i]) * dt_t[r][e:e + 1, :] for i, (r, e) in enumerate(row_heads)]
    m2 = [jnp.concatenate([m[r * SSD_HEADS + 2 * p], m[r * SSD_HEADS + 2 * p + 1]], axis=1).astype(BF16)
          for r, p in row_pairs]
    xp = [xs[r][:, 2 * p * P:(2 * p + 2) * P] for r, p in row_pairs]
    bd = [jnp.concatenate([jnp.where(left, x, 0.0), jnp.where(left, 0.0, x)], axis=0).astype(BF16)
          for x in xp]
    diag = [_dot(a, b) for a, b in zip(m2, bd)]
    for r in rows_:
        y = (jnp.concatenate(diag[r * NP:(r + 1) * NP], axis=1) + jnp.concatenate(y_off[r * G:(r + 1) * G], axis=1)
             + dsk_ref[...] * xs[r])
        y = y * _silu(z_ref[r])
        o_ref[r] = jnp.concatenate([_rms(y[:, gs], ng_ref[:, gs]) for gs in gsl], axis=1).astype(o_ref.dtype)


SSD_BATCH_PER_STEP = 2


def _ssd(proj, B, S, cw, cb, shift, dtb, alog, dsk, ng, expand):
    nc = S // CHUNK
    nb = SSD_BATCH_PER_STEP
    proj3 = proj.reshape(B, S, proj.shape[1])
    full = lambda a: pl.BlockSpec(a.shape, lambda b, c: (0,) * a.ndim)
    rows = lambda w, idx: pl.BlockSpec((nb, CHUNK, w), lambda b, c: (b, c, idx))
    consts = [cw, cb, shift, dtb, alog, dsk, ng, expand]
    out = pl.pallas_call(
        _ssd_kernel,
        grid=(B // nb, nc),
        in_specs=[rows(W_Z, OFF_Z // W_Z), rows(W_XS, OFF_XS // W_XS), rows(W_BC, OFF_BC // W_BC),
                  rows(W_SM, OFF_SM // W_SM)] + [full(a) for a in consts],
        out_specs=pl.BlockSpec((nb, CHUNK, SSD_INNER), lambda b, c: (b, c, 0)),
        out_shape=jax.ShapeDtypeStruct((B, S, SSD_INNER), BF16),
        scratch_shapes=[pltpu.VMEM((nb, HALO, W_XS + W_BC), BF16),
                        pltpu.VMEM((nb, SSD_STATE, SSD_INNER), F32)],
        compiler_params=_params("parallel", "arbitrary"),
        name="ssd_mixer",
    )(proj3, proj3, proj3, proj3, *consts)
    return out.reshape(B * S, SSD_INNER)


def _l2norm(x):
    return x * lax.rsqrt(jnp.sum(x * x, axis=-1, keepdims=True) + EPS)


def _split(x):
    hi = x.astype(BF16)
    return hi, (x - hi.astype(F32)).astype(BF16)


def _dot3(a, b):
    (ah, al), (bh, bl) = a, b
    return _dot(jnp.concatenate([ah, al, ah], axis=1), jnp.concatenate([bh, bh, bl], axis=0))


def _dot2(a, b):
    return _dot(jnp.concatenate(a, axis=1), jnp.concatenate([b, b], axis=0))


def _dot2r(a, b):
    return _dot(jnp.concatenate([a, a], axis=1), jnp.concatenate(b, axis=0))


SOLVE_BLOCK_LOG2 = 3


def _block_masks(row, col):
    n_levels = int(math.log2(CHUNK)) - SOLVE_BLOCK_LOG2
    diag = (row >> SOLVE_BLOCK_LOG2) == (col >> SOLVE_BLOCK_LOG2)
    merges = []
    for lv in range(n_levels):
        s = SOLVE_BLOCK_LOG2 + lv
        same = (row >> (s + 1)) == (col >> (s + 1))
        lower_left = jnp.where(same, ((row >> s) & 1) - ((col >> s) & 1), 0) == 1
        merges.append(lower_left)
    return diag, merges


def _unit_lower_inverses(mats, eye, diag_mask, merge_masks):
    def as_factor(mask):
        return jnp.where(mask, 1.0, 0.0).astype(BF16)

    eye_b = eye.astype(BF16)
    diag_b = as_factor(diag_mask)
    d1 = [a * diag_b for a in mats]
    d2 = [_split(_dot(d, d)) for d in d1]
    t0 = [eye_b - d for d in d1]
    ts = [t.astype(F32) + _dot2r(t, x) for t, x in zip(t0, d2)]
    d4 = [_split(_dot3(x, x)) for x in d2]
    ts = [t + _dot3(_split(t), x) for t, x in zip(ts, d4)]
    for lv, m in enumerate(merge_masks):
        n = 1 << (SOLVE_BLOCK_LOG2 + lv)
        pairs = range(CHUNK // (2 * n))

        def take(x):
            return jnp.concatenate([x[(2 * b + 1) * n:(2 * b + 2) * n] for b in pairs], axis=0)

        def put(x, part):
            pieces = []
            for b in pairs:
                pieces += [x[2 * b * n:(2 * b + 1) * n], part[b * n:(b + 1) * n]]
            return jnp.concatenate(pieces, axis=0)

        tsp = [_split(t) for t in ts]
        mb = as_factor(m)
        low = [take(t) for t in ts]
        mid = [_split(_dot2(_split(x), a * mb)) for x, a in zip(low, mats)]
        ts = [put(t, x - _dot3(md, tp)) for t, x, md, tp in zip(ts, low, mid, tsp)]
    return ts


def _gdn_kernel(qkv_ref, gz_ref, sm_ref, cw_ref, shift_ref, dtb_ref, alog_ref, ng_ref, o_ref, tail_ref, s_ref):
    HD = GDN_HEAD
    NH = GDN_V_HEADS
    NB = qkv_ref.shape[0]
    rows_ = range(NB)
    items = [(r, h) for r in rows_ for h in range(NH)]

    @pl.when(pl.program_id(1) == 0)
    def _():
        tail_ref[...] = jnp.zeros(tail_ref.shape, tail_ref.dtype)
        s_ref[...] = jnp.zeros(s_ref.shape, F32)

    qkv = [_silu(_causal_conv(tail_ref.at[r], qkv_ref[r], cw_ref, shift_ref)) for r in rows_]

    row = lax.broadcasted_iota(jnp.int32, (CHUNK, CHUNK), 0)
    col = lax.broadcasted_iota(jnp.int32, (CHUNK, CHUNK), 1)
    incl = row >= col
    strict = row > col
    eye = (row == col).astype(F32)
    diag_mask, merge_masks = _block_masks(row, col)
    upper = (row <= col).astype(F32)
    pad = jnp.zeros((CHUNK - 2 * NH, CHUNK), F32)

    sm_t = [sm_ref[r].T for r in rows_]
    beta_t = [_sigmoid(x[SM_B:SM_B + NH, :]) for x in sm_t]
    g_t = [-jnp.exp(alog_ref[...]) * _softplus(x[SM_A:SM_A + NH, :] + dtb_ref[...]) for x in sm_t]
    gc_t = [jnp.dot(x, upper, precision=HIGHEST, preferred_element_type=F32) for x in g_t]
    cols = [jnp.concatenate([beta_t[r], gc_t[r], pad], axis=0).T for r in rows_]

    q = [[_l2norm(qkv[r][:, j * HD:(j + 1) * HD]) * (HD ** -0.5) for j in range(GDN_QK_HEADS)] for r in rows_]
    k = [[_l2norm(qkv[r][:, GDN_QK + j * HD:GDN_QK + (j + 1) * HD]) for j in range(GDN_QK_HEADS)] for r in rows_]
    k_t = [[x.T.astype(BF16) for x in k[r]] for r in rows_]
    qk_raw = [[_dot(a.astype(BF16), b) for a, b in zip(q[r], k_t[r])] for r in rows_]

    b_col = [cols[r][:, h:h + 1] for r, h in items]
    g_col = [cols[r][:, NH + h:NH + h + 1] for r, h in items]
    g_last = [gc_t[r][h:h + 1, CHUNK - 1:CHUNK] for r, h in items]
    decay = [jnp.exp(jnp.where(incl, g_col[i] - gc_t[r][h:h + 1, :], NEG_BIG)) for i, (r, h) in enumerate(items)]
    kb = [k[r][h // 2] * b_col[i] for i, (r, h) in enumerate(items)]
    a_mat = [jnp.where(strict, _dot(kb[i].astype(BF16), k_t[r][h // 2]) * decay[i], 0.0).astype(BF16)
             for i, (r, h) in enumerate(items)]
    t_inv = _unit_lower_inverses(a_mat, eye, diag_mask, merge_masks)

    e_g = [jnp.exp(x) for x in g_col]
    v = [qkv[r][:, 2 * GDN_QK + h * HD:2 * GDN_QK + (h + 1) * HD] for r, h in items]
    rhs = [jnp.concatenate([v[i] * b_col[i], kb[i] * e_g[i]], axis=1).astype(BF16) for i in range(len(items))]
    sol = [_dot2(_split(t_inv[i]), rhs[i]) for i in range(len(items))]
    state = [s_ref[i] for i in range(len(items))]
    ws = [_dot(jnp.concatenate([sol[i][:, HD:2 * HD], q[r][h // 2] * e_g[i]], axis=0).astype(BF16),
               state[i].astype(BF16)) for i, (r, h) in enumerate(items)]
    v_new_b = [(sol[i][:, 0:HD] - ws[i][0:CHUNK]).astype(BF16) for i in range(len(items))]
    o = [ws[i][CHUNK:2 * CHUNK] + _dot((qk_raw[r][h // 2] * decay[i]).astype(BF16), v_new_b[i])
         for i, (r, h) in enumerate(items)]
    k_dec_t = [(k[r][h // 2] * jnp.exp(g_last[i] - g_col[i])).T.astype(BF16) for i, (r, h) in enumerate(items)]
    for i in range(len(items)):
        s_ref[i] = state[i] * jnp.exp(g_last[i]) + _dot(k_dec_t[i], v_new_b[i])
    for i, (r, h) in enumerate(items):
        out = _rms(o[i], ng_ref[...]) * _silu(gz_ref[r, :, h * HD:(h + 1) * HD])
        o_ref[r, :, h * HD:(h + 1) * HD] = out.astype(o_ref.dtype)


GDN_BATCH_PER_STEP = 2


def _gdn(proj, B, S, cw, shift, dtb, alog, ng):
    nc = S // CHUNK
    nb = GDN_BATCH_PER_STEP
    width = GDN_V_HEADS * GDN_HEAD
    proj3 = proj.reshape(B, S, proj.shape[1])
    full = lambda a: pl.BlockSpec(a.shape, lambda b, c: (0,) * a.ndim)
    rows = lambda w, idx: pl.BlockSpec((nb, CHUNK, w), lambda b, c: (b, c, idx))
    consts = [cw, shift, dtb, alog, ng]
    out = pl.pallas_call(
        _gdn_kernel,
        grid=(B // nb, nc),
        in_specs=[rows(W_GQKV, OFF_GQKV // W_GQKV), rows(W_GZ, OFF_GZ // W_GZ),
                  rows(W_SM, OFF_SM // W_SM)] + [full(a) for a in consts],
        out_specs=pl.BlockSpec((nb, CHUNK, width), lambda b, c: (b, c, 0)),
        out_shape=jax.ShapeDtypeStruct((B, S, width), BF16),
        scratch_shapes=[pltpu.VMEM((nb, HALO, GDN_QKV), BF16),
                        pltpu.VMEM((nb * GDN_V_HEADS, GDN_HEAD, GDN_HEAD), F32)],
        compiler_params=_params("parallel", "arbitrary"),
        name="gdn_mixer",
    )(proj3, proj3, proj3, *consts)
    return out.reshape(B * S, width)


ATTN_BLOCK = 512
LOG2E = math.log2(math.e)
BF16_SUBLANES = 16
V_ROWS = MLA_V + BF16_SUBLANES
HEADS_PER_STEP = 4


def _mla_prep_kernel(cq_ref, ckv_ref, kr_ref, cos_ref, sin_ref, qg_ref, kvg_ref, wq_ref, wkv_ref,
                     qt_ref, k_ref, vt_ref):
    H = MLA_HEADS
    tm = cq_ref.shape[0]
    scale = (MLA_NOPE + MLA_ROPE) ** -0.5 * LOG2E
    cos4 = cos_ref[...]
    sin4 = sin_ref[...]
    lane = lax.broadcasted_iota(jnp.int32, (tm, LANES), 1)
    left = lane < MLA_ROPE

    qm = _dot(_rms(cq_ref[...], qg_ref[...]).astype(BF16), wq_ref[...])
    kvm = _dot(_rms(ckv_ref[...], kvg_ref[...]).astype(BF16), wkv_ref[...])

    kr = kr_ref[...]
    k_rope = (kr * jnp.where(left, cos4, sin4)
              + pltpu.roll(kr, MLA_ROPE, axis=1) * jnp.where(left, sin4, cos4)).astype(BF16)

    pe_off = H * MLA_NOPE
    rot_off = pe_off + H * MLA_ROPE
    for h in range(H):
        jb = (h // 2) * LANES
        pe = qm[:, pe_off + jb:pe_off + jb + LANES]
        rot = qm[:, rot_off + jb:rot_off + jb + LANES]
        roped = pe * cos4 + rot * sin4
        mine = left if h % 2 == 0 else jnp.logical_not(left)
        q_nope = qm[:, h * MLA_NOPE:(h + 1) * MLA_NOPE]
        qt_ref[h, 0, 0:LANES, :] = (q_nope * scale).T.astype(BF16)
        qt_ref[h, 0, LANES:2 * LANES, :] = (jnp.where(mine, roped, 0.0) * scale).T.astype(BF16)
        base = h * (MLA_NOPE + MLA_V)
        k_ref[h] = jnp.concatenate([kvm[:, base:base + MLA_NOPE].astype(BF16), k_rope], axis=1)
        vt_ref[h, 0, 0:MLA_V, :] = kvm[:, base + MLA_NOPE:base + MLA_NOPE + MLA_V].T.astype(BF16)
        ones_row = lax.broadcasted_iota(jnp.int32, (BF16_SUBLANES, tm), 0) == 0
        vt_ref[h, 0, MLA_V:V_ROWS, :] = jnp.where(ones_row, 1.0, 0.0).astype(BF16)


def _mla_prep(proj, cos4, sin4, qg, kvg, wq, wkv):
    T = proj.shape[0]
    H = MLA_HEADS
    tm = ATTN_BLOCK
    nt = T // tm
    full = lambda a: pl.BlockSpec(a.shape, lambda i: (0,) * a.ndim)
    rows = lambda w, idx: pl.BlockSpec((tm, w), lambda i: (i, idx))
    consts = [qg, kvg, wq, wkv]
    dq = 2 * LANES
    return pl.pallas_call(
        _mla_prep_kernel,
        grid=(nt,),
        in_specs=[rows(W_CQ, OFF_CQ // W_CQ), rows(W_CKV, OFF_CKV // W_CKV), rows(W_KR, OFF_KR // W_KR),
                  rows(LANES, 0), rows(LANES, 0)] + [full(a) for a in consts],
        out_specs=[pl.BlockSpec((H, 1, dq, tm), lambda i: (0, i, 0, 0)),
                   pl.BlockSpec((H, tm, dq), lambda i: (0, i, 0)),
                   pl.BlockSpec((H, 1, V_ROWS, tm), lambda i: (0, i, 0, 0))],
        out_shape=[jax.ShapeDtypeStruct((H, nt, dq, tm), BF16),
                   jax.ShapeDtypeStruct((H, T, dq), BF16),
                   jax.ShapeDtypeStruct((H, nt, V_ROWS, tm), BF16)],
        compiler_params=_params("parallel"),
        name="mla_prep",
    )(proj, proj, proj, cos4, sin4, *consts)


def _attn_kernel(qt_ref, k_ref, vt_ref, o_ref, m_ref, acc_ref):
    tb = ATTN_BLOCK
    hs = range(HEADS_PER_STEP)
    qi = pl.program_id(2)
    m_ref[...] = jnp.full(m_ref.shape, NEG_BIG, F32)
    acc_ref[...] = jnp.zeros(acc_ref.shape, F32)

    def update(kis, last_is_diagonal):
        nb = range(len(kis))
        s = [[_dot(k_ref[h, pl.ds(pl.multiple_of(ki * tb, tb), tb), :], qt_ref[h, 0]) for ki in kis]
             for h in hs]
        if last_is_diagonal:
            key = lax.broadcasted_iota(jnp.int32, (tb, tb), 0)
            qry = lax.broadcasted_iota(jnp.int32, (tb, tb), 1)
            for h in hs:
                s[h][-1] = jnp.where(key <= qry, s[h][-1], NEG_BIG)
        m_old = [m_ref[h] for h in hs]
        m_new = list(m_old)
        for h in hs:
            for j in nb:
                m_new[h] = jnp.maximum(m_new[h], jnp.max(s[h][j], axis=0, keepdims=True))
        p = [[jnp.exp2(s[h][j] - m_new[h]).astype(BF16) for j in nb] for h in hs]
        pv = [[_dot(vt_ref[h, kis[j]], p[h][j]) for j in nb] for h in hs]
        for h in hs:
            acc_ref[h] = jnp.exp2(m_old[h] - m_new[h]) * acc_ref[h] + sum(pv[h][1:], pv[h][0])
            m_ref[h] = m_new[h]

    def pair(i, carry):
        update([2 * i, 2 * i + 1], False)
        return carry

    lax.fori_loop(0, lax.shift_right_logical(qi, 1), pair, 0)
    odd = (qi & 1) == 1

    @pl.when(odd)
    def _():
        update([qi - 1, qi], True)

    @pl.when(jnp.logical_not(odd))
    def _():
        update([qi], True)

    for h in hs:
        acc = acc_ref[h]
        o_ref[:, h * MLA_V:(h + 1) * MLA_V] = (acc[0:MLA_V] / acc[MLA_V:MLA_V + 1]).T.astype(o_ref.dtype)


def _attention(q_t, k, v_t, B, S):
    H, T, dq = k.shape
    tb = ATTN_BLOCK
    nq = S // tb
    hp = HEADS_PER_STEP
    return pl.pallas_call(
        _attn_kernel,
        grid=(B, H // hp, nq),
        in_specs=[pl.BlockSpec((hp, 1, dq, tb), lambda b, h, i: (h, b * nq + i, 0, 0)),
                  pl.BlockSpec((hp, S, dq), lambda b, h, i: (h, b, 0)),
                  pl.BlockSpec((hp, nq, V_ROWS, tb), lambda b, h, i: (h, b, 0, 0))],
        out_specs=pl.BlockSpec((tb, hp * MLA_V), lambda b, h, i: (b * nq + i, h)),
        out_shape=jax.ShapeDtypeStruct((T, H * MLA_V), BF16),
        scratch_shapes=[pltpu.VMEM((hp, 1, tb), F32), pltpu.VMEM((hp, V_ROWS, tb), F32)],
        compiler_params=_params("parallel", "parallel", "arbitrary"),
        name="mla_attention",
    )(q_t, k, v_t)


def _merge_kernel(x_ref, ys_ref, ym_ref, yg_ref, g_ref, wgate_ref, ws_ref, wm_ref, wg_ref, wo_ref, h_ref):
    D = D_MODEL
    x = x_ref[...]
    xn = _rms(x, g_ref[...]).astype(BF16)
    mixed = None
    for idx, (y_ref, w_ref) in enumerate(((ys_ref, ws_ref), (ym_ref, wm_ref), (yg_ref, wg_ref))):
        gate = _sigmoid(_dot(xn, wgate_ref[:, idx * D:(idx + 1) * D]))
        term = gate * _dot(y_ref[...], w_ref[...])
        mixed = term if mixed is None else mixed + term
    h_ref[...] = x + _dot(mixed.astype(BF16), wo_ref[...])


def _merge(x2, ys, ym, yg, g, wgate, ws, wm, wg, wo):
    T, D = x2.shape
    tm = min(T, 512)
    full = lambda a: pl.BlockSpec(a.shape, lambda i: (0,) * a.ndim)
    rows = pl.BlockSpec((tm, D), lambda i: (i, 0))
    consts = [g, wgate, ws, wm, wg, wo]
    return pl.pallas_call(
        _merge_kernel,
        grid=(T // tm,),
        in_specs=[rows] * 4 + [full(a) for a in consts],
        out_specs=rows,
        out_shape=jax.ShapeDtypeStruct((T, D), F32),
        compiler_params=_params("parallel"),
        name="merge",
    )(x2, ys, ym, yg, *consts)


def _ffn_kernel(h_ref, g_ref, wu_ref, wd_ref, fg_ref, o_ref, hn_ref, acc_ref, *, final_norm):
    j = pl.program_id(1)

    @pl.when(j == 0)
    def _():
        hn_ref[...] = _rms(h_ref[...], g_ref[...]).astype(BF16)
        acc_ref[...] = jnp.zeros(acc_ref.shape, F32)

    up = jnp.maximum(_dot(hn_ref[...], wu_ref[...]), 0.0)
    acc_ref[...] += _dot((up * up).astype(BF16), wd_ref[...])

    @pl.when(j == pl.num_programs(1) - 1)
    def _():
        out = h_ref[...] + acc_ref[...]
        if final_norm:
            out = _rms(out, fg_ref[...])
        o_ref[...] = out


def _ffn(h, g, wu, wd, fg, final_norm):
    T, D = h.shape
    F = wu.shape[1]
    tm = min(T, 1024)
    tf = min(F, 2048)
    return pl.pallas_call(
        functools.partial(_ffn_kernel, final_norm=final_norm),
        grid=(T // tm, F // tf),
        in_specs=[pl.BlockSpec((tm, D), lambda i, j: (i, 0)),
                  pl.BlockSpec((1, D), lambda i, j: (0, 0)),
                  pl.BlockSpec((D, tf), lambda i, j: (0, j)),
                  pl.BlockSpec((tf, D), lambda i, j: (j, 0)),
                  pl.BlockSpec((1, D), lambda i, j: (0, 0))],
        out_specs=pl.BlockSpec((tm, D), lambda i, j: (i, 0)),
        out_shape=jax.ShapeDtypeStruct((T, D), F32),
        scratch_shapes=[pltpu.VMEM((tm, D), BF16), pltpu.VMEM((tm, D), F32)],
        compiler_params=_params("parallel", "arbitrary"),
        name="ffn",
    )(h, g, wu, wd, fg)


def _rotate_half_cols(w):
    half = w.shape[-1] // 2
    return jnp.concatenate([-w[..., half:], w[..., :half]], axis=-1)


def _split_in_proj(w_in):
    sizes = (SSD_INNER, SSD_INNER, 2 * SSD_GROUPS * SSD_STATE, SSD_HEADS, MLA_Q_LORA, MLA_KV_LORA, MLA_ROPE,
             GDN_QKV, GDN_V_HEADS * GDN_HEAD, GDN_V_HEADS, GDN_V_HEADS, 3 * D_MODEL)
    offs = np.cumsum((0,) + sizes)
    return [w_in[:, int(offs[i]):int(offs[i + 1])] for i in range(len(sizes))]


def _layer_weights(w_in, w_uq):
    z, xs, bc, dt, cq, ckv, kr, gqkv, gz, gb, ga, gates = _split_in_proj(w_in)
    small = jnp.concatenate([dt, gb, ga, jnp.zeros((D_MODEL, W_SM - 32), w_in.dtype)], axis=1)
    w_a = jnp.concatenate([gqkv, z, xs, gz, bc, cq, ckv, kr, _rotate_half_cols(kr), small], axis=1)
    H = MLA_HEADS
    wq = w_uq.reshape(MLA_Q_LORA, H, MLA_NOPE + MLA_ROPE)
    nope = wq[:, :, :MLA_NOPE].reshape(MLA_Q_LORA, H * MLA_NOPE)
    pe = wq[:, :, MLA_NOPE:]
    w_q = jnp.concatenate([nope, pe.reshape(MLA_Q_LORA, H * MLA_ROPE),
                           _rotate_half_cols(pe).reshape(MLA_Q_LORA, H * MLA_ROPE)], axis=1)
    return w_a, gates, w_q


def _pad_lanes(v, n=LANES):
    return jnp.concatenate([v, jnp.zeros((n - v.shape[0],), v.dtype)])[None, :]


def kernel(x, positions, norm1_g, w_in, ssd_conv_w, ssd_conv_b, ssd_dt_bias, ssd_a_log, ssd_d, ssd_norm_g,
           mla_q_norm_g, mla_w_uq, mla_kv_norm_g, mla_w_ukv, gdn_conv_w, gdn_dt_bias, gdn_a_log, gdn_norm_g,
           w_ssd_out, w_mla_out, w_gdn_out, w_out, norm2_g, w_up, w_down, final_norm_g):
    B, S, D = x.shape
    T = B * S
    x2 = x.reshape(T, D)

    inv = ROPE_THETA ** (-jnp.arange(0, MLA_ROPE, 2, dtype=F32) / MLA_ROPE)
    inv4 = jnp.tile(inv, LANES // inv.shape[0])[None, :]
    cos4, sin4 = _rope_tables(positions.reshape(T, 1), inv4)

    shift = _shift_matrix()
    expand = (jnp.arange(LANES)[:, None] == (jnp.arange(SSD_INNER)[None, :] // SSD_HEAD_DIM)).astype(BF16)

    for l in range(DEPTH):
        w_a, w_gate, w_q = _layer_weights(w_in[l].astype(BF16), mla_w_uq[l].astype(BF16))
        proj = _inproj(x2, norm1_g[l][None, :], w_a)

        y_ssd = _ssd(proj, B, S, ssd_conv_w[l], ssd_conv_b[l][None, :], shift,
                     _pad_lanes(ssd_dt_bias[l]), _pad_lanes(ssd_a_log[l]),
                     jnp.repeat(ssd_d[l], SSD_HEAD_DIM)[None, :], ssd_norm_g[l][None, :], expand)

        q, k, v = _mla_prep(proj, cos4, sin4, mla_q_norm_g[l][None, :], mla_kv_norm_g[l][None, :],
                            w_q, mla_w_ukv[l].astype(BF16))
        y_mla = _attention(q, k, v, B, S)

        y_gdn = _gdn(proj, B, S, gdn_conv_w[l], shift,
                     jnp.broadcast_to(gdn_dt_bias[l][:, None], (GDN_V_HEADS, CHUNK)),
                     jnp.broadcast_to(gdn_a_log[l][:, None], (GDN_V_HEADS, CHUNK)),
                     gdn_norm_g[l][None, :])

        h = _merge(x2, y_ssd, y_mla, y_gdn, norm1_g[l][None, :], w_gate,
                   w_ssd_out[l].astype(BF16), w_mla_out[l].astype(BF16), w_gdn_out[l].astype(BF16),
                   w_out[l].astype(BF16))
        x2 = _ffn(h, norm2_g[l][None, :], w_up[l].astype(BF16), w_down[l].astype(BF16),
                  final_norm_g[None, :], final_norm=(l == DEPTH - 1))
    return x2.reshape(B, S, D)
```

```python
import functools
import math

import jax
import jax.numpy as jnp
import numpy as np
from jax import lax
from jax.experimental import pallas as pl
from jax.experimental.pallas import tpu as pltpu

F32 = jnp.float32
BF16 = jnp.bfloat16
HIGHEST = lax.Precision.HIGHEST

EPS = 1e-6
D_MODEL = 1024
DEPTH = 2
SSD_HEADS = 16
SSD_HEAD_DIM = 64
SSD_GROUPS = 2
SSD_STATE = 128
SSD_INNER = 1024
MLA_HEADS = 8
MLA_NOPE = 128
MLA_ROPE = 64
MLA_V = 128
MLA_Q_LORA = 512
MLA_KV_LORA = 256
ROPE_THETA = 10000.0
GDN_HEAD = 128
GDN_V_HEADS = 8
GDN_QK_HEADS = 4
GDN_QK = GDN_QK_HEADS * GDN_HEAD
GDN_QKV = 2 * GDN_QK + GDN_V_HEADS * GDN_HEAD
D_FF = 4096
CONV_K = 4

CHUNK = 128
HALO = 16
LANES = 128
VMEM_LIMIT = 56 * 1024 * 1024

W_GQKV, W_Z, W_XS, W_GZ, W_BC, W_CQ, W_CKV, W_KR, W_SM = 2048, 1024, 1024, 1024, 512, 512, 256, 128, 128
OFF_GQKV, OFF_Z, OFF_XS, OFF_GZ, OFF_BC, OFF_CQ, OFF_CKV, OFF_KR, OFF_SM = (
    0, 2048, 3072, 4096, 5120, 5632, 6144, 6400, 6528)
N_PROJ = 6656
SM_DT, SM_B, SM_A = 0, 16, 24

NEG_BIG = -1e30


def _rms(x, g):
    return x * lax.rsqrt(jnp.mean(x * x, axis=-1, keepdims=True) + EPS) * g


def _sigmoid(x):
    return 1.0 / (1.0 + jnp.exp(-x))


def _silu(x):
    h = 0.5 * x
    return h + h * jnp.tanh(h)


def _softplus(x):
    return jnp.maximum(x, 0.0) + jnp.log1p(jnp.exp(-jnp.abs(x)))


def _dot(a, b):
    return jnp.dot(a, b, preferred_element_type=F32)


def _dot_nt(a, b):
    return lax.dot_general(a, b, (((1,), (1,)), ((), ())), preferred_element_type=F32)


def _params(*sem):
    return pltpu.CompilerParams(dimension_semantics=sem, vmem_limit_bytes=VMEM_LIMIT)


def _rope_kernel(pos_ref, inv_ref, cos_ref, sin_ref):
    ang = pos_ref[...].astype(F32) * inv_ref[...]
    cos_ref[...] = jnp.cos(ang)
    sin_ref[...] = jnp.sin(ang)


def _rope_tables(pos, inv4):
    R = pos.shape[0]
    tm = min(R, 2048)
    return pl.pallas_call(
        _rope_kernel,
        grid=(R // tm,),
        in_specs=[pl.BlockSpec((tm, 1), lambda i: (i, 0)),
                  pl.BlockSpec((1, LANES), lambda i: (0, 0))],
        out_specs=[pl.BlockSpec((tm, LANES), lambda i: (i, 0))] * 2,
        out_shape=[jax.ShapeDtypeStruct((R, LANES), F32)] * 2,
        compiler_params=_params("parallel"),
        name="rope_tables",
    )(pos, inv4)


def _inproj_kernel(x_ref, g_ref, w_ref, o_ref, xn_ref):
    @pl.when(pl.program_id(1) == 0)
    def _():
        xn_ref[...] = _rms(x_ref[...], g_ref[...]).astype(BF16)

    o_ref[...] = _dot(xn_ref[...], w_ref[...])


def _inproj(x2, g, w):
    T, D = x2.shape
    N = w.shape[1]
    tm = min(T, 1024)
    tn = N // 4
    return pl.pallas_call(
        _inproj_kernel,
        grid=(T // tm, N // tn),
        in_specs=[pl.BlockSpec((tm, D), lambda i, j: (i, 0)),
                  pl.BlockSpec((1, D), lambda i, j: (0, 0)),
                  pl.BlockSpec((D, tn), lambda i, j: (0, j))],
        out_specs=pl.BlockSpec((tm, tn), lambda i, j: (i, j)),
        out_shape=jax.ShapeDtypeStruct((T, N), F32),
        scratch_shapes=[pltpu.VMEM((tm, D), BF16)],
        compiler_params=_params("parallel", "arbitrary"),
        name="in_proj",
    )(x2, g, w)


def _shift_matrix():
    r = np.arange((CONV_K - 1) * CHUNK)[:, None]
    c = np.arange(HALO + CHUNK)[None, :]
    return jnp.asarray(c == HALO + (r % CHUNK) - (r // CHUNK + 1), BF16)


def _causal_conv(tail_ref, cur, w_ref, shift_ref):
    cur_b = cur.astype(BF16)
    delayed = _dot(shift_ref[...], jnp.concatenate([tail_ref[...], cur_b], axis=0))
    tail_ref[...] = cur_b[CHUNK - HALO:CHUNK, :]
    acc = w_ref[CONV_K - 1:CONV_K, :] * cur
    for j in range(1, CONV_K):
        acc = acc + w_ref[CONV_K - 1 - j:CONV_K - j, :] * delayed[(j - 1) * CHUNK:j * CHUNK]
    return acc


def _ssd_kernel(z_ref, xs_ref, bc_ref, sm_ref, cw_ref, cb_ref, shift_ref, dtb_ref,
                alog_ref, dsk_ref, ng_ref, e_ref, o_ref, tail_ref, h_ref):
    G, N, P = SSD_GROUPS, SSD_STATE, SSD_HEAD_DIM
    HG = SSD_HEADS // G
    GW = HG * P
    NP = SSD_HEADS // 2
    rows_ = range(z_ref.shape[0])
    gsl = [slice(g * GW, (g + 1) * GW) for g in range(G)]
    row_groups = [(r, g) for r in rows_ for g in range(G)]
    row_heads = [(r, e) for r in rows_ for e in range(SSD_HEADS)]
    row_pairs = [(r, p) for r in rows_ for p in range(NP)]

    @pl.when(pl.program_id(1) == 0)
    def _():
        tail_ref[...] = jnp.zeros(tail_ref.shape, tail_ref.dtype)
        h_ref[...] = jnp.zeros(h_ref.shape, F32)

    raw = [jnp.concatenate([xs_ref[r], bc_ref[r]], axis=1) for r in rows_]
    xbc = [_silu(_causal_conv(tail_ref.at[r], raw[r], cw_ref, shift_ref) + cb_ref[...]) for r in rows_]
    xs = [x[:, 0:SSD_INNER] for x in xbc]
    bc = [x[:, SSD_INNER:] for x in xbc]

    row = lax.broadcasted_iota(jnp.int32, (CHUNK, CHUNK), 0)
    col = lax.broadcasted_iota(jnp.int32, (CHUNK, CHUNK), 1)
    tril = row >= col
    tril_f = tril.astype(F32)
    lane = lax.broadcasted_iota(jnp.int32, (CHUNK, LANES), 1)
    left = lane < P

    neg_a = -jnp.exp(alog_ref[...])
    dt = [_softplus(sm_ref[r] + dtb_ref[...]) for r in rows_]
    a_cum = [jnp.dot(tril_f, d * neg_a, precision=HIGHEST, preferred_element_type=F32) for d in dt]
    a_cum_t = [a.T for a in a_cum]
    dt_t = [d.T for d in dt]
    stack = [jnp.concatenate([jnp.exp(a_cum[r]), dt[r] * jnp.exp(a_cum[r][CHUNK - 1:CHUNK, :] - a_cum[r])], axis=0)
             for r in rows_]
    hi = [x.astype(BF16) for x in stack]
    lo = [(x - h.astype(F32)).astype(BF16) for x, h in zip(stack, hi)]
    ex = [_dot(h, e_ref[...]) + _dot(l, e_ref[...]) for h, l in zip(hi, lo)]
    e_a_x = [x[0:CHUNK] for x in ex]
    sdec_x = [x[CHUNK:2 * CHUNK] for x in ex]

    b_g = [bc[r][:, g * N:(g + 1) * N] for r, g in row_groups]
    c_b = [bc[r][:, (G + g) * N:(G + g + 1) * N].astype(BF16) for r, g in row_groups]
    cb = [_dot_nt(c, b.astype(BF16)) for c, b in zip(c_b, b_g)]
    x_sd = [(xs[r][:, gsl[g]] * sdec_x[r][:, gsl[g]]).astype(BF16) for r, g in row_groups]
    st = [_dot(b.T.astype(BF16), x) for b, x in zip(b_g, x_sd)]
    h_prev = [h_ref[r, :, gsl[g]] for r, g in row_groups]
    y_off = [_dot(c_b[i], h_prev[i].astype(BF16)) * e_a_x[r][:, gsl[g]] for i, (r, g) in enumerate(row_groups)]
    for i, (r, g) in enumerate(row_groups):
        h_ref[r, :, gsl[g]] = h_prev[i] * e_a_x[r][CHUNK - 1:CHUNK, gsl[g]] + st[i]
    seg = [jnp.where(tril, a_cum[r][:, e:e + 1] - a_cum_t[r][e:e + 1, :], NEG_BIG) for r, e in row_heads]
    m = [cb[r * G + e // HG] * jnp.exp(seg[i]) * dt_t[r][e:e + 1, :] for i, (r, e) in enumerate(row_heads)]
    m2 = [jnp.concatenate([m[r * SSD_HEADS + 2 * p], m[r * SSD_HEADS + 2 * p + 1]], axis=1).astype(BF16)
          for r, p in row_pairs]
    xp = [xs[r][:, 2 * p * P:(2 * p + 2) * P] for r, p in row_pairs]
    bd = [jnp.concatenate([jnp.where(left, x, 0.0), jnp.where(left, 0.0, x)], axis=0).astype(BF16)
          for x in xp]
    diag = [_dot(a, b) for a, b in zip(m2, bd)]
    for r in rows_:
        y = (jnp.concatenate(diag[r * NP:(r + 1) * NP], axis=1) + jnp.concatenate(y_off[r * G:(r + 1) * G], axis=1)
             + dsk_ref[...] * xs[r])
        y = y * _silu(z_ref[r])
        o_ref[r] = jnp.concatenate([_rms(y[:, gs], ng_ref[:, gs]) for gs in gsl], axis=1).astype(o_ref.dtype)


SSD_BATCH_PER_STEP = 4


def _ssd(proj, B, S, cw, cb, shift, dtb, alog, dsk, ng, expand):
    nc = S // CHUNK
    nb = SSD_BATCH_PER_STEP
    proj3 = proj.reshape(B, S, proj.shape[1])
    full = lambda a: pl.BlockSpec(a.shape, lambda b, c: (0,) * a.ndim)
    rows = lambda w, idx: pl.BlockSpec((nb, CHUNK, w), lambda b, c: (b, c, idx))
    consts = [cw, cb, shift, dtb, alog, dsk, ng, expand]
    out = pl.pallas_call(
        _ssd_kernel,
        grid=(B // nb, nc),
        in_specs=[rows(W_Z, OFF_Z // W_Z), rows(W_XS, OFF_XS // W_XS), rows(W_BC, OFF_BC // W_BC),
                  rows(W_SM, OFF_SM // W_SM)] + [full(a) for a in consts],
        out_specs=pl.BlockSpec((nb, CHUNK, SSD_INNER), lambda b, c: (b, c, 0)),
        out_shape=jax.ShapeDtypeStruct((B, S, SSD_INNER), BF16),
        scratch_shapes=[pltpu.VMEM((nb, HALO, W_XS + W_BC), BF16),
                        pltpu.VMEM((nb, SSD_STATE, SSD_INNER), F32)],
        compiler_params=_params("parallel", "arbitrary"),
        name="ssd_mixer",
    )(proj3, proj3, proj3, proj3, *consts)
    return out.reshape(B * S, SSD_INNER)


def _l2norm(x):
    return x * lax.rsqrt(jnp.sum(x * x, axis=-1, keepdims=True) + EPS)


def _split(x):
    hi = x.astype(BF16)
    return hi, (x - hi.astype(F32)).astype(BF16)


def _dot3(a, b):
    (ah, al), (bh, bl) = a, b
    return _dot(jnp.concatenate([ah, al, ah], axis=1), jnp.concatenate([bh, bh, bl], axis=0))


def _dot2(a, b):
    return _dot(jnp.concatenate(a, axis=1), jnp.concatenate([b, b], axis=0))


def _dot2r(a, b):
    return _dot(jnp.concatenate([a, a], axis=1), jnp.concatenate(b, axis=0))


SOLVE_BLOCK_LOG2 = 3


def _block_masks(row, col):
    n_levels = int(math.log2(CHUNK)) - SOLVE_BLOCK_LOG2
    diag = (row >> SOLVE_BLOCK_LOG2) == (col >> SOLVE_BLOCK_LOG2)
    merges = []
    for lv in range(n_levels):
        s = SOLVE_BLOCK_LOG2 + lv
        same = (row >> (s + 1)) == (col >> (s + 1))
        lower_left = jnp.where(same, ((row >> s) & 1) - ((col >> s) & 1), 0) == 1
        merges.append(lower_left)
    return diag, merges


def _unit_lower_inverses(mats, eye, diag_mask, merge_masks):
    def as_factor(mask):
        return jnp.where(mask, 1.0, 0.0).astype(BF16)

    eye_b = eye.astype(BF16)
    diag_b = as_factor(diag_mask)
    d1 = [a * diag_b for a in mats]
    d2 = [_split(_dot(d, d)) for d in d1]
    t0 = [eye_b - d for d in d1]
    ts = [t.astype(F32) + _dot2r(t, x) for t, x in zip(t0, d2)]
    d4 = [_split(_dot3(x, x)) for x in d2]
    ts = [t + _dot3(_split(t), x) for t, x in zip(ts, d4)]
    for lv, m in enumerate(merge_masks):
        n = 1 << (SOLVE_BLOCK_LOG2 + lv)
        pairs = range(CHUNK // (2 * n))

        def take(x):
            return jnp.concatenate([x[(2 * b + 1) * n:(2 * b + 2) * n] for b in pairs], axis=0)

        def put(x, part):
            pieces = []
            for b in pairs:
                pieces += [x[2 * b * n:(2 * b + 1) * n], part[b * n:(b + 1) * n]]
            return jnp.concatenate(pieces, axis=0)

        tsp = [_split(t) for t in ts]
        mb = as_factor(m)
        low = [take(t) for t in ts]
        mid = [_split(_dot2(_split(x), a * mb)) for x, a in zip(low, mats)]
        ts = [put(t, x - _dot3(md, tp)) for t, x, md, tp in zip(ts, low, mid, tsp)]
    return ts


def _gdn_kernel(qkv_ref, gz_ref, sm_ref, cw_ref, shift_ref, dtb_ref, alog_ref, ng_ref, o_ref, tail_ref, s_ref):
    HD = GDN_HEAD
    NH = GDN_V_HEADS
    NB = qkv_ref.shape[0]
    rows_ = range(NB)
    items = [(r, h) for r in rows_ for h in range(NH)]

    @pl.when(pl.program_id(1) == 0)
    def _():
        tail_ref[...] = jnp.zeros(tail_ref.shape, tail_ref.dtype)
        s_ref[...] = jnp.zeros(s_ref.shape, F32)

    qkv = [_silu(_causal_conv(tail_ref.at[r], qkv_ref[r], cw_ref, shift_ref)) for r in rows_]

    row = lax.broadcasted_iota(jnp.int32, (CHUNK, CHUNK), 0)
    col = lax.broadcasted_iota(jnp.int32, (CHUNK, CHUNK), 1)
    incl = row >= col
    strict = row > col
    eye = (row == col).astype(F32)
    diag_mask, merge_masks = _block_masks(row, col)
    upper = (row <= col).astype(F32)
    pad = jnp.zeros((CHUNK - 2 * NH, CHUNK), F32)

    sm_t = [sm_ref[r].T for r in rows_]
    beta_t = [_sigmoid(x[SM_B:SM_B + NH, :]) for x in sm_t]
    g_t = [-jnp.exp(alog_ref[...]) * _softplus(x[SM_A:SM_A + NH, :] + dtb_ref[...]) for x in sm_t]
    gc_t = [jnp.dot(x, upper, precision=HIGHEST, preferred_element_type=F32) for x in g_t]
    cols = [jnp.concatenate([beta_t[r], gc_t[r], pad], axis=0).T for r in rows_]

    q = [[_l2norm(qkv[r][:, j * HD:(j + 1) * HD]) * (HD ** -0.5) for j in range(GDN_QK_HEADS)] for r in rows_]
    k = [[_l2norm(qkv[r][:, GDN_QK + j * HD:GDN_QK + (j + 1) * HD]) for j in range(GDN_QK_HEADS)] for r in rows_]
    k_t = [[x.T.astype(BF16) for x in k[r]] for r in rows_]
    qk_raw = [[_dot(a.astype(BF16), b) for a, b in zip(q[r], k_t[r])] for r in rows_]

    b_col = [cols[r][:, h:h + 1] for r, h in items]
    g_col = [cols[r][:, NH + h:NH + h + 1] for r, h in items]
    g_last = [gc_t[r][h:h + 1, CHUNK - 1:CHUNK] for r, h in items]
    decay = [jnp.exp(jnp.where(incl, g_col[i] - gc_t[r][h:h + 1, :], NEG_BIG)) for i, (r, h) in enumerate(items)]
    kb = [k[r][h // 2] * b_col[i] for i, (r, h) in enumerate(items)]
    a_mat = [jnp.where(strict, _dot(kb[i].astype(BF16), k_t[r][h // 2]) * decay[i], 0.0).astype(BF16)
             for i, (r, h) in enumerate(items)]
    t_inv = _unit_lower_inverses(a_mat, eye, diag_mask, merge_masks)

    e_g = [jnp.exp(x) for x in g_col]
    v = [qkv[r][:, 2 * GDN_QK + h * HD:2 * GDN_QK + (h + 1) * HD] for r, h in items]
    rhs = [jnp.concatenate([v[i] * b_col[i], kb[i] * e_g[i]], axis=1).astype(BF16) for i in range(len(items))]
    sol = [_dot2(_split(t_inv[i]), rhs[i]) for i in range(len(items))]
    state = [s_ref[i] for i in range(len(items))]
    ws = [_dot(jnp.concatenate([sol[i][:, HD:2 * HD], q[r][h // 2] * e_g[i]], axis=0).astype(BF16),
               state[i].astype(BF16)) for i, (r, h) in enumerate(items)]
    v_new_b = [(sol[i][:, 0:HD] - ws[i][0:CHUNK]).astype(BF16) for i in range(len(items))]
    o = [ws[i][CHUNK:2 * CHUNK] + _dot((qk_raw[r][h // 2] * decay[i]).astype(BF16), v_new_b[i])
         for i, (r, h) in enumerate(items)]
    k_dec_t = [(k[r][h // 2] * jnp.exp(g_last[i] - g_col[i])).T.astype(BF16) for i, (r, h) in enumerate(items)]
    for i in range(len(items)):
        s_ref[i] = state[i] * jnp.exp(g_last[i]) + _dot(k_dec_t[i], v_new_b[i])
    for i, (r, h) in enumerate(items):
        out = _rms(o[i], ng_ref[...]) * _silu(gz_ref[r, :, h * HD:(h + 1) * HD])
        o_ref[r, :, h * HD:(h + 1) * HD] = out.astype(o_ref.dtype)


GDN_BATCH_PER_STEP = 4


def _gdn(proj, B, S, cw, shift, dtb, alog, ng):
    nc = S // CHUNK
    nb = GDN_BATCH_PER_STEP
    width = GDN_V_HEADS * GDN_HEAD
    proj3 = proj.reshape(B, S, proj.shape[1])
    full = lambda a: pl.BlockSpec(a.shape, lambda b, c: (0,) * a.ndim)
    rows = lambda w, idx: pl.BlockSpec((nb, CHUNK, w), lambda b, c: (b, c, idx))
    consts = [cw, shift, dtb, alog, ng]
    out = pl.pallas_call(
        _gdn_kernel,
        grid=(B // nb, nc),
        in_specs=[rows(W_GQKV, OFF_GQKV // W_GQKV), rows(W_GZ, OFF_GZ // W_GZ),
                  rows(W_SM, OFF_SM // W_SM)] + [full(a) for a in consts],
        out_specs=pl.BlockSpec((nb, CHUNK, width), lambda b, c: (b, c, 0)),
        out_shape=jax.ShapeDtypeStruct((B, S, width), BF16),
        scratch_shapes=[pltpu.VMEM((nb, HALO, GDN_QKV), BF16),
                        pltpu.VMEM((nb * GDN_V_HEADS, GDN_HEAD, GDN_HEAD), F32)],
        compiler_params=_params("parallel", "arbitrary"),
        name="gdn_mixer",
    )(proj3, proj3, proj3, *consts)
    return out.reshape(B * S, width)


ATTN_BLOCK = 512
LOG2E = math.log2(math.e)
BF16_SUBLANES = 16
V_ROWS = MLA_V + BF16_SUBLANES
HEADS_PER_STEP = 4


def _mla_prep_kernel(cq_ref, ckv_ref, kr_ref, cos_ref, sin_ref, qg_ref, kvg_ref, wq_ref, wkv_ref,
                     qt_ref, k_ref, vt_ref):
    H = MLA_HEADS
    tm = cq_ref.shape[0]
    scale = (MLA_NOPE + MLA_ROPE) ** -0.5 * LOG2E
    cos4 = cos_ref[...]
    sin4 = sin_ref[...]
    lane = lax.broadcasted_iota(jnp.int32, (tm, LANES), 1)
    left = lane < MLA_ROPE

    qm = _dot(_rms(cq_ref[...], qg_ref[...]).astype(BF16), wq_ref[...])
    kvm = _dot(_rms(ckv_ref[...], kvg_ref[...]).astype(BF16), wkv_ref[...])

    kr = kr_ref[...]
    k_rope = (kr * jnp.where(left, cos4, sin4)
              + pltpu.roll(kr, MLA_ROPE, axis=1) * jnp.where(left, sin4, cos4)).astype(BF16)

    pe_off = H * MLA_NOPE
    rot_off = pe_off + H * MLA_ROPE
    for h in range(H):
        jb = (h // 2) * LANES
        pe = qm[:, pe_off + jb:pe_off + jb + LANES]
        rot = qm[:, rot_off + jb:rot_off + jb + LANES]
        roped = pe * cos4 + rot * sin4
        mine = left if h % 2 == 0 else jnp.logical_not(left)
        q_nope = qm[:, h * MLA_NOPE:(h + 1) * MLA_NOPE]
        qt_ref[h, 0, 0:LANES, :] = (q_nope * scale).T.astype(BF16)
        qt_ref[h, 0, LANES:2 * LANES, :] = (jnp.where(mine, roped, 0.0) * scale).T.astype(BF16)
        base = h * (MLA_NOPE + MLA_V)
        k_ref[h] = jnp.concatenate([kvm[:, base:base + MLA_NOPE].astype(BF16), k_rope], axis=1)
        vt_ref[h, 0, 0:MLA_V, :] = kvm[:, base + MLA_NOPE:base + MLA_NOPE + MLA_V].T.astype(BF16)
        ones_row = lax.broadcasted_iota(jnp.int32, (BF16_SUBLANES, tm), 0) == 0
        vt_ref[h, 0, MLA_V:V_ROWS, :] = jnp.where(ones_row, 1.0, 0.0).astype(BF16)


def _mla_prep(proj, cos4, sin4, qg, kvg, wq, wkv):
    T = proj.shape[0]
    H = MLA_HEADS
    tm = ATTN_BLOCK
    nt = T // tm
    full = lambda a: pl.BlockSpec(a.shape, lambda i: (0,) * a.ndim)
    rows = lambda w, idx: pl.BlockSpec((tm, w), lambda i: (i, idx))
    consts = [qg, kvg, wq, wkv]
    dq = 2 * LANES
    return pl.pallas_call(
        _mla_prep_kernel,
        grid=(nt,),
        in_specs=[rows(W_CQ, OFF_CQ // W_CQ), rows(W_CKV, OFF_CKV // W_CKV), rows(W_KR, OFF_KR // W_KR),
                  rows(LANES, 0), rows(LANES, 0)] + [full(a) for a in consts],
        out_specs=[pl.BlockSpec((H, 1, dq, tm), lambda i: (0, i, 0, 0)),
                   pl.BlockSpec((H, tm, dq), lambda i: (0, i, 0)),
                   pl.BlockSpec((H, 1, V_ROWS, tm), lambda i: (0, i, 0, 0))],
        out_shape=[jax.ShapeDtypeStruct((H, nt, dq, tm), BF16),
                   jax.ShapeDtypeStruct((H, T, dq), BF16),
                   jax.ShapeDtypeStruct((H, nt, V_ROWS, tm), BF16)],
        compiler_params=_params("parallel"),
        name="mla_prep",
    )(proj, proj, proj, cos4, sin4, *consts)


def _attn_kernel(qt_ref, k_ref, vt_ref, o_ref, m_ref, acc_ref):
    tb = ATTN_BLOCK
    hs = range(HEADS_PER_STEP)
    qi = pl.program_id(2)
    m_ref[...] = jnp.full(m_ref.shape, NEG_BIG, F32)
    acc_ref[...] = jnp.zeros(acc_ref.shape, F32)

    def update(kis, last_is_diagonal):
        nb = range(len(kis))
        s = [[_dot(k_ref[h, pl.ds(pl.multiple_of(ki * tb, tb), tb), :], qt_ref[h, 0]) for ki in kis]
             for h in hs]
        if last_is_diagonal:
            key = lax.broadcasted_iota(jnp.int32, (tb, tb), 0)
            qry = lax.broadcasted_iota(jnp.int32, (tb, tb), 1)
            for h in hs:
                s[h][-1] = jnp.where(key <= qry, s[h][-1], NEG_BIG)
        m_old = [m_ref[h] for h in hs]
        m_new = list(m_old)
        for h in hs:
            for j in nb:
                m_new[h] = jnp.maximum(m_new[h], jnp.max(s[h][j], axis=0, keepdims=True))
        p = [[jnp.exp2(s[h][j] - m_new[h]).astype(BF16) for j in nb] for h in hs]
        pv = [[_dot(vt_ref[h, kis[j]], p[h][j]) for j in nb] for h in hs]
        for h in hs:
            acc_ref[h] = jnp.exp2(m_old[h] - m_new[h]) * acc_ref[h] + sum(pv[h][1:], pv[h][0])
            m_ref[h] = m_new[h]

    def pair(i, carry):
        update([2 * i, 2 * i + 1], False)
        return carry

    lax.fori_loop(0, lax.shift_right_logical(qi, 1), pair, 0)
    odd = (qi & 1) == 1

    @pl.when(odd)
    def _():
        update([qi - 1, qi], True)

    @pl.when(jnp.logical_not(odd))
    def _():
        update([qi], True)

    for h in hs:
        acc = acc_ref[h]
        o_ref[:, h * MLA_V:(h + 1) * MLA_V] = (acc[0:MLA_V] / acc[MLA_V:MLA_V + 1]).T.astype(o_ref.dtype)


def _attention(q_t, k, v_t, B, S):
    H, T, dq = k.shape
    tb = ATTN_BLOCK
    nq = S // tb
    hp = HEADS_PER_STEP
    return pl.pallas_call(
        _attn_kernel,
        grid=(B, H // hp, nq),
        in_specs=[pl.BlockSpec((hp, 1, dq, tb), lambda b, h, i: (h, b * nq + i, 0, 0)),
                  pl.BlockSpec((hp, S, dq), lambda b, h, i: (h, b, 0)),
                  pl.BlockSpec((hp, nq, V_ROWS, tb), lambda b, h, i: (h, b, 0, 0))],
        out_specs=pl.BlockSpec((tb, hp * MLA_V), lambda b, h, i: (b * nq + i, h)),
        out_shape=jax.ShapeDtypeStruct((T, H * MLA_V), BF16),
        scratch_shapes=[pltpu.VMEM((hp, 1, tb), F32), pltpu.VMEM((hp, V_ROWS, tb), F32)],
        compiler_params=_params("parallel", "parallel", "arbitrary"),
        name="mla_attention",
    )(q_t, k, v_t)


def _merge_kernel(x_ref, ys_ref, ym_ref, yg_ref, g_ref, wgate_ref, ws_ref, wm_ref, wg_ref, wo_ref, h_ref):
    D = D_MODEL
    x = x_ref[...]
    xn = _rms(x, g_ref[...]).astype(BF16)
    mixed = None
    for idx, (y_ref, w_ref) in enumerate(((ys_ref, ws_ref), (ym_ref, wm_ref), (yg_ref, wg_ref))):
        gate = _sigmoid(_dot(xn, wgate_ref[:, idx * D:(idx + 1) * D]))
        term = gate * _dot(y_ref[...], w_ref[...])
        mixed = term if mixed is None else mixed + term
    h_ref[...] = x + _dot(mixed.astype(BF16), wo_ref[...])


def _merge(x2, ys, ym, yg, g, wgate, ws, wm, wg, wo):
    T, D = x2.shape
    tm = min(T, 512)
    full = lambda a: pl.BlockSpec(a.shape, lambda i: (0,) * a.ndim)
    rows = pl.BlockSpec((tm, D), lambda i: (i, 0))
    consts = [g, wgate, ws, wm, wg, wo]
    return pl.pallas_call(
        _merge_kernel,
        grid=(T // tm,),
        in_specs=[rows] * 4 + [full(a) for a in consts],
        out_specs=rows,
        out_shape=jax.ShapeDtypeStruct((T, D), F32),
        compiler_params=_params("parallel"),
        name="merge",
    )(x2, ys, ym, yg, *consts)


def _ffn_kernel(h_ref, g_ref, wu_ref, wd_ref, fg_ref, o_ref, hn_ref, acc_ref, *, final_norm):
    j = pl.program_id(1)

    @pl.when(j == 0)
    def _():
        hn_ref[...] = _rms(h_ref[...], g_ref[...]).astype(BF16)
        acc_ref[...] = jnp.zeros(acc_ref.shape, F32)

    up = jnp.maximum(_dot(hn_ref[...], wu_ref[...]), 0.0)
    acc_ref[...] += _dot((up * up).astype(BF16), wd_ref[...])

    @pl.when(j == pl.num_programs(1) - 1)
    def _():
        out = h_ref[...] + acc_ref[...]
        if final_norm:
            out = _rms(out, fg_ref[...])
        o_ref[...] = out


def _ffn(h, g, wu, wd, fg, final_norm):
    T, D = h.shape
    F = wu.shape[1]
    tm = min(T, 1024)
    tf = min(F, 2048)
    return pl.pallas_call(
        functools.partial(_ffn_kernel, final_norm=final_norm),
        grid=(T // tm, F // tf),
        in_specs=[pl.BlockSpec((tm, D), lambda i, j: (i, 0)),
                  pl.BlockSpec((1, D), lambda i, j: (0, 0)),
                  pl.BlockSpec((D, tf), lambda i, j: (0, j)),
                  pl.BlockSpec((tf, D), lambda i, j: (j, 0)),
                  pl.BlockSpec((1, D), lambda i, j: (0, 0))],
        out_specs=pl.BlockSpec((tm, D), lambda i, j: (i, 0)),
        out_shape=jax.ShapeDtypeStruct((T, D), F32),
        scratch_shapes=[pltpu.VMEM((tm, D), BF16), pltpu.VMEM((tm, D), F32)],
        compiler_params=_params("parallel", "arbitrary"),
        name="ffn",
    )(h, g, wu, wd, fg)


def _rotate_half_cols(w):
    half = w.shape[-1] // 2
    return jnp.concatenate([-w[..., half:], w[..., :half]], axis=-1)


def _split_in_proj(w_in):
    sizes = (SSD_INNER, SSD_INNER, 2 * SSD_GROUPS * SSD_STATE, SSD_HEADS, MLA_Q_LORA, MLA_KV_LORA, MLA_ROPE,
             GDN_QKV, GDN_V_HEADS * GDN_HEAD, GDN_V_HEADS, GDN_V_HEADS, 3 * D_MODEL)
    offs = np.cumsum((0,) + sizes)
    return [w_in[:, int(offs[i]):int(offs[i + 1])] for i in range(len(sizes))]


def _layer_weights(w_in, w_uq):
    z, xs, bc, dt, cq, ckv, kr, gqkv, gz, gb, ga, gates = _split_in_proj(w_in)
    small = jnp.concatenate([dt, gb, ga, jnp.zeros((D_MODEL, W_SM - 32), w_in.dtype)], axis=1)
    w_a = jnp.concatenate([gqkv, z, xs, gz, bc, cq, ckv, kr, _rotate_half_cols(kr), small], axis=1)
    H = MLA_HEADS
    wq = w_uq.reshape(MLA_Q_LORA, H, MLA_NOPE + MLA_ROPE)
    nope = wq[:, :, :MLA_NOPE].reshape(MLA_Q_LORA, H * MLA_NOPE)
    pe = wq[:, :, MLA_NOPE:]
    w_q = jnp.concatenate([nope, pe.reshape(MLA_Q_LORA, H * MLA_ROPE),
                           _rotate_half_cols(pe).reshape(MLA_Q_LORA, H * MLA_ROPE)], axis=1)
    return w_a, gates, w_q


def _pad_lanes(v, n=LANES):
    return jnp.concatenate([v, jnp.zeros((n - v.shape[0],), v.dtype)])[None, :]


def kernel(x, positions, norm1_g, w_in, ssd_conv_w, ssd_conv_b, ssd_dt_bias, ssd_a_log, ssd_d, ssd_norm_g,
           mla_q_norm_g, mla_w_uq, mla_kv_norm_g, mla_w_ukv, gdn_conv_w, gdn_dt_bias, gdn_a_log, gdn_norm_g,
           w_ssd_out, w_mla_out, w_gdn_out, w_out, norm2_g, w_up, w_down, final_norm_g):
    B, S, D = x.shape
    T = B * S
    x2 = x.reshape(T, D)

    inv = ROPE_THETA ** (-jnp.arange(0, MLA_ROPE, 2, dtype=F32) / MLA_ROPE)
    inv4 = jnp.tile(inv, LANES // inv.shape[0])[None, :]
    cos4, sin4 = _rope_tables(positions.reshape(T, 1), inv4)

    shift = _shift_matrix()
    expand = (jnp.arange(LANES)[:, None] == (jnp.arange(SSD_INNER)[None, :] // SSD_HEAD_DIM)).astype(BF16)

    for l in range(DEPTH):
        w_a, w_gate, w_q = _layer_weights(w_in[l].astype(BF16), mla_w_uq[l].astype(BF16))
        proj = _inproj(x2, norm1_g[l][None, :], w_a)

        y_ssd = _ssd(proj, B, S, ssd_conv_w[l], ssd_conv_b[l][None, :], shift,
                     _pad_lanes(ssd_dt_bias[l]), _pad_lanes(ssd_a_log[l]),
                     jnp.repeat(ssd_d[l], SSD_HEAD_DIM)[None, :], ssd_norm_g[l][None, :], expand)

        q, k, v = _mla_prep(proj, cos4, sin4, mla_q_norm_g[l][None, :], mla_kv_norm_g[l][None, :],
                            w_q, mla_w_ukv[l].astype(BF16))
        y_mla = _attention(q, k, v, B, S)

        y_gdn = _gdn(proj, B, S, gdn_conv_w[l], shift,
                     jnp.broadcast_to(gdn_dt_bias[l][:, None], (GDN_V_HEADS, CHUNK)),
                     jnp.broadcast_to(gdn_a_log[l][:, None], (GDN_V_HEADS, CHUNK)),
                     gdn_norm_g[l][None, :])

        h = _merge(x2, y_ssd, y_mla, y_gdn, norm1_g[l][None, :], w_gate,
                   w_ssd_out[l].astype(BF16), w_mla_out[l].astype(BF16), w_gdn_out[l].astype(BF16),
                   w_out[l].astype(BF16))
        x2 = _ffn(h, norm2_g[l][None, :], w_up[l].astype(BF16), w_down[l].astype(BF16),
                  final_norm_g[None, :], final_norm=(l == DEPTH - 1))
    return x2.reshape(B, S, D)
```

```python
import functools
import math

import jax
import jax.numpy as jnp
import numpy as np
from jax import lax
from jax.experimental import pallas as pl
from jax.experimental.pallas import tpu as pltpu

F32 = jnp.float32
BF16 = jnp.bfloat16
HIGHEST = lax.Precision.HIGHEST

EPS = 1e-6
D_MODEL = 1024
DEPTH = 2
SSD_HEADS = 16
SSD_HEAD_DIM = 64
SSD_GROUPS = 2
SSD_STATE = 128
SSD_INNER = 1024
MLA_HEADS = 8
MLA_NOPE = 128
MLA_ROPE = 64
MLA_V = 128
MLA_Q_LORA = 512
MLA_KV_LORA = 256
ROPE_THETA = 10000.0
GDN_HEAD = 128
GDN_V_HEADS = 8
GDN_QK_HEADS = 4
GDN_QK = GDN_QK_HEADS * GDN_HEAD
GDN_QKV = 2 * GDN_QK + GDN_V_HEADS * GDN_HEAD
D_FF = 4096
CONV_K = 4

CHUNK = 128
HALO = 16
LANES = 128
VMEM_LIMIT = 56 * 1024 * 1024

ROPE_ROWS = 2048
INPROJ_ROWS = 1024
INPROJ_COL_STEPS = 4
MERGE_ROWS = 512
FFN_ROWS = 1024
FFN_HIDDEN_TILE = 2048

W_GQKV, W_Z, W_XS, W_GZ, W_BC, W_CQ, W_CKV, W_KR, W_SM = 2048, 1024, 1024, 1024, 512, 512, 256, 128, 128
OFF_GQKV, OFF_Z, OFF_XS, OFF_GZ, OFF_BC, OFF_CQ, OFF_CKV, OFF_KR, OFF_SM = (
    0, 2048, 3072, 4096, 5120, 5632, 6144, 6400, 6528)
N_PROJ = 6656
SM_DT, SM_B, SM_A = 0, 16, 24

NEG_BIG = -1e30


def _rms(x, g):
    return x * lax.rsqrt(jnp.mean(x * x, axis=-1, keepdims=True) + EPS) * g


def _sigmoid(x):
    return 1.0 / (1.0 + jnp.exp(-x))


def _silu(x):
    h = 0.5 * x
    return h + h * jnp.tanh(h)


def _softplus(x):
    return jnp.maximum(x, 0.0) + jnp.log1p(jnp.exp(-jnp.abs(x)))


def _dot(a, b):
    return jnp.dot(a, b, preferred_element_type=F32)


def _dot_nt(a, b):
    return lax.dot_general(a, b, (((1,), (1,)), ((), ())), preferred_element_type=F32)


def _params(*sem):
    return pltpu.CompilerParams(dimension_semantics=sem, vmem_limit_bytes=VMEM_LIMIT)


def _rope_kernel(pos_ref, inv_ref, cos_ref, sin_ref):
    ang = pos_ref[...].astype(F32) * inv_ref[...]
    cos_ref[...] = jnp.cos(ang)
    sin_ref[...] = jnp.sin(ang)


def _rope_tables(pos, inv4):
    R = pos.shape[0]
    tm = min(R, ROPE_ROWS)
    return pl.pallas_call(
        _rope_kernel,
        grid=(R // tm,),
        in_specs=[pl.BlockSpec((tm, 1), lambda i: (i, 0)),
                  pl.BlockSpec((1, LANES), lambda i: (0, 0))],
        out_specs=[pl.BlockSpec((tm, LANES), lambda i: (i, 0))] * 2,
        out_shape=[jax.ShapeDtypeStruct((R, LANES), F32)] * 2,
        compiler_params=_params("parallel"),
        name="rope_tables",
    )(pos, inv4)


def _inproj_kernel(x_ref, g_ref, w_ref, o_ref, xn_ref):
    @pl.when(pl.program_id(1) == 0)
    def _():
        xn_ref[...] = _rms(x_ref[...], g_ref[...]).astype(BF16)

    o_ref[...] = _dot(xn_ref[...], w_ref[...])


def _inproj(x2, g, w):
    T, D = x2.shape
    N = w.shape[1]
    tm = min(T, INPROJ_ROWS)
    tn = N // INPROJ_COL_STEPS
    return pl.pallas_call(
        _inproj_kernel,
        grid=(T // tm, N // tn),
        in_specs=[pl.BlockSpec((tm, D), lambda i, j: (i, 0)),
                  pl.BlockSpec((1, D), lambda i, j: (0, 0)),
                  pl.BlockSpec((D, tn), lambda i, j: (0, j))],
        out_specs=pl.BlockSpec((tm, tn), lambda i, j: (i, j)),
        out_shape=jax.ShapeDtypeStruct((T, N), F32),
        scratch_shapes=[pltpu.VMEM((tm, D), BF16)],
        compiler_params=_params("parallel", "arbitrary"),
        name="in_proj",
    )(x2, g, w)


def _shift_matrix():
    r = np.arange((CONV_K - 1) * CHUNK)[:, None]
    c = np.arange(HALO + CHUNK)[None, :]
    return jnp.asarray(c == HALO + (r % CHUNK) - (r // CHUNK + 1), BF16)


def _causal_conv(tail_ref, cur, w_ref, shift_ref):
    cur_b = cur.astype(BF16)
    delayed = _dot(shift_ref[...], jnp.concatenate([tail_ref[...], cur_b], axis=0))
    tail_ref[...] = cur_b[CHUNK - HALO:CHUNK, :]
    acc = w_ref[CONV_K - 1:CONV_K, :] * cur
    for j in range(1, CONV_K):
        acc = acc + w_ref[CONV_K - 1 - j:CONV_K - j, :] * delayed[(j - 1) * CHUNK:j * CHUNK]
    return acc


def _ssd_kernel(z_ref, xs_ref, bc_ref, sm_ref, cw_ref, cb_ref, shift_ref, dtb_ref,
                alog_ref, dsk_ref, ng_ref, e_ref, o_ref, tail_ref, h_ref):
    G, N, P = SSD_GROUPS, SSD_STATE, SSD_HEAD_DIM
    HG = SSD_HEADS // G
    GW = HG * P
    NP = SSD_HEADS // 2
    rows_ = range(z_ref.shape[0])
    gsl = [slice(g * GW, (g + 1) * GW) for g in range(G)]
    row_groups = [(r, g) for r in rows_ for g in range(G)]
    row_heads = [(r, e) for r in rows_ for e in range(SSD_HEADS)]
    row_pairs = [(r, p) for r in rows_ for p in range(NP)]

    @pl.when(pl.program_id(1) == 0)
    def _():
        tail_ref[...] = jnp.zeros(tail_ref.shape, tail_ref.dtype)
        h_ref[...] = jnp.zeros(h_ref.shape, F32)

    raw = [jnp.concatenate([xs_ref[r], bc_ref[r]], axis=1) for r in rows_]
    xbc = [_silu(_causal_conv(tail_ref.at[r], raw[r], cw_ref, shift_ref) + cb_ref[...]) for r in rows_]
    xs = [x[:, 0:SSD_INNER] for x in xbc]
    bc = [x[:, SSD_INNER:] for x in xbc]

    row = lax.broadcasted_iota(jnp.int32, (CHUNK, CHUNK), 0)
    col = lax.broadcasted_iota(jnp.int32, (CHUNK, CHUNK), 1)
    tril = row >= col
    tril_f = tril.astype(F32)
    lane = lax.broadcasted_iota(jnp.int32, (CHUNK, LANES), 1)
    left = lane < P

    neg_a = -jnp.exp(alog_ref[...])
    dt = [_softplus(sm_ref[r] + dtb_ref[...]) for r in rows_]
    a_cum = [jnp.dot(tril_f, d * neg_a, precision=HIGHEST, preferred_element_type=F32) for d in dt]
    a_cum_t = [a.T for a in a_cum]
    dt_t = [d.T for d in dt]
    stack = [jnp.concatenate([jnp.exp(a_cum[r]), dt[r] * jnp.exp(a_cum[r][CHUNK - 1:CHUNK, :] - a_cum[r])], axis=0)
             for r in rows_]
    hi = [x.astype(BF16) for x in stack]
    lo = [(x - h.astype(F32)).astype(BF16) for x, h in zip(stack, hi)]
    ex = [_dot(h, e_ref[...]) + _dot(l, e_ref[...]) for h, l in zip(hi, lo)]
    e_a_x = [x[0:CHUNK] for x in ex]
    sdec_x = [x[CHUNK:2 * CHUNK] for x in ex]

    b_g = [bc[r][:, g * N:(g + 1) * N] for r, g in row_groups]
    c_b = [bc[r][:, (G + g) * N:(G + g + 1) * N].astype(BF16) for r, g in row_groups]
    cb = [_dot_nt(c, b.astype(BF16)) for c, b in zip(c_b, b_g)]
    x_sd = [(xs[r][:, gsl[g]] * sdec_x[r][:, gsl[g]]).astype(BF16) for r, g in row_groups]
    st = [_dot(b.T.astype(BF16), x) for b, x in zip(b_g, x_sd)]
    h_prev = [h_ref[r, :, gsl[g]] for r, g in row_groups]
    y_off = [_dot(c_b[i], h_prev[i].astype(BF16)) * e_a_x[r][:, gsl[g]] for i, (r, g) in enumerate(row_groups)]
    for i, (r, g) in enumerate(row_groups):
        h_ref[r, :, gsl[g]] = h_prev[i] * e_a_x[r][CHUNK - 1:CHUNK, gsl[g]] + st[i]
    seg = [jnp.where(tril, a_cum[r][:, e:e + 1] - a_cum_t[r][e:e + 1, :], NEG_BIG) for r, e in row_heads]
    m = [cb[r * G + e // HG] * jnp.exp(seg[i]) * dt_t[r][e:e + 1, :] for i, (r, e) in enumerate(row_heads)]
    m2 = [jnp.concatenate([m[r * SSD_HEADS + 2 * p], m[r * SSD_HEADS + 2 * p + 1]], axis=1).astype(BF16)
          for r, p in row_pairs]
    xp = [xs[r][:, 2 * p * P:(2 * p + 2) * P] for r, p in row_pairs]
    bd = [jnp.concatenate([jnp.where(left, x, 0.0), jnp.where(left, 0.0, x)], axis=0).astype(BF16)
          for x in xp]
    diag = [_dot(a, b) for a, b in zip(m2, bd)]
    for r in rows_:
        y = (jnp.concatenate(diag[r * NP:(r + 1) * NP], axis=1) + jnp.concatenate(y_off[r * G:(r + 1) * G], axis=1)
             + dsk_ref[...] * xs[r])
        y = y * _silu(z_ref[r])
        o_ref[r] = jnp.concatenate([_rms(y[:, gs], ng_ref[:, gs]) for gs in gsl], axis=1).astype(o_ref.dtype)


SSD_BATCH_PER_STEP = 4


def _ssd(proj, B, S, cw, cb, shift, dtb, alog, dsk, ng, expand):
    nc = S // CHUNK
    nb = SSD_BATCH_PER_STEP
    proj3 = proj.reshape(B, S, proj.shape[1])
    full = lambda a: pl.BlockSpec(a.shape, lambda b, c: (0,) * a.ndim)
    rows = lambda w, idx: pl.BlockSpec((nb, CHUNK, w), lambda b, c: (b, c, idx))
    consts = [cw, cb, shift, dtb, alog, dsk, ng, expand]
    out = pl.pallas_call(
        _ssd_kernel,
        grid=(B // nb, nc),
        in_specs=[rows(W_Z, OFF_Z // W_Z), rows(W_XS, OFF_XS // W_XS), rows(W_BC, OFF_BC // W_BC),
                  rows(W_SM, OFF_SM // W_SM)] + [full(a) for a in consts],
        out_specs=pl.BlockSpec((nb, CHUNK, SSD_INNER), lambda b, c: (b, c, 0)),
        out_shape=jax.ShapeDtypeStruct((B, S, SSD_INNER), BF16),
        scratch_shapes=[pltpu.VMEM((nb, HALO, W_XS + W_BC), BF16),
                        pltpu.VMEM((nb, SSD_STATE, SSD_INNER), F32)],
        compiler_params=_params("parallel", "arbitrary"),
        name="ssd_mixer",
    )(proj3, proj3, proj3, proj3, *consts)
    return out.reshape(B * S, SSD_INNER)


def _l2norm(x):
    return x * lax.rsqrt(jnp.sum(x * x, axis=-1, keepdims=True) + EPS)


def _split(x):
    hi = x.astype(BF16)
    return hi, (x - hi.astype(F32)).astype(BF16)


def _dot3(a, b):
    (ah, al), (bh, bl) = a, b
    return _dot(jnp.concatenate([ah, al, ah], axis=1), jnp.concatenate([bh, bh, bl], axis=0))


def _dot2(a, b):
    return _dot(jnp.concatenate(a, axis=1), jnp.concatenate([b, b], axis=0))


def _dot2r(a, b):
    return _dot(jnp.concatenate([a, a], axis=1), jnp.concatenate(b, axis=0))


SOLVE_BLOCK_LOG2 = 3


def _block_masks(row, col):
    n_levels = int(math.log2(CHUNK)) - SOLVE_BLOCK_LOG2
    diag = (row >> SOLVE_BLOCK_LOG2) == (col >> SOLVE_BLOCK_LOG2)
    merges = []
    for lv in range(n_levels):
        s = SOLVE_BLOCK_LOG2 + lv
        same = (row >> (s + 1)) == (col >> (s + 1))
        lower_left = jnp.where(same, ((row >> s) & 1) - ((col >> s) & 1), 0) == 1
        merges.append(lower_left)
    return diag, merges


def _unit_lower_inverses(mats, eye, diag_mask, merge_masks):
    def as_factor(mask):
        return jnp.where(mask, 1.0, 0.0).astype(BF16)

    eye_b = eye.astype(BF16)
    diag_b = as_factor(diag_mask)
    d1 = [a * diag_b for a in mats]
    d2 = [_split(_dot(d, d)) for d in d1]
    t0 = [eye_b - d for d in d1]
    ts = [t.astype(F32) + _dot2r(t, x) for t, x in zip(t0, d2)]
    d4 = [_split(_dot3(x, x)) for x in d2]
    ts = [t + _dot3(_split(t), x) for t, x in zip(ts, d4)]
    for lv, m in enumerate(merge_masks):
        n = 1 << (SOLVE_BLOCK_LOG2 + lv)
        pairs = range(CHUNK // (2 * n))

        def take(x):
            return jnp.concatenate([x[(2 * b + 1) * n:(2 * b + 2) * n] for b in pairs], axis=0)

        def put(x, part):
            pieces = []
            for b in pairs:
                pieces += [x[2 * b * n:(2 * b + 1) * n], part[b * n:(b + 1) * n]]
            return jnp.concatenate(pieces, axis=0)

        tsp = [_split(t) for t in ts]
        mb = as_factor(m)
        low = [take(t) for t in ts]
        mid = [_split(_dot2(_split(x), a * mb)) for x, a in zip(low, mats)]
        ts = [put(t, x - _dot3(md, tp)) for t, x, md, tp in zip(ts, low, mid, tsp)]
    return ts


def _gdn_kernel(qkv_ref, gz_ref, sm_ref, cw_ref, shift_ref, dtb_ref, alog_ref, ng_ref, o_ref, tail_ref, s_ref):
    HD = GDN_HEAD
    NH = GDN_V_HEADS
    NB = qkv_ref.shape[0]
    rows_ = range(NB)
    items = [(r, h) for r in rows_ for h in range(NH)]

    @pl.when(pl.program_id(1) == 0)
    def _():
        tail_ref[...] = jnp.zeros(tail_ref.shape, tail_ref.dtype)
        s_ref[...] = jnp.zeros(s_ref.shape, F32)

    qkv = [_silu(_causal_conv(tail_ref.at[r], qkv_ref[r], cw_ref, shift_ref)) for r in rows_]

    row = lax.broadcasted_iota(jnp.int32, (CHUNK, CHUNK), 0)
    col = lax.broadcasted_iota(jnp.int32, (CHUNK, CHUNK), 1)
    incl = row >= col
    strict = row > col
    eye = (row == col).astype(F32)
    diag_mask, merge_masks = _block_masks(row, col)
    upper = (row <= col).astype(F32)
    pad = jnp.zeros((CHUNK - 2 * NH, CHUNK), F32)

    sm_t = [sm_ref[r].T for r in rows_]
    beta_t = [_sigmoid(x[SM_B:SM_B + NH, :]) for x in sm_t]
    g_t = [-jnp.exp(alog_ref[...]) * _softplus(x[SM_A:SM_A + NH, :] + dtb_ref[...]) for x in sm_t]
    gc_t = [jnp.dot(x, upper, precision=HIGHEST, preferred_element_type=F32) for x in g_t]
    cols = [jnp.concatenate([beta_t[r], gc_t[r], pad], axis=0).T for r in rows_]

    q = [[_l2norm(qkv[r][:, j * HD:(j + 1) * HD]) * (HD ** -0.5) for j in range(GDN_QK_HEADS)] for r in rows_]
    k = [[_l2norm(qkv[r][:, GDN_QK + j * HD:GDN_QK + (j + 1) * HD]) for j in range(GDN_QK_HEADS)] for r in rows_]
    k_t = [[x.T.astype(BF16) for x in k[r]] for r in rows_]
    qk_raw = [[_dot(a.astype(BF16), b) for a, b in zip(q[r], k_t[r])] for r in rows_]

    b_col = [cols[r][:, h:h + 1] for r, h in items]
    g_col = [cols[r][:, NH + h:NH + h + 1] for r, h in items]
    g_last = [gc_t[r][h:h + 1, CHUNK - 1:CHUNK] for r, h in items]
    decay = [jnp.exp(jnp.where(incl, g_col[i] - gc_t[r][h:h + 1, :], NEG_BIG)) for i, (r, h) in enumerate(items)]
    kb = [k[r][h // 2] * b_col[i] for i, (r, h) in enumerate(items)]
    a_mat = [jnp.where(strict, _dot(kb[i].astype(BF16), k_t[r][h // 2]) * decay[i], 0.0).astype(BF16)
             for i, (r, h) in enumerate(items)]
    t_inv = _unit_lower_inverses(a_mat, eye, diag_mask, merge_masks)

    e_g = [jnp.exp(x) for x in g_col]
    v = [qkv[r][:, 2 * GDN_QK + h * HD:2 * GDN_QK + (h + 1) * HD] for r, h in items]
    rhs = [jnp.concatenate([v[i] * b_col[i], kb[i] * e_g[i]], axis=1).astype(BF16) for i in range(len(items))]
    sol = [_dot2(_split(t_inv[i]), rhs[i]) for i in range(len(items))]
    state = [s_ref[i] for i in range(len(items))]
    ws = [_dot(jnp.concatenate([sol[i][:, HD:2 * HD], q[r][h // 2] * e_g[i]], axis=0).astype(BF16),
               state[i].astype(BF16)) for i, (r, h) in enumerate(items)]
    v_new_b = [(sol[i][:, 0:HD] - ws[i][0:CHUNK]).astype(BF16) for i in range(len(items))]
    o = [ws[i][CHUNK:2 * CHUNK] + _dot((qk_raw[r][h // 2] * decay[i]).astype(BF16), v_new_b[i])
         for i, (r, h) in enumerate(items)]
    k_dec_t = [(k[r][h // 2] * jnp.exp(g_last[i] - g_col[i])).T.astype(BF16) for i, (r, h) in enumerate(items)]
    for i in range(len(items)):
        s_ref[i] = state[i] * jnp.exp(g_last[i]) + _dot(k_dec_t[i], v_new_b[i])
    for i, (r, h) in enumerate(items):
        out = _rms(o[i], ng_ref[...]) * _silu(gz_ref[r, :, h * HD:(h + 1) * HD])
        o_ref[r, :, h * HD:(h + 1) * HD] = out.astype(o_ref.dtype)


GDN_BATCH_PER_STEP = 4


def _gdn(proj, B, S, cw, shift, dtb, alog, ng):
    nc = S // CHUNK
    nb = GDN_BATCH_PER_STEP
    width = GDN_V_HEADS * GDN_HEAD
    proj3 = proj.reshape(B, S, proj.shape[1])
    full = lambda a: pl.BlockSpec(a.shape, lambda b, c: (0,) * a.ndim)
    rows = lambda w, idx: pl.BlockSpec((nb, CHUNK, w), lambda b, c: (b, c, idx))
    consts = [cw, shift, dtb, alog, ng]
    out = pl.pallas_call(
        _gdn_kernel,
        grid=(B // nb, nc),
        in_specs=[rows(W_GQKV, OFF_GQKV // W_GQKV), rows(W_GZ, OFF_GZ // W_GZ),
                  rows(W_SM, OFF_SM // W_SM)] + [full(a) for a in consts],
        out_specs=pl.BlockSpec((nb, CHUNK, width), lambda b, c: (b, c, 0)),
        out_shape=jax.ShapeDtypeStruct((B, S, width), BF16),
        scratch_shapes=[pltpu.VMEM((nb, HALO, GDN_QKV), BF16),
                        pltpu.VMEM((nb * GDN_V_HEADS, GDN_HEAD, GDN_HEAD), F32)],
        compiler_params=_params("parallel", "arbitrary"),
        name="gdn_mixer",
    )(proj3, proj3, proj3, *consts)
    return out.reshape(B * S, width)


ATTN_BLOCK = 512
LOG2E = math.log2(math.e)
BF16_SUBLANES = 16
V_ROWS = MLA_V + BF16_SUBLANES
HEADS_PER_STEP = 4


def _mla_prep_kernel(cq_ref, ckv_ref, kr_ref, cos_ref, sin_ref, qg_ref, kvg_ref, wq_ref, wkv_ref,
                     qt_ref, k_ref, vt_ref):
    H = MLA_HEADS
    tm = cq_ref.shape[0]
    scale = (MLA_NOPE + MLA_ROPE) ** -0.5 * LOG2E
    cos4 = cos_ref[...]
    sin4 = sin_ref[...]
    lane = lax.broadcasted_iota(jnp.int32, (tm, LANES), 1)
    left = lane < MLA_ROPE

    qm = _dot(_rms(cq_ref[...], qg_ref[...]).astype(BF16), wq_ref[...])
    kvm = _dot(_rms(ckv_ref[...], kvg_ref[...]).astype(BF16), wkv_ref[...])

    kr = kr_ref[...]
    k_rope = (kr * jnp.where(left, cos4, sin4)
              + pltpu.roll(kr, MLA_ROPE, axis=1) * jnp.where(left, sin4, cos4)).astype(BF16)

    pe_off = H * MLA_NOPE
    rot_off = pe_off + H * MLA_ROPE
    for h in range(H):
        jb = (h // 2) * LANES
        pe = qm[:, pe_off + jb:pe_off + jb + LANES]
        rot = qm[:, rot_off + jb:rot_off + jb + LANES]
        roped = pe * cos4 + rot * sin4
        mine = left if h % 2 == 0 else jnp.logical_not(left)
        q_nope = qm[:, h * MLA_NOPE:(h + 1) * MLA_NOPE]
        qt_ref[h, 0, 0:LANES, :] = (q_nope * scale).T.astype(BF16)
        qt_ref[h, 0, LANES:2 * LANES, :] = (jnp.where(mine, roped, 0.0) * scale).T.astype(BF16)
        base = h * (MLA_NOPE + MLA_V)
        k_ref[h] = jnp.concatenate([kvm[:, base:base + MLA_NOPE].astype(BF16), k_rope], axis=1)
        vt_ref[h, 0, 0:MLA_V, :] = kvm[:, base + MLA_NOPE:base + MLA_NOPE + MLA_V].T.astype(BF16)
        ones_row = lax.broadcasted_iota(jnp.int32, (BF16_SUBLANES, tm), 0) == 0
        vt_ref[h, 0, MLA_V:V_ROWS, :] = jnp.where(ones_row, 1.0, 0.0).astype(BF16)


def _mla_prep(proj, cos4, sin4, qg, kvg, wq, wkv):
    T = proj.shape[0]
    H = MLA_HEADS
    tm = ATTN_BLOCK
    nt = T // tm
    full = lambda a: pl.BlockSpec(a.shape, lambda i: (0,) * a.ndim)
    rows = lambda w, idx: pl.BlockSpec((tm, w), lambda i: (i, idx))
    consts = [qg, kvg, wq, wkv]
    dq = 2 * LANES
    return pl.pallas_call(
        _mla_prep_kernel,
        grid=(nt,),
        in_specs=[rows(W_CQ, OFF_CQ // W_CQ), rows(W_CKV, OFF_CKV // W_CKV), rows(W_KR, OFF_KR // W_KR),
                  rows(LANES, 0), rows(LANES, 0)] + [full(a) for a in consts],
        out_specs=[pl.BlockSpec((H, 1, dq, tm), lambda i: (0, i, 0, 0)),
                   pl.BlockSpec((H, tm, dq), lambda i: (0, i, 0)),
                   pl.BlockSpec((H, 1, V_ROWS, tm), lambda i: (0, i, 0, 0))],
        out_shape=[jax.ShapeDtypeStruct((H, nt, dq, tm), BF16),
                   jax.ShapeDtypeStruct((H, T, dq), BF16),
                   jax.ShapeDtypeStruct((H, nt, V_ROWS, tm), BF16)],
        compiler_params=_params("parallel"),
        name="mla_prep",
    )(proj, proj, proj, cos4, sin4, *consts)


def _attn_kernel(qt_ref, k_ref, vt_ref, o_ref, m_ref, acc_ref):
    tb = ATTN_BLOCK
    hs = range(HEADS_PER_STEP)
    qi = pl.program_id(2)
    m_ref[...] = jnp.full(m_ref.shape, NEG_BIG, F32)
    acc_ref[...] = jnp.zeros(acc_ref.shape, F32)

    def update(kis, last_is_diagonal):
        nb = range(len(kis))
        s = [[_dot(k_ref[h, pl.ds(pl.multiple_of(ki * tb, tb), tb), :], qt_ref[h, 0]) for ki in kis]
             for h in hs]
        if last_is_diagonal:
            key = lax.broadcasted_iota(jnp.int32, (tb, tb), 0)
            qry = lax.broadcasted_iota(jnp.int32, (tb, tb), 1)
            for h in hs:
                s[h][-1] = jnp.where(key <= qry, s[h][-1], NEG_BIG)
        m_old = [m_ref[h] for h in hs]
        m_new = list(m_old)
        for h in hs:
            for j in nb:
                m_new[h] = jnp.maximum(m_new[h], jnp.max(s[h][j], axis=0, keepdims=True))
        p = [[jnp.exp2(s[h][j] - m_new[h]).astype(BF16) for j in nb] for h in hs]
        pv = [[_dot(vt_ref[h, kis[j]], p[h][j]) for j in nb] for h in hs]
        for h in hs:
            acc_ref[h] = jnp.exp2(m_old[h] - m_new[h]) * acc_ref[h] + sum(pv[h][1:], pv[h][0])
            m_ref[h] = m_new[h]

    def pair(i, carry):
        update([2 * i, 2 * i + 1], False)
        return carry

    lax.fori_loop(0, lax.shift_right_logical(qi, 1), pair, 0)
    odd = (qi & 1) == 1

    @pl.when(odd)
    def _():
        update([qi - 1, qi], True)

    @pl.when(jnp.logical_not(odd))
    def _():
        update([qi], True)

    for h in hs:
        acc = acc_ref[h]
        o_ref[:, h * MLA_V:(h + 1) * MLA_V] = (acc[0:MLA_V] / acc[MLA_V:MLA_V + 1]).T.astype(o_ref.dtype)


def _attention(q_t, k, v_t, B, S):
    H, T, dq = k.shape
    tb = ATTN_BLOCK
    nq = S // tb
    hp = HEADS_PER_STEP
    return pl.pallas_call(
        _attn_kernel,
        grid=(B, H // hp, nq),
        in_specs=[pl.BlockSpec((hp, 1, dq, tb), lambda b, h, i: (h, b * nq + i, 0, 0)),
                  pl.BlockSpec((hp, S, dq), lambda b, h, i: (h, b, 0)),
                  pl.BlockSpec((hp, nq, V_ROWS, tb), lambda b, h, i: (h, b, 0, 0))],
        out_specs=pl.BlockSpec((tb, hp * MLA_V), lambda b, h, i: (b * nq + i, h)),
        out_shape=jax.ShapeDtypeStruct((T, H * MLA_V), BF16),
        scratch_shapes=[pltpu.VMEM((hp, 1, tb), F32), pltpu.VMEM((hp, V_ROWS, tb), F32)],
        compiler_params=_params("parallel", "parallel", "arbitrary"),
        name="mla_attention",
    )(q_t, k, v_t)


def _merge_kernel(x_ref, ys_ref, ym_ref, yg_ref, g_ref, wgate_ref, ws_ref, wm_ref, wg_ref, wo_ref, h_ref):
    D = D_MODEL
    x = x_ref[...]
    xn = _rms(x, g_ref[...]).astype(BF16)
    mixed = None
    for idx, (y_ref, w_ref) in enumerate(((ys_ref, ws_ref), (ym_ref, wm_ref), (yg_ref, wg_ref))):
        gate = _sigmoid(_dot(xn, wgate_ref[:, idx * D:(idx + 1) * D]))
        term = gate * _dot(y_ref[...], w_ref[...])
        mixed = term if mixed is None else mixed + term
    h_ref[...] = x + _dot(mixed.astype(BF16), wo_ref[...])


def _merge(x2, ys, ym, yg, g, wgate, ws, wm, wg, wo):
    T, D = x2.shape
    tm = min(T, MERGE_ROWS)
    full = lambda a: pl.BlockSpec(a.shape, lambda i: (0,) * a.ndim)
    rows = pl.BlockSpec((tm, D), lambda i: (i, 0))
    consts = [g, wgate, ws, wm, wg, wo]
    return pl.pallas_call(
        _merge_kernel,
        grid=(T // tm,),
        in_specs=[rows] * 4 + [full(a) for a in consts],
        out_specs=rows,
        out_shape=jax.ShapeDtypeStruct((T, D), F32),
        compiler_params=_params("parallel"),
        name="merge",
    )(x2, ys, ym, yg, *consts)


def _ffn_kernel(h_ref, g_ref, wu_ref, wd_ref, fg_ref, o_ref, hn_ref, acc_ref, *, final_norm):
    j = pl.program_id(1)

    @pl.when(j == 0)
    def _():
        hn_ref[...] = _rms(h_ref[...], g_ref[...]).astype(BF16)
        acc_ref[...] = jnp.zeros(acc_ref.shape, F32)

    up = jnp.maximum(_dot(hn_ref[...], wu_ref[...]), 0.0)
    acc_ref[...] += _dot((up * up).astype(BF16), wd_ref[...])

    @pl.when(j == pl.num_programs(1) - 1)
    def _():
        out = h_ref[...] + acc_ref[...]
        if final_norm:
            out = _rms(out, fg_ref[...])
        o_ref[...] = out


def _ffn(h, g, wu, wd, fg, final_norm):
    T, D = h.shape
    F = wu.shape[1]
    tm = min(T, FFN_ROWS)
    tf = min(F, FFN_HIDDEN_TILE)
    return pl.pallas_call(
        functools.partial(_ffn_kernel, final_norm=final_norm),
        grid=(T // tm, F // tf),
        in_specs=[pl.BlockSpec((tm, D), lambda i, j: (i, 0)),
                  pl.BlockSpec((1, D), lambda i, j: (0, 0)),
                  pl.BlockSpec((D, tf), lambda i, j: (0, j)),
                  pl.BlockSpec((tf, D), lambda i, j: (j, 0)),
                  pl.BlockSpec((1, D), lambda i, j: (0, 0))],
        out_specs=pl.BlockSpec((tm, D), lambda i, j: (i, 0)),
        out_shape=jax.ShapeDtypeStruct((T, D), F32),
        scratch_shapes=[pltpu.VMEM((tm, D), BF16), pltpu.VMEM((tm, D), F32)],
        compiler_params=_params("parallel", "arbitrary"),
        name="ffn",
    )(h, g, wu, wd, fg)


def _rotate_half_cols(w):
    half = w.shape[-1] // 2
    return jnp.concatenate([-w[..., half:], w[..., :half]], axis=-1)


def _split_in_proj(w_in):
    sizes = (SSD_INNER, SSD_INNER, 2 * SSD_GROUPS * SSD_STATE, SSD_HEADS, MLA_Q_LORA, MLA_KV_LORA, MLA_ROPE,
             GDN_QKV, GDN_V_HEADS * GDN_HEAD, GDN_V_HEADS, GDN_V_HEADS, 3 * D_MODEL)
    offs = np.cumsum((0,) + sizes)
    return [w_in[:, int(offs[i]):int(offs[i + 1])] for i in range(len(sizes))]


def _layer_weights(w_in, w_uq):
    z, xs, bc, dt, cq, ckv, kr, gqkv, gz, gb, ga, gates = _split_in_proj(w_in)
    small = jnp.concatenate([dt, gb, ga, jnp.zeros((D_MODEL, W_SM - 32), w_in.dtype)], axis=1)
    w_a = jnp.concatenate([gqkv, z, xs, gz, bc, cq, ckv, kr, _rotate_half_cols(kr), small], axis=1)
    H = MLA_HEADS
    wq = w_uq.reshape(MLA_Q_LORA, H, MLA_NOPE + MLA_ROPE)
    nope = wq[:, :, :MLA_NOPE].reshape(MLA_Q_LORA, H * MLA_NOPE)
    pe = wq[:, :, MLA_NOPE:]
    w_q = jnp.concatenate([nope, pe.reshape(MLA_Q_LORA, H * MLA_ROPE),
                           _rotate_half_cols(pe).reshape(MLA_Q_LORA, H * MLA_ROPE)], axis=1)
    return w_a, gates, w_q


def _pad_lanes(v, n=LANES):
    return jnp.concatenate([v, jnp.zeros((n - v.shape[0],), v.dtype)])[None, :]


def kernel(x, positions, norm1_g, w_in, ssd_conv_w, ssd_conv_b, ssd_dt_bias, ssd_a_log, ssd_d, ssd_norm_g,
           mla_q_norm_g, mla_w_uq, mla_kv_norm_g, mla_w_ukv, gdn_conv_w, gdn_dt_bias, gdn_a_log, gdn_norm_g,
           w_ssd_out, w_mla_out, w_gdn_out, w_out, norm2_g, w_up, w_down, final_norm_g):
    B, S, D = x.shape
    T = B * S
    x2 = x.reshape(T, D)

    inv = ROPE_THETA ** (-jnp.arange(0, MLA_ROPE, 2, dtype=F32) / MLA_ROPE)
    inv4 = jnp.tile(inv, LANES // inv.shape[0])[None, :]
    cos4, sin4 = _rope_tables(positions.reshape(T, 1), inv4)

    shift = _shift_matrix()
    expand = (jnp.arange(LANES)[:, None] == (jnp.arange(SSD_INNER)[None, :] // SSD_HEAD_DIM)).astype(BF16)

    for l in range(DEPTH):
        w_a, w_gate, w_q = _layer_weights(w_in[l].astype(BF16), mla_w_uq[l].astype(BF16))
        proj = _inproj(x2, norm1_g[l][None, :], w_a)

        y_ssd = _ssd(proj, B, S, ssd_conv_w[l], ssd_conv_b[l][None, :], shift,
                     _pad_lanes(ssd_dt_bias[l]), _pad_lanes(ssd_a_log[l]),
                     jnp.repeat(ssd_d[l], SSD_HEAD_DIM)[None, :], ssd_norm_g[l][None, :], expand)

        q, k, v = _mla_prep(proj, cos4, sin4, mla_q_norm_g[l][None, :], mla_kv_norm_g[l][None, :],
                            w_q, mla_w_ukv[l].astype(BF16))
        y_mla = _attention(q, k, v, B, S)

        y_gdn = _gdn(proj, B, S, gdn_conv_w[l], shift,
                     jnp.broadcast_to(gdn_dt_bias[l][:, None], (GDN_V_HEADS, CHUNK)),
                     jnp.broadcast_to(gdn_a_log[l][:, None], (GDN_V_HEADS, CHUNK)),
                     gdn_norm_g[l][None, :])

        h = _merge(x2, y_ssd, y_mla, y_gdn, norm1_g[l][None, :], w_gate,
                   w_ssd_out[l].astype(BF16), w_mla_out[l].astype(BF16), w_gdn_out[l].astype(BF16),
                   w_out[l].astype(BF16))
        x2 = _ffn(h, norm2_g[l][None, :], w_up[l].astype(BF16), w_down[l].astype(BF16),
                  final_norm_g[None, :], final_norm=(l == DEPTH - 1))
    return x2.reshape(B, S, D)
```

```python
import functools
import math

import jax
import jax.numpy as jnp
import numpy as np
from jax import lax
from jax.experimental import pallas as pl
from jax.experimental.pallas import tpu as pltpu

F32 = jnp.float32
BF16 = jnp.bfloat16
HIGHEST = lax.Precision.HIGHEST

EPS = 1e-6
D_MODEL = 1024
DEPTH = 2
SSD_HEADS = 16
SSD_HEAD_DIM = 64
SSD_GROUPS = 2
SSD_STATE = 128
SSD_INNER = 1024
MLA_HEADS = 8
MLA_NOPE = 128
MLA_ROPE = 64
MLA_V = 128
MLA_Q_LORA = 512
MLA_KV_LORA = 256
ROPE_THETA = 10000.0
GDN_HEAD = 128
GDN_V_HEADS = 8
GDN_QK_HEADS = 4
GDN_QK = GDN_QK_HEADS * GDN_HEAD
GDN_QKV = 2 * GDN_QK + GDN_V_HEADS * GDN_HEAD
D_FF = 4096
CONV_K = 4

CHUNK = 128
HALO = 16
LANES = 128
VMEM_LIMIT = 56 * 1024 * 1024

ROPE_ROWS = 2048
INPROJ_ROWS = 1024
INPROJ_COL_STEPS = 4
MERGE_ROWS = 512
FFN_ROWS = 1024
FFN_HIDDEN_TILE = 2048

W_GQKV, W_Z, W_XS, W_GZ, W_BC, W_CQ, W_CKV, W_KR, W_SM = 2048, 1024, 1024, 1024, 512, 512, 256, 128, 128
OFF_GQKV, OFF_Z, OFF_XS, OFF_GZ, OFF_BC, OFF_CQ, OFF_CKV, OFF_KR, OFF_SM = (
    0, 2048, 3072, 4096, 5120, 5632, 6144, 6400, 6528)
N_PROJ = 6656
SM_DT, SM_B, SM_A = 0, 16, 24

NEG_BIG = -1e30


def _rms(x, g):
    return x * lax.rsqrt(jnp.mean(x * x, axis=-1, keepdims=True) + EPS) * g


def _sigmoid(x):
    return 1.0 / (1.0 + jnp.exp(-x))


def _silu(x):
    h = 0.5 * x
    return h + h * jnp.tanh(h)


def _softplus(x):
    return jnp.maximum(x, 0.0) + jnp.log1p(jnp.exp(-jnp.abs(x)))


def _dot(a, b):
    return jnp.dot(a, b, preferred_element_type=F32)


def _dot_nt(a, b):
    return lax.dot_general(a, b, (((1,), (1,)), ((), ())), preferred_element_type=F32)


def _params(*sem):
    return pltpu.CompilerParams(dimension_semantics=sem, vmem_limit_bytes=VMEM_LIMIT)


def _rope_kernel(pos_ref, inv_ref, cos_ref, sin_ref):
    ang = pos_ref[...].astype(F32) * inv_ref[...]
    cos_ref[...] = jnp.cos(ang)
    sin_ref[...] = jnp.sin(ang)


def _rope_tables(pos, inv4):
    R = pos.shape[0]
    tm = min(R, ROPE_ROWS)
    return pl.pallas_call(
        _rope_kernel,
        grid=(R // tm,),
        in_specs=[pl.BlockSpec((tm, 1), lambda i: (i, 0)),
                  pl.BlockSpec((1, LANES), lambda i: (0, 0))],
        out_specs=[pl.BlockSpec((tm, LANES), lambda i: (i, 0))] * 2,
        out_shape=[jax.ShapeDtypeStruct((R, LANES), F32)] * 2,
        compiler_params=_params("parallel"),
        name="rope_tables",
    )(pos, inv4)


def _inproj_kernel(x_ref, g_ref, w_ref, o_ref, xn_ref):
    @pl.when(pl.program_id(1) == 0)
    def _():
        xn_ref[...] = _rms(x_ref[...], g_ref[...]).astype(BF16)

    o_ref[...] = _dot(xn_ref[...], w_ref[...])


def _inproj(x2, g, w):
    T, D = x2.shape
    N = w.shape[1]
    tm = min(T, INPROJ_ROWS)
    tn = N // INPROJ_COL_STEPS
    return pl.pallas_call(
        _inproj_kernel,
        grid=(T // tm, N // tn),
        in_specs=[pl.BlockSpec((tm, D), lambda i, j: (i, 0)),
                  pl.BlockSpec((1, D), lambda i, j: (0, 0)),
                  pl.BlockSpec((D, tn), lambda i, j: (0, j))],
        out_specs=pl.BlockSpec((tm, tn), lambda i, j: (i, j)),
        out_shape=jax.ShapeDtypeStruct((T, N), F32),
        scratch_shapes=[pltpu.VMEM((tm, D), BF16)],
        compiler_params=_params("parallel", "arbitrary"),
        name="in_proj",
    )(x2, g, w)


def _shift_matrix():
    r = np.arange((CONV_K - 1) * CHUNK)[:, None]
    c = np.arange(HALO + CHUNK)[None, :]
    return jnp.asarray(c == HALO + (r % CHUNK) - (r // CHUNK + 1), BF16)


def _causal_conv(tail_ref, cur, w_ref, shift_ref):
    cur_b = cur.astype(BF16)
    delayed = _dot(shift_ref[...], jnp.concatenate([tail_ref[...], cur_b], axis=0))
    tail_ref[...] = cur_b[CHUNK - HALO:CHUNK, :]
    acc = w_ref[CONV_K - 1:CONV_K, :] * cur
    for j in range(1, CONV_K):
        acc = acc + w_ref[CONV_K - 1 - j:CONV_K - j, :] * delayed[(j - 1) * CHUNK:j * CHUNK]
    return acc


def _ssd_kernel(z_ref, xs_ref, bc_ref, sm_ref, cw_ref, cb_ref, shift_ref, dtb_ref,
                alog_ref, dsk_ref, ng_ref, e_ref, o_ref, tail_ref, h_ref):
    G, N, P = SSD_GROUPS, SSD_STATE, SSD_HEAD_DIM
    HG = SSD_HEADS // G
    GW = HG * P
    NP = SSD_HEADS // 2
    rows_ = range(z_ref.shape[0])
    gsl = [slice(g * GW, (g + 1) * GW) for g in range(G)]
    row_groups = [(r, g) for r in rows_ for g in range(G)]
    row_heads = [(r, e) for r in rows_ for e in range(SSD_HEADS)]
    row_pairs = [(r, p) for r in rows_ for p in range(NP)]

    @pl.when(pl.program_id(1) == 0)
    def _():
        tail_ref[...] = jnp.zeros(tail_ref.shape, tail_ref.dtype)
        h_ref[...] = jnp.zeros(h_ref.shape, F32)

    raw = [jnp.concatenate([xs_ref[r], bc_ref[r]], axis=1) for r in rows_]
    xbc = [_silu(_causal_conv(tail_ref.at[r], raw[r], cw_ref, shift_ref) + cb_ref[...]) for r in rows_]
    xs = [x[:, 0:SSD_INNER] for x in xbc]
    bc = [x[:, SSD_INNER:] for x in xbc]

    row = lax.broadcasted_iota(jnp.int32, (CHUNK, CHUNK), 0)
    col = lax.broadcasted_iota(jnp.int32, (CHUNK, CHUNK), 1)
    tril = row >= col
    tril_f = tril.astype(F32)
    lane = lax.broadcasted_iota(jnp.int32, (CHUNK, LANES), 1)
    left = lane < P

    neg_a = -jnp.exp(alog_ref[...])
    dt = [_softplus(sm_ref[r] + dtb_ref[...]) for r in rows_]
    a_cum = [jnp.dot(tril_f, d * neg_a, precision=HIGHEST, preferred_element_type=F32) for d in dt]
    a_cum_t = [a.T for a in a_cum]
    dt_t = [d.T for d in dt]
    stack = [jnp.concatenate([jnp.exp(a_cum[r]), dt[r] * jnp.exp(a_cum[r][CHUNK - 1:CHUNK, :] - a_cum[r])], axis=0)
             for r in rows_]
    hi = [x.astype(BF16) for x in stack]
    lo = [(x - h.astype(F32)).astype(BF16) for x, h in zip(stack, hi)]
    ex = [_dot(h, e_ref[...]) + _dot(l, e_ref[...]) for h, l in zip(hi, lo)]
    e_a_x = [x[0:CHUNK] for x in ex]
    sdec_x = [x[CHUNK:2 * CHUNK] for x in ex]

    b_g = [bc[r][:, g * N:(g + 1) * N] for r, g in row_groups]
    c_b = [bc[r][:, (G + g) * N:(G + g + 1) * N].astype(BF16) for r, g in row_groups]
    cb = [_dot_nt(c, b.astype(BF16)) for c, b in zip(c_b, b_g)]
    x_sd = [(xs[r][:, gsl[g]] * sdec_x[r][:, gsl[g]]).astype(BF16) for r, g in row_groups]
    st = [_dot(b.T.astype(BF16), x) for b, x in zip(b_g, x_sd)]
    h_prev = [h_ref[r, :, gsl[g]] for r, g in row_groups]
    y_off = [_dot(c_b[i], h_prev[i].astype(BF16)) * e_a_x[r][:, gsl[g]] for i, (r, g) in enumerate(row_groups)]
    for i, (r, g) in enumerate(row_groups):
        h_ref[r, :, gsl[g]] = h_prev[i] * e_a_x[r][CHUNK - 1:CHUNK, gsl[g]] + st[i]
    seg = [jnp.where(tril, a_cum[r][:, e:e + 1] - a_cum_t[r][e:e + 1, :], NEG_BIG) for r, e in row_heads]
    m = [cb[r * G + e // HG] * jnp.exp(seg[i]) * dt_t[r][e:e + 1, :] for i, (r, e) in enumerate(row_heads)]
    m2 = [jnp.concatenate([m[r * SSD_HEADS + 2 * p], m[r * SSD_HEADS + 2 * p + 1]], axis=1).astype(BF16)
          for r, p in row_pairs]
    xp = [xs[r][:, 2 * p * P:(2 * p + 2) * P] for r, p in row_pairs]
    bd = [jnp.concatenate([jnp.where(left, x, 0.0), jnp.where(left, 0.0, x)], axis=0).astype(BF16)
          for x in xp]
    diag = [_dot(a, b) for a, b in zip(m2, bd)]
    for r in rows_:
        y = (jnp.concatenate(diag[r * NP:(r + 1) * NP], axis=1) + jnp.concatenate(y_off[r * G:(r + 1) * G], axis=1)
             + dsk_ref[...] * xs[r])
        y = y * _silu(z_ref[r])
        o_ref[r] = jnp.concatenate([_rms(y[:, gs], ng_ref[:, gs]) for gs in gsl], axis=1).astype(o_ref.dtype)


SSD_BATCH_PER_STEP = 4


def _ssd(proj, B, S, cw, cb, shift, dtb, alog, dsk, ng, expand):
    nc = S // CHUNK
    nb = SSD_BATCH_PER_STEP
    assert S % CHUNK == 0 and B % nb == 0, (B, S)
    proj3 = proj.reshape(B, S, proj.shape[1])
    full = lambda a: pl.BlockSpec(a.shape, lambda b, c: (0,) * a.ndim)
    rows = lambda w, idx: pl.BlockSpec((nb, CHUNK, w), lambda b, c: (b, c, idx))
    consts = [cw, cb, shift, dtb, alog, dsk, ng, expand]
    out = pl.pallas_call(
        _ssd_kernel,
        grid=(B // nb, nc),
        in_specs=[rows(W_Z, OFF_Z // W_Z), rows(W_XS, OFF_XS // W_XS), rows(W_BC, OFF_BC // W_BC),
                  rows(W_SM, OFF_SM // W_SM)] + [full(a) for a in consts],
        out_specs=pl.BlockSpec((nb, CHUNK, SSD_INNER), lambda b, c: (b, c, 0)),
        out_shape=jax.ShapeDtypeStruct((B, S, SSD_INNER), BF16),
        scratch_shapes=[pltpu.VMEM((nb, HALO, W_XS + W_BC), BF16),
                        pltpu.VMEM((nb, SSD_STATE, SSD_INNER), F32)],
        compiler_params=_params("parallel", "arbitrary"),
        name="ssd_mixer",
    )(proj3, proj3, proj3, proj3, *consts)
    return out.reshape(B * S, SSD_INNER)


def _l2norm(x):
    return x * lax.rsqrt(jnp.sum(x * x, axis=-1, keepdims=True) + EPS)


def _split(x):
    hi = x.astype(BF16)
    return hi, (x - hi.astype(F32)).astype(BF16)


def _dot3(a, b):
    (ah, al), (bh, bl) = a, b
    return _dot(jnp.concatenate([ah, al, ah], axis=1), jnp.concatenate([bh, bh, bl], axis=0))


def _dot2(a, b):
    return _dot(jnp.concatenate(a, axis=1), jnp.concatenate([b, b], axis=0))


def _dot2r(a, b):
    return _dot(jnp.concatenate([a, a], axis=1), jnp.concatenate(b, axis=0))


SOLVE_BLOCK_LOG2 = 3


def _block_masks(row, col):
    n_levels = int(math.log2(CHUNK)) - SOLVE_BLOCK_LOG2
    diag = (row >> SOLVE_BLOCK_LOG2) == (col >> SOLVE_BLOCK_LOG2)
    merges = []
    for lv in range(n_levels):
        s = SOLVE_BLOCK_LOG2 + lv
        same = (row >> (s + 1)) == (col >> (s + 1))
        lower_left = jnp.where(same, ((row >> s) & 1) - ((col >> s) & 1), 0) == 1
        merges.append(lower_left)
    return diag, merges


def _unit_lower_inverses(mats, eye, diag_mask, merge_masks):
    def as_factor(mask):
        return jnp.where(mask, 1.0, 0.0).astype(BF16)

    eye_b = eye.astype(BF16)
    diag_b = as_factor(diag_mask)
    d1 = [a * diag_b for a in mats]
    d2 = [_split(_dot(d, d)) for d in d1]
    t0 = [eye_b - d for d in d1]
    ts = [t.astype(F32) + _dot2r(t, x) for t, x in zip(t0, d2)]
    d4 = [_split(_dot3(x, x)) for x in d2]
    ts = [t + _dot3(_split(t), x) for t, x in zip(ts, d4)]
    for lv, m in enumerate(merge_masks):
        n = 1 << (SOLVE_BLOCK_LOG2 + lv)
        pairs = range(CHUNK // (2 * n))

        def take(x):
            return jnp.concatenate([x[(2 * b + 1) * n:(2 * b + 2) * n] for b in pairs], axis=0)

        def put(x, part):
            pieces = []
            for b in pairs:
                pieces += [x[2 * b * n:(2 * b + 1) * n], part[b * n:(b + 1) * n]]
            return jnp.concatenate(pieces, axis=0)

        tsp = [_split(t) for t in ts]
        mb = as_factor(m)
        low = [take(t) for t in ts]
        mid = [_split(_dot2(_split(x), a * mb)) for x, a in zip(low, mats)]
        ts = [put(t, x - _dot3(md, tp)) for t, x, md, tp in zip(ts, low, mid, tsp)]
    return ts


def _gdn_kernel(qkv_ref, gz_ref, sm_ref, cw_ref, shift_ref, dtb_ref, alog_ref, ng_ref, o_ref, tail_ref, s_ref):
    HD = GDN_HEAD
    NH = GDN_V_HEADS
    NB = qkv_ref.shape[0]
    rows_ = range(NB)
    items = [(r, h) for r in rows_ for h in range(NH)]

    @pl.when(pl.program_id(1) == 0)
    def _():
        tail_ref[...] = jnp.zeros(tail_ref.shape, tail_ref.dtype)
        s_ref[...] = jnp.zeros(s_ref.shape, F32)

    qkv = [_silu(_causal_conv(tail_ref.at[r], qkv_ref[r], cw_ref, shift_ref)) for r in rows_]

    row = lax.broadcasted_iota(jnp.int32, (CHUNK, CHUNK), 0)
    col = lax.broadcasted_iota(jnp.int32, (CHUNK, CHUNK), 1)
    incl = row >= col
    strict = row > col
    eye = (row == col).astype(F32)
    diag_mask, merge_masks = _block_masks(row, col)
    upper = (row <= col).astype(F32)
    pad = jnp.zeros((CHUNK - 2 * NH, CHUNK), F32)

    sm_t = [sm_ref[r].T for r in rows_]
    beta_t = [_sigmoid(x[SM_B:SM_B + NH, :]) for x in sm_t]
    g_t = [-jnp.exp(alog_ref[...]) * _softplus(x[SM_A:SM_A + NH, :] + dtb_ref[...]) for x in sm_t]
    gc_t = [jnp.dot(x, upper, precision=HIGHEST, preferred_element_type=F32) for x in g_t]
    cols = [jnp.concatenate([beta_t[r], gc_t[r], pad], axis=0).T for r in rows_]

    q = [[_l2norm(qkv[r][:, j * HD:(j + 1) * HD]) * (HD ** -0.5) for j in range(GDN_QK_HEADS)] for r in rows_]
    k = [[_l2norm(qkv[r][:, GDN_QK + j * HD:GDN_QK + (j + 1) * HD]) for j in range(GDN_QK_HEADS)] for r in rows_]
    k_t = [[x.T.astype(BF16) for x in k[r]] for r in rows_]
    qk_raw = [[_dot(a.astype(BF16), b) for a, b in zip(q[r], k_t[r])] for r in rows_]

    b_col = [cols[r][:, h:h + 1] for r, h in items]
    g_col = [cols[r][:, NH + h:NH + h + 1] for r, h in items]
    g_last = [gc_t[r][h:h + 1, CHUNK - 1:CHUNK] for r, h in items]
    decay = [jnp.exp(jnp.where(incl, g_col[i] - gc_t[r][h:h + 1, :], NEG_BIG)) for i, (r, h) in enumerate(items)]
    kb = [k[r][h // 2] * b_col[i] for i, (r, h) in enumerate(items)]
    a_mat = [jnp.where(strict, _dot(kb[i].astype(BF16), k_t[r][h // 2]) * decay[i], 0.0).astype(BF16)
             for i, (r, h) in enumerate(items)]
    t_inv = _unit_lower_inverses(a_mat, eye, diag_mask, merge_masks)

    e_g = [jnp.exp(x) for x in g_col]
    v = [qkv[r][:, 2 * GDN_QK + h * HD:2 * GDN_QK + (h + 1) * HD] for r, h in items]
    rhs = [jnp.concatenate([v[i] * b_col[i], kb[i] * e_g[i]], axis=1).astype(BF16) for i in range(len(items))]
    sol = [_dot2(_split(t_inv[i]), rhs[i]) for i in range(len(items))]
    state = [s_ref[i] for i in range(len(items))]
    ws = [_dot(jnp.concatenate([sol[i][:, HD:2 * HD], q[r][h // 2] * e_g[i]], axis=0).astype(BF16),
               state[i].astype(BF16)) for i, (r, h) in enumerate(items)]
    v_new_b = [(sol[i][:, 0:HD] - ws[i][0:CHUNK]).astype(BF16) for i in range(len(items))]
    o = [ws[i][CHUNK:2 * CHUNK] + _dot((qk_raw[r][h // 2] * decay[i]).astype(BF16), v_new_b[i])
         for i, (r, h) in enumerate(items)]
    k_dec_t = [(k[r][h // 2] * jnp.exp(g_last[i] - g_col[i])).T.astype(BF16) for i, (r, h) in enumerate(items)]
    for i in range(len(items)):
        s_ref[i] = state[i] * jnp.exp(g_last[i]) + _dot(k_dec_t[i], v_new_b[i])
    for i, (r, h) in enumerate(items):
        out = _rms(o[i], ng_ref[...]) * _silu(gz_ref[r, :, h * HD:(h + 1) * HD])
        o_ref[r, :, h * HD:(h + 1) * HD] = out.astype(o_ref.dtype)


GDN_BATCH_PER_STEP = 4


def _gdn(proj, B, S, cw, shift, dtb, alog, ng):
    nc = S // CHUNK
    nb = GDN_BATCH_PER_STEP
    assert S % CHUNK == 0 and B % nb == 0, (B, S)
    width = GDN_V_HEADS * GDN_HEAD
    proj3 = proj.reshape(B, S, proj.shape[1])
    full = lambda a: pl.BlockSpec(a.shape, lambda b, c: (0,) * a.ndim)
    rows = lambda w, idx: pl.BlockSpec((nb, CHUNK, w), lambda b, c: (b, c, idx))
    consts = [cw, shift, dtb, alog, ng]
    out = pl.pallas_call(
        _gdn_kernel,
        grid=(B // nb, nc),
        in_specs=[rows(W_GQKV, OFF_GQKV // W_GQKV), rows(W_GZ, OFF_GZ // W_GZ),
                  rows(W_SM, OFF_SM // W_SM)] + [full(a) for a in consts],
        out_specs=pl.BlockSpec((nb, CHUNK, width), lambda b, c: (b, c, 0)),
        out_shape=jax.ShapeDtypeStruct((B, S, width), BF16),
        scratch_shapes=[pltpu.VMEM((nb, HALO, GDN_QKV), BF16),
                        pltpu.VMEM((nb * GDN_V_HEADS, GDN_HEAD, GDN_HEAD), F32)],
        compiler_params=_params("parallel", "arbitrary"),
        name="gdn_mixer",
    )(proj3, proj3, proj3, *consts)
    return out.reshape(B * S, width)


ATTN_BLOCK = 512
LOG2E = math.log2(math.e)
BF16_SUBLANES = 16
V_ROWS = MLA_V + BF16_SUBLANES
HEADS_PER_STEP = 4


def _mla_prep_kernel(cq_ref, ckv_ref, kr_ref, cos_ref, sin_ref, qg_ref, kvg_ref, wq_ref, wkv_ref,
                     qt_ref, k_ref, vt_ref):
    H = MLA_HEADS
    tm = cq_ref.shape[0]
    scale = (MLA_NOPE + MLA_ROPE) ** -0.5 * LOG2E
    cos4 = cos_ref[...]
    sin4 = sin_ref[...]
    lane = lax.broadcasted_iota(jnp.int32, (tm, LANES), 1)
    left = lane < MLA_ROPE

    qm = _dot(_rms(cq_ref[...], qg_ref[...]).astype(BF16), wq_ref[...])
    kvm = _dot(_rms(ckv_ref[...], kvg_ref[...]).astype(BF16), wkv_ref[...])

    kr = kr_ref[...]
    k_rope = (kr * jnp.where(left, cos4, sin4)
              + pltpu.roll(kr, MLA_ROPE, axis=1) * jnp.where(left, sin4, cos4)).astype(BF16)

    pe_off = H * MLA_NOPE
    rot_off = pe_off + H * MLA_ROPE
    for h in range(H):
        jb = (h // 2) * LANES
        pe = qm[:, pe_off + jb:pe_off + jb + LANES]
        rot = qm[:, rot_off + jb:rot_off + jb + LANES]
        roped = pe * cos4 + rot * sin4
        mine = left if h % 2 == 0 else jnp.logical_not(left)
        q_nope = qm[:, h * MLA_NOPE:(h + 1) * MLA_NOPE]
        qt_ref[h, 0, 0:LANES, :] = (q_nope * scale).T.astype(BF16)
        qt_ref[h, 0, LANES:2 * LANES, :] = (jnp.where(mine, roped, 0.0) * scale).T.astype(BF16)
        base = h * (MLA_NOPE + MLA_V)
        k_ref[h] = jnp.concatenate([kvm[:, base:base + MLA_NOPE].astype(BF16), k_rope], axis=1)
        vt_ref[h, 0, 0:MLA_V, :] = kvm[:, base + MLA_NOPE:base + MLA_NOPE + MLA_V].T.astype(BF16)
        ones_row = lax.broadcasted_iota(jnp.int32, (BF16_SUBLANES, tm), 0) == 0
        vt_ref[h, 0, MLA_V:V_ROWS, :] = jnp.where(ones_row, 1.0, 0.0).astype(BF16)


def _mla_prep(proj, cos4, sin4, qg, kvg, wq, wkv):
    T = proj.shape[0]
    H = MLA_HEADS
    tm = ATTN_BLOCK
    nt = T // tm
    full = lambda a: pl.BlockSpec(a.shape, lambda i: (0,) * a.ndim)
    rows = lambda w, idx: pl.BlockSpec((tm, w), lambda i: (i, idx))
    consts = [qg, kvg, wq, wkv]
    dq = 2 * LANES
    return pl.pallas_call(
        _mla_prep_kernel,
        grid=(nt,),
        in_specs=[rows(W_CQ, OFF_CQ // W_CQ), rows(W_CKV, OFF_CKV // W_CKV), rows(W_KR, OFF_KR // W_KR),
                  rows(LANES, 0), rows(LANES, 0)] + [full(a) for a in consts],
        out_specs=[pl.BlockSpec((H, 1, dq, tm), lambda i: (0, i, 0, 0)),
                   pl.BlockSpec((H, tm, dq), lambda i: (0, i, 0)),
                   pl.BlockSpec((H, 1, V_ROWS, tm), lambda i: (0, i, 0, 0))],
        out_shape=[jax.ShapeDtypeStruct((H, nt, dq, tm), BF16),
                   jax.ShapeDtypeStruct((H, T, dq), BF16),
                   jax.ShapeDtypeStruct((H, nt, V_ROWS, tm), BF16)],
        compiler_params=_params("parallel"),
        name="mla_prep",
    )(proj, proj, proj, cos4, sin4, *consts)


def _attn_kernel(qt_ref, k_ref, vt_ref, o_ref, m_ref, acc_ref):
    tb = ATTN_BLOCK
    hs = range(HEADS_PER_STEP)
    qi = pl.program_id(2)
    m_ref[...] = jnp.full(m_ref.shape, NEG_BIG, F32)
    acc_ref[...] = jnp.zeros(acc_ref.shape, F32)

    def update(kis, last_is_diagonal):
        nb = range(len(kis))
        s = [[_dot(k_ref[h, pl.ds(pl.multiple_of(ki * tb, tb), tb), :], qt_ref[h, 0]) for ki in kis]
             for h in hs]
        if last_is_diagonal:
            key = lax.broadcasted_iota(jnp.int32, (tb, tb), 0)
            qry = lax.broadcasted_iota(jnp.int32, (tb, tb), 1)
            for h in hs:
                s[h][-1] = jnp.where(key <= qry, s[h][-1], NEG_BIG)
        m_old = [m_ref[h] for h in hs]
        m_new = list(m_old)
        for h in hs:
            for j in nb:
                m_new[h] = jnp.maximum(m_new[h], jnp.max(s[h][j], axis=0, keepdims=True))
        p = [[jnp.exp2(s[h][j] - m_new[h]).astype(BF16) for j in nb] for h in hs]
        pv = [[_dot(vt_ref[h, kis[j]], p[h][j]) for j in nb] for h in hs]
        for h in hs:
            acc_ref[h] = jnp.exp2(m_old[h] - m_new[h]) * acc_ref[h] + sum(pv[h][1:], pv[h][0])
            m_ref[h] = m_new[h]

    def pair(i, carry):
        update([2 * i, 2 * i + 1], False)
        return carry

    lax.fori_loop(0, lax.shift_right_logical(qi, 1), pair, 0)
    odd = (qi & 1) == 1

    @pl.when(odd)
    def _():
        update([qi - 1, qi], True)

    @pl.when(jnp.logical_not(odd))
    def _():
        update([qi], True)

    for h in hs:
        acc = acc_ref[h]
        o_ref[:, h * MLA_V:(h + 1) * MLA_V] = (acc[0:MLA_V] / acc[MLA_V:MLA_V + 1]).T.astype(o_ref.dtype)


def _attention(q_t, k, v_t, B, S):
    H, T, dq = k.shape
    tb = ATTN_BLOCK
    nq = S // tb
    hp = HEADS_PER_STEP
    assert S % tb == 0 and H % hp == 0, (S, H)
    return pl.pallas_call(
        _attn_kernel,
        grid=(B, H // hp, nq),
        in_specs=[pl.BlockSpec((hp, 1, dq, tb), lambda b, h, i: (h, b * nq + i, 0, 0)),
                  pl.BlockSpec((hp, S, dq), lambda b, h, i: (h, b, 0)),
                  pl.BlockSpec((hp, nq, V_ROWS, tb), lambda b, h, i: (h, b, 0, 0))],
        out_specs=pl.BlockSpec((tb, hp * MLA_V), lambda b, h, i: (b * nq + i, h)),
        out_shape=jax.ShapeDtypeStruct((T, H * MLA_V), BF16),
        scratch_shapes=[pltpu.VMEM((hp, 1, tb), F32), pltpu.VMEM((hp, V_ROWS, tb), F32)],
        compiler_params=_params("parallel", "parallel", "arbitrary"),
        name="mla_attention",
    )(q_t, k, v_t)


def _merge_kernel(x_ref, ys_ref, ym_ref, yg_ref, g_ref, wgate_ref, ws_ref, wm_ref, wg_ref, wo_ref, h_ref):
    D = D_MODEL
    x = x_ref[...]
    xn = _rms(x, g_ref[...]).astype(BF16)
    mixed = None
    for idx, (y_ref, w_ref) in enumerate(((ys_ref, ws_ref), (ym_ref, wm_ref), (yg_ref, wg_ref))):
        gate = _sigmoid(_dot(xn, wgate_ref[:, idx * D:(idx + 1) * D]))
        term = gate * _dot(y_ref[...], w_ref[...])
        mixed = term if mixed is None else mixed + term
    h_ref[...] = x + _dot(mixed.astype(BF16), wo_ref[...])


def _merge(x2, ys, ym, yg, g, wgate, ws, wm, wg, wo):
    T, D = x2.shape
    tm = min(T, MERGE_ROWS)
    full = lambda a: pl.BlockSpec(a.shape, lambda i: (0,) * a.ndim)
    rows = pl.BlockSpec((tm, D), lambda i: (i, 0))
    consts = [g, wgate, ws, wm, wg, wo]
    return pl.pallas_call(
        _merge_kernel,
        grid=(T // tm,),
        in_specs=[rows] * 4 + [full(a) for a in consts],
        out_specs=rows,
        out_shape=jax.ShapeDtypeStruct((T, D), F32),
        compiler_params=_params("parallel"),
        name="merge",
    )(x2, ys, ym, yg, *consts)


def _ffn_kernel(h_ref, g_ref, wu_ref, wd_ref, fg_ref, o_ref, hn_ref, acc_ref, *, final_norm):
    j = pl.program_id(1)

    @pl.when(j == 0)
    def _():
        hn_ref[...] = _rms(h_ref[...], g_ref[...]).astype(BF16)
        acc_ref[...] = jnp.zeros(acc_ref.shape, F32)

    up = jnp.maximum(_dot(hn_ref[...], wu_ref[...]), 0.0)
    acc_ref[...] += _dot((up * up).astype(BF16), wd_ref[...])

    @pl.when(j == pl.num_programs(1) - 1)
    def _():
        out = h_ref[...] + acc_ref[...]
        if final_norm:
            out = _rms(out, fg_ref[...])
        o_ref[...] = out


def _ffn(h, g, wu, wd, fg, final_norm):
    T, D = h.shape
    F = wu.shape[1]
    tm = min(T, FFN_ROWS)
    tf = min(F, FFN_HIDDEN_TILE)
    return pl.pallas_call(
        functools.partial(_ffn_kernel, final_norm=final_norm),
        grid=(T // tm, F // tf),
        in_specs=[pl.BlockSpec((tm, D), lambda i, j: (i, 0)),
                  pl.BlockSpec((1, D), lambda i, j: (0, 0)),
                  pl.BlockSpec((D, tf), lambda i, j: (0, j)),
                  pl.BlockSpec((tf, D), lambda i, j: (j, 0)),
                  pl.BlockSpec((1, D), lambda i, j: (0, 0))],
        out_specs=pl.BlockSpec((tm, D), lambda i, j: (i, 0)),
        out_shape=jax.ShapeDtypeStruct((T, D), F32),
        scratch_shapes=[pltpu.VMEM((tm, D), BF16), pltpu.VMEM((tm, D), F32)],
        compiler_params=_params("parallel", "arbitrary"),
        name="ffn",
    )(h, g, wu, wd, fg)


def _rotate_half_cols(w):
    half = w.shape[-1] // 2
    return jnp.concatenate([-w[..., half:], w[..., :half]], axis=-1)


def _split_in_proj(w_in):
    sizes = (SSD_INNER, SSD_INNER, 2 * SSD_GROUPS * SSD_STATE, SSD_HEADS, MLA_Q_LORA, MLA_KV_LORA, MLA_ROPE,
             GDN_QKV, GDN_V_HEADS * GDN_HEAD, GDN_V_HEADS, GDN_V_HEADS, 3 * D_MODEL)
    offs = np.cumsum((0,) + sizes)
    return [w_in[:, int(offs[i]):int(offs[i + 1])] for i in range(len(sizes))]


def _layer_weights(w_in, w_uq):
    z, xs, bc, dt, cq, ckv, kr, gqkv, gz, gb, ga, gates = _split_in_proj(w_in)
    small = jnp.concatenate([dt, gb, ga, jnp.zeros((D_MODEL, W_SM - 32), w_in.dtype)], axis=1)
    w_a = jnp.concatenate([gqkv, z, xs, gz, bc, cq, ckv, kr, _rotate_half_cols(kr), small], axis=1)
    H = MLA_HEADS
    wq = w_uq.reshape(MLA_Q_LORA, H, MLA_NOPE + MLA_ROPE)
    nope = wq[:, :, :MLA_NOPE].reshape(MLA_Q_LORA, H * MLA_NOPE)
    pe = wq[:, :, MLA_NOPE:]
    w_q = jnp.concatenate([nope, pe.reshape(MLA_Q_LORA, H * MLA_ROPE),
                           _rotate_half_cols(pe).reshape(MLA_Q_LORA, H * MLA_ROPE)], axis=1)
    return w_a, gates, w_q


def _pad_lanes(v, n=LANES):
    return jnp.concatenate([v, jnp.zeros((n - v.shape[0],), v.dtype)])[None, :]


def kernel(x, positions, norm1_g, w_in, ssd_conv_w, ssd_conv_b, ssd_dt_bias, ssd_a_log, ssd_d, ssd_norm_g,
           mla_q_norm_g, mla_w_uq, mla_kv_norm_g, mla_w_ukv, gdn_conv_w, gdn_dt_bias, gdn_a_log, gdn_norm_g,
           w_ssd_out, w_mla_out, w_gdn_out, w_out, norm2_g, w_up, w_down, final_norm_g):
    B, S, D = x.shape
    T = B * S
    x2 = x.reshape(T, D)

    inv = ROPE_THETA ** (-jnp.arange(0, MLA_ROPE, 2, dtype=F32) / MLA_ROPE)
    inv4 = jnp.tile(inv, LANES // inv.shape[0])[None, :]
    cos4, sin4 = _rope_tables(positions.reshape(T, 1), inv4)

    shift = _shift_matrix()
    expand = (jnp.arange(LANES)[:, None] == (jnp.arange(SSD_INNER)[None, :] // SSD_HEAD_DIM)).astype(BF16)

    for l in range(DEPTH):
        w_a, w_gate, w_q = _layer_weights(w_in[l].astype(BF16), mla_w_uq[l].astype(BF16))
        proj = _inproj(x2, norm1_g[l][None, :], w_a)

        y_ssd = _ssd(proj, B, S, ssd_conv_w[l], ssd_conv_b[l][None, :], shift,
                     _pad_lanes(ssd_dt_bias[l]), _pad_lanes(ssd_a_log[l]),
                     jnp.repeat(ssd_d[l], SSD_HEAD_DIM)[None, :], ssd_norm_g[l][None, :], expand)

        q, k, v = _mla_prep(proj, cos4, sin4, mla_q_norm_g[l][None, :], mla_kv_norm_g[l][None, :],
                            w_q, mla_w_ukv[l].astype(BF16))
        y_mla = _attention(q, k, v, B, S)

        y_gdn = _gdn(proj, B, S, gdn_conv_w[l], shift,
                     jnp.broadcast_to(gdn_dt_bias[l][:, None], (GDN_V_HEADS, CHUNK)),
                     jnp.broadcast_to(gdn_a_log[l][:, None], (GDN_V_HEADS, CHUNK)),
                     gdn_norm_g[l][None, :])

        h = _merge(x2, y_ssd, y_mla, y_gdn, norm1_g[l][None, :], w_gate,
                   w_ssd_out[l].astype(BF16), w_mla_out[l].astype(BF16), w_gdn_out[l].astype(BF16),
                   w_out[l].astype(BF16))
        x2 = _ffn(h, norm2_g[l][None, :], w_up[l].astype(BF16), w_down[l].astype(BF16),
                  final_norm_g[None, :], final_norm=(l == DEPTH - 1))
    return x2.reshape(B, S, D)
```

```python
import functools
import math

import jax
import jax.numpy as jnp
import numpy as np
from jax import lax
from jax.experimental import pallas as pl
from jax.experimental.pallas import tpu as pltpu

F32 = jnp.float32
BF16 = jnp.bfloat16
HIGHEST = lax.Precision.HIGHEST

EPS = 1e-6
D_MODEL = 1024
DEPTH = 2
SSD_HEADS = 16
SSD_HEAD_DIM = 64
SSD_GROUPS = 2
SSD_STATE = 128
SSD_INNER = 1024
MLA_HEADS = 8
MLA_NOPE = 128
MLA_ROPE = 64
MLA_V = 128
MLA_Q_LORA = 512
MLA_KV_LORA = 256
ROPE_THETA = 10000.0
GDN_HEAD = 128
GDN_V_HEADS = 8
GDN_QK_HEADS = 4
GDN_QK = GDN_QK_HEADS * GDN_HEAD
GDN_QKV = 2 * GDN_QK + GDN_V_HEADS * GDN_HEAD
D_FF = 4096
CONV_K = 4

CHUNK = 128
HALO = 16
LANES = 128
VMEM_LIMIT = 56 * 1024 * 1024

ROPE_ROWS = 2048
INPROJ_ROWS = 1024
INPROJ_COL_STEPS = 4
MERGE_ROWS = 512
FFN_ROWS = 1024
FFN_HIDDEN_TILE = 2048

W_GQKV, W_Z, W_XS, W_GZ, W_BC, W_CQ, W_CKV, W_KR, W_SM = 2048, 1024, 1024, 1024, 512, 512, 256, 128, 128
OFF_GQKV, OFF_Z, OFF_XS, OFF_GZ, OFF_BC, OFF_CQ, OFF_CKV, OFF_KR, OFF_SM = (
    0, 2048, 3072, 4096, 5120, 5632, 6144, 6400, 6528)
N_PROJ = 6656
SM_DT, SM_B, SM_A = 0, 16, 24

NEG_BIG = -1e30


def _rms(x, g):
    return x * lax.rsqrt(jnp.mean(x * x, axis=-1, keepdims=True) + EPS) * g


def _sigmoid(x):
    return 1.0 / (1.0 + jnp.exp(-x))


def _silu(x):
    h = 0.5 * x
    return h + h * jnp.tanh(h)


def _softplus(x):
    return jnp.maximum(x, 0.0) + jnp.log1p(jnp.exp(-jnp.abs(x)))


def _dot(a, b):
    return jnp.dot(a, b, preferred_element_type=F32)


def _dot_nt(a, b):
    return lax.dot_general(a, b, (((1,), (1,)), ((), ())), preferred_element_type=F32)


def _params(*sem):
    return pltpu.CompilerParams(dimension_semantics=sem, vmem_limit_bytes=VMEM_LIMIT)


def _rope_kernel(pos_ref, inv_ref, cos_ref, sin_ref):
    ang = pos_ref[...].astype(F32) * inv_ref[...]
    cos_ref[...] = jnp.cos(ang)
    sin_ref[...] = jnp.sin(ang)


def _rope_tables(pos, inv4):
    R = pos.shape[0]
    tm = min(R, ROPE_ROWS)
    return pl.pallas_call(
        _rope_kernel,
        grid=(R // tm,),
        in_specs=[pl.BlockSpec((tm, 1), lambda i: (i, 0)),
                  pl.BlockSpec((1, LANES), lambda i: (0, 0))],
        out_specs=[pl.BlockSpec((tm, LANES), lambda i: (i, 0))] * 2,
        out_shape=[jax.ShapeDtypeStruct((R, LANES), F32)] * 2,
        compiler_params=_params("parallel"),
        name="rope_tables",
    )(pos, inv4)


def _inproj_kernel(x_ref, g_ref, w_ref, o_ref, xn_ref):
    @pl.when(pl.program_id(1) == 0)
    def _():
        xn_ref[...] = _rms(x_ref[...], g_ref[...]).astype(BF16)

    o_ref[...] = _dot(xn_ref[...], w_ref[...])


def _inproj(x2, g, w):
    T, D = x2.shape
    N = w.shape[1]
    tm = min(T, INPROJ_ROWS)
    tn = N // INPROJ_COL_STEPS
    return pl.pallas_call(
        _inproj_kernel,
        grid=(T // tm, N // tn),
        in_specs=[pl.BlockSpec((tm, D), lambda i, j: (i, 0)),
                  pl.BlockSpec((1, D), lambda i, j: (0, 0)),
                  pl.BlockSpec((D, tn), lambda i, j: (0, j))],
        out_specs=pl.BlockSpec((tm, tn), lambda i, j: (i, j)),
        out_shape=jax.ShapeDtypeStruct((T, N), F32),
        scratch_shapes=[pltpu.VMEM((tm, D), BF16)],
        compiler_params=_params("parallel", "arbitrary"),
        name="in_proj",
    )(x2, g, w)


def _shift_matrix():
    r = np.arange((CONV_K - 1) * CHUNK)[:, None]
    c = np.arange(HALO + CHUNK)[None, :]
    return jnp.asarray(c == HALO + (r % CHUNK) - (r // CHUNK + 1), BF16)


def _causal_conv(tail_ref, cur, w_ref, shift_ref):
    cur_b = cur.astype(BF16)
    delayed = _dot(shift_ref[...], jnp.concatenate([tail_ref[...], cur_b], axis=0))
    tail_ref[...] = cur_b[CHUNK - HALO:CHUNK, :]
    acc = w_ref[CONV_K - 1:CONV_K, :] * cur
    for j in range(1, CONV_K):
        acc = acc + w_ref[CONV_K - 1 - j:CONV_K - j, :] * delayed[(j - 1) * CHUNK:j * CHUNK]
    return acc


def _ssd_kernel(z_ref, xs_ref, bc_ref, sm_ref, cw_ref, cb_ref, shift_ref, dtb_ref,
                alog_ref, dsk_ref, ng_ref, e_ref, o_ref, tail_ref, h_ref):
    G, N, P = SSD_GROUPS, SSD_STATE, SSD_HEAD_DIM
    HG = SSD_HEADS // G
    GW = HG * P
    NP = SSD_HEADS // 2
    rows_ = range(z_ref.shape[0])
    gsl = [slice(g * GW, (g + 1) * GW) for g in range(G)]
    row_groups = [(r, g) for r in rows_ for g in range(G)]
    row_heads = [(r, e) for r in rows_ for e in range(SSD_HEADS)]
    row_pairs = [(r, p) for r in rows_ for p in range(NP)]

    @pl.when(pl.program_id(1) == 0)
    def _():
        tail_ref[...] = jnp.zeros(tail_ref.shape, tail_ref.dtype)
        h_ref[...] = jnp.zeros(h_ref.shape, F32)

    raw = [jnp.concatenate([xs_ref[r], bc_ref[r]], axis=1) for r in rows_]
    xbc = [_silu(_causal_conv(tail_ref.at[r], raw[r], cw_ref, shift_ref) + cb_ref[...]) for r in rows_]
    xs = [x[:, 0:SSD_INNER] for x in xbc]
    bc = [x[:, SSD_INNER:] for x in xbc]

    row = lax.broadcasted_iota(jnp.int32, (CHUNK, CHUNK), 0)
    col = lax.broadcasted_iota(jnp.int32, (CHUNK, CHUNK), 1)
    tril = row >= col
    tril_f = tril.astype(F32)
    lane = lax.broadcasted_iota(jnp.int32, (CHUNK, LANES), 1)
    left = lane < P

    neg_a = -jnp.exp(alog_ref[...])
    dt = [_softplus(sm_ref[r] + dtb_ref[...]) for r in rows_]
    a_cum = [jnp.dot(tril_f, d * neg_a, precision=HIGHEST, preferred_element_type=F32) for d in dt]
    a_cum_t = [a.T for a in a_cum]
    dt_t = [d.T for d in dt]
    stack = [jnp.concatenate([jnp.exp(a_cum[r]), dt[r] * jnp.exp(a_cum[r][CHUNK - 1:CHUNK, :] - a_cum[r])], axis=0)
             for r in rows_]
    hi = [x.astype(BF16) for x in stack]
    lo = [(x - h.astype(F32)).astype(BF16) for x, h in zip(stack, hi)]
    ex = [_dot2((h, l), e_ref[...]) for h, l in zip(hi, lo)]
    e_a_x = [x[0:CHUNK] for x in ex]
    sdec_x = [x[CHUNK:2 * CHUNK] for x in ex]

    b_g = [bc[r][:, g * N:(g + 1) * N] for r, g in row_groups]
    c_b = [bc[r][:, (G + g) * N:(G + g + 1) * N].astype(BF16) for r, g in row_groups]
    cb = [_dot_nt(c, b.astype(BF16)) for c, b in zip(c_b, b_g)]
    x_sd = [(xs[r][:, gsl[g]] * sdec_x[r][:, gsl[g]]).astype(BF16) for r, g in row_groups]
    st = [_dot(b.T.astype(BF16), x) for b, x in zip(b_g, x_sd)]
    h_prev = [h_ref[r, :, gsl[g]] for r, g in row_groups]
    y_off = [_dot(c_b[i], h_prev[i].astype(BF16)) * e_a_x[r][:, gsl[g]] for i, (r, g) in enumerate(row_groups)]
    for i, (r, g) in enumerate(row_groups):
        h_ref[r, :, gsl[g]] = h_prev[i] * e_a_x[r][CHUNK - 1:CHUNK, gsl[g]] + st[i]
    seg = [jnp.where(tril, a_cum[r][:, e:e + 1] - a_cum_t[r][e:e + 1, :], NEG_BIG) for r, e in row_heads]
    m = [cb[r * G + e // HG] * jnp.exp(seg[i]) * dt_t[r][e:e + 1, :] for i, (r, e) in enumerate(row_heads)]
    m2 = [jnp.concatenate([m[r * SSD_HEADS + 2 * p], m[r * SSD_HEADS + 2 * p + 1]], axis=1).astype(BF16)
          for r, p in row_pairs]
    xp = [xs[r][:, 2 * p * P:(2 * p + 2) * P] for r, p in row_pairs]
    bd = [jnp.concatenate([jnp.where(left, x, 0.0), jnp.where(left, 0.0, x)], axis=0).astype(BF16)
          for x in xp]
    diag = [_dot(a, b) for a, b in zip(m2, bd)]
    for r in rows_:
        y = (jnp.concatenate(diag[r * NP:(r + 1) * NP], axis=1) + jnp.concatenate(y_off[r * G:(r + 1) * G], axis=1)
             + dsk_ref[...] * xs[r])
        y = y * _silu(z_ref[r])
        o_ref[r] = jnp.concatenate([_rms(y[:, gs], ng_ref[:, gs]) for gs in gsl], axis=1).astype(o_ref.dtype)


SSD_BATCH_PER_STEP = 4


def _ssd(proj, B, S, cw, cb, shift, dtb, alog, dsk, ng, expand):
    nc = S // CHUNK
    nb = SSD_BATCH_PER_STEP
    assert S % CHUNK == 0 and B % nb == 0, (B, S)
    proj3 = proj.reshape(B, S, proj.shape[1])
    full = lambda a: pl.BlockSpec(a.shape, lambda b, c: (0,) * a.ndim)
    rows = lambda w, idx: pl.BlockSpec((nb, CHUNK, w), lambda b, c: (b, c, idx))
    consts = [cw, cb, shift, dtb, alog, dsk, ng, expand]
    out = pl.pallas_call(
        _ssd_kernel,
        grid=(B // nb, nc),
        in_specs=[rows(W_Z, OFF_Z // W_Z), rows(W_XS, OFF_XS // W_XS), rows(W_BC, OFF_BC // W_BC),
                  rows(W_SM, OFF_SM // W_SM)] + [full(a) for a in consts],
        out_specs=pl.BlockSpec((nb, CHUNK, SSD_INNER), lambda b, c: (b, c, 0)),
        out_shape=jax.ShapeDtypeStruct((B, S, SSD_INNER), BF16),
        scratch_shapes=[pltpu.VMEM((nb, HALO, W_XS + W_BC), BF16),
                        pltpu.VMEM((nb, SSD_STATE, SSD_INNER), F32)],
        compiler_params=_params("parallel", "arbitrary"),
        name="ssd_mixer",
    )(proj3, proj3, proj3, proj3, *consts)
    return out.reshape(B * S, SSD_INNER)


def _l2norm(x):
    return x * lax.rsqrt(jnp.sum(x * x, axis=-1, keepdims=True) + EPS)


def _split(x):
    hi = x.astype(BF16)
    return hi, (x - hi.astype(F32)).astype(BF16)


def _dot3(a, b):
    (ah, al), (bh, bl) = a, b
    return _dot(jnp.concatenate([ah, al, ah], axis=1), jnp.concatenate([bh, bh, bl], axis=0))


def _dot2(a, b):
    return _dot(jnp.concatenate(a, axis=1), jnp.concatenate([b, b], axis=0))


def _dot2r(a, b):
    return _dot(jnp.concatenate([a, a], axis=1), jnp.concatenate(b, axis=0))


SOLVE_BLOCK_LOG2 = 3


def _block_masks(row, col):
    n_levels = int(math.log2(CHUNK)) - SOLVE_BLOCK_LOG2
    diag = (row >> SOLVE_BLOCK_LOG2) == (col >> SOLVE_BLOCK_LOG2)
    merges = []
    for lv in range(n_levels):
        s = SOLVE_BLOCK_LOG2 + lv
        same = (row >> (s + 1)) == (col >> (s + 1))
        lower_left = jnp.where(same, ((row >> s) & 1) - ((col >> s) & 1), 0) == 1
        merges.append(lower_left)
    return diag, merges


def _unit_lower_inverses(mats, eye, diag_mask, merge_masks):
    def as_factor(mask):
        return jnp.where(mask, 1.0, 0.0).astype(BF16)

    eye_b = eye.astype(BF16)
    diag_b = as_factor(diag_mask)
    d1 = [a * diag_b for a in mats]
    d2 = [_split(_dot(d, d)) for d in d1]
    t0 = [eye_b - d for d in d1]
    ts = [t.astype(F32) + _dot2r(t, x) for t, x in zip(t0, d2)]
    d4 = [_split(_dot3(x, x)) for x in d2]
    ts = [t + _dot3(_split(t), x) for t, x in zip(ts, d4)]
    for lv, m in enumerate(merge_masks):
        n = 1 << (SOLVE_BLOCK_LOG2 + lv)
        pairs = range(CHUNK // (2 * n))

        def take(x):
            return jnp.concatenate([x[(2 * b + 1) * n:(2 * b + 2) * n] for b in pairs], axis=0)

        def put(x, part):
            pieces = []
            for b in pairs:
                pieces += [x[2 * b * n:(2 * b + 1) * n], part[b * n:(b + 1) * n]]
            return jnp.concatenate(pieces, axis=0)

        tsp = [_split(t) for t in ts]
        mb = as_factor(m)
        low = [take(t) for t in ts]
        mid = [_split(_dot2(_split(x), a * mb)) for x, a in zip(low, mats)]
        ts = [put(t, x - _dot3(md, tp)) for t, x, md, tp in zip(ts, low, mid, tsp)]
    return ts


def _gdn_kernel(qkv_ref, gz_ref, sm_ref, cw_ref, shift_ref, dtb_ref, alog_ref, ng_ref, o_ref, tail_ref, s_ref):
    HD = GDN_HEAD
    NH = GDN_V_HEADS
    NB = qkv_ref.shape[0]
    rows_ = range(NB)
    items = [(r, h) for r in rows_ for h in range(NH)]

    @pl.when(pl.program_id(1) == 0)
    def _():
        tail_ref[...] = jnp.zeros(tail_ref.shape, tail_ref.dtype)
        s_ref[...] = jnp.zeros(s_ref.shape, F32)

    qkv = [_silu(_causal_conv(tail_ref.at[r], qkv_ref[r], cw_ref, shift_ref)) for r in rows_]

    row = lax.broadcasted_iota(jnp.int32, (CHUNK, CHUNK), 0)
    col = lax.broadcasted_iota(jnp.int32, (CHUNK, CHUNK), 1)
    incl = row >= col
    strict = row > col
    eye = (row == col).astype(F32)
    diag_mask, merge_masks = _block_masks(row, col)
    upper = (row <= col).astype(F32)
    pad = jnp.zeros((CHUNK - 2 * NH, CHUNK), F32)

    sm_t = [sm_ref[r].T for r in rows_]
    beta_t = [_sigmoid(x[SM_B:SM_B + NH, :]) for x in sm_t]
    g_t = [-jnp.exp(alog_ref[...]) * _softplus(x[SM_A:SM_A + NH, :] + dtb_ref[...]) for x in sm_t]
    gc_t = [jnp.dot(x, upper, precision=HIGHEST, preferred_element_type=F32) for x in g_t]
    cols = [jnp.concatenate([beta_t[r], gc_t[r], pad], axis=0).T for r in rows_]

    q = [[_l2norm(qkv[r][:, j * HD:(j + 1) * HD]) * (HD ** -0.5) for j in range(GDN_QK_HEADS)] for r in rows_]
    k = [[_l2norm(qkv[r][:, GDN_QK + j * HD:GDN_QK + (j + 1) * HD]) for j in range(GDN_QK_HEADS)] for r in rows_]
    k_t = [[x.T.astype(BF16) for x in k[r]] for r in rows_]
    qk_raw = [[_dot(a.astype(BF16), b) for a, b in zip(q[r], k_t[r])] for r in rows_]

    b_col = [cols[r][:, h:h + 1] for r, h in items]
    g_col = [cols[r][:, NH + h:NH + h + 1] for r, h in items]
    g_last = [gc_t[r][h:h + 1, CHUNK - 1:CHUNK] for r, h in items]
    decay = [jnp.exp(jnp.where(incl, g_col[i] - gc_t[r][h:h + 1, :], NEG_BIG)) for i, (r, h) in enumerate(items)]
    kb = [k[r][h // 2] * b_col[i] for i, (r, h) in enumerate(items)]
    a_mat = [jnp.where(strict, _dot(kb[i].astype(BF16), k_t[r][h // 2]) * decay[i], 0.0).astype(BF16)
             for i, (r, h) in enumerate(items)]
    t_inv = _unit_lower_inverses(a_mat, eye, diag_mask, merge_masks)

    e_g = [jnp.exp(x) for x in g_col]
    v = [qkv[r][:, 2 * GDN_QK + h * HD:2 * GDN_QK + (h + 1) * HD] for r, h in items]
    rhs = [jnp.concatenate([v[i] * b_col[i], kb[i] * e_g[i]], axis=1).astype(BF16) for i in range(len(items))]
    sol = [_dot2(_split(t_inv[i]), rhs[i]) for i in range(len(items))]
    state = [s_ref[i] for i in range(len(items))]
    ws = [_dot(jnp.concatenate([sol[i][:, HD:2 * HD], q[r][h // 2] * e_g[i]], axis=0).astype(BF16),
               state[i].astype(BF16)) for i, (r, h) in enumerate(items)]
    v_new_b = [(sol[i][:, 0:HD] - ws[i][0:CHUNK]).astype(BF16) for i in range(len(items))]
    o = [ws[i][CHUNK:2 * CHUNK] + _dot((qk_raw[r][h // 2] * decay[i]).astype(BF16), v_new_b[i])
         for i, (r, h) in enumerate(items)]
    k_dec_t = [(k[r][h // 2] * jnp.exp(g_last[i] - g_col[i])).T.astype(BF16) for i, (r, h) in enumerate(items)]
    for i in range(len(items)):
        s_ref[i] = state[i] * jnp.exp(g_last[i]) + _dot(k_dec_t[i], v_new_b[i])
    for i, (r, h) in enumerate(items):
        out = _rms(o[i], ng_ref[...]) * _silu(gz_ref[r, :, h * HD:(h + 1) * HD])
        o_ref[r, :, h * HD:(h + 1) * HD] = out.astype(o_ref.dtype)


GDN_BATCH_PER_STEP = 4


def _gdn(proj, B, S, cw, shift, dtb, alog, ng):
    nc = S // CHUNK
    nb = GDN_BATCH_PER_STEP
    assert S % CHUNK == 0 and B % nb == 0, (B, S)
    width = GDN_V_HEADS * GDN_HEAD
    proj3 = proj.reshape(B, S, proj.shape[1])
    full = lambda a: pl.BlockSpec(a.shape, lambda b, c: (0,) * a.ndim)
    rows = lambda w, idx: pl.BlockSpec((nb, CHUNK, w), lambda b, c: (b, c, idx))
    consts = [cw, shift, dtb, alog, ng]
    out = pl.pallas_call(
        _gdn_kernel,
        grid=(B // nb, nc),
        in_specs=[rows(W_GQKV, OFF_GQKV // W_GQKV), rows(W_GZ, OFF_GZ // W_GZ),
                  rows(W_SM, OFF_SM // W_SM)] + [full(a) for a in consts],
        out_specs=pl.BlockSpec((nb, CHUNK, width), lambda b, c: (b, c, 0)),
        out_shape=jax.ShapeDtypeStruct((B, S, width), BF16),
        scratch_shapes=[pltpu.VMEM((nb, HALO, GDN_QKV), BF16),
                        pltpu.VMEM((nb * GDN_V_HEADS, GDN_HEAD, GDN_HEAD), F32)],
        compiler_params=_params("parallel", "arbitrary"),
        name="gdn_mixer",
    )(proj3, proj3, proj3, *consts)
    return out.reshape(B * S, width)


ATTN_BLOCK = 512
LOG2E = math.log2(math.e)
BF16_SUBLANES = 16
V_ROWS = MLA_V + BF16_SUBLANES
HEADS_PER_STEP = 4


def _mla_prep_kernel(cq_ref, ckv_ref, kr_ref, cos_ref, sin_ref, qg_ref, kvg_ref, wq_ref, wkv_ref,
                     qt_ref, k_ref, vt_ref):
    H = MLA_HEADS
    tm = cq_ref.shape[0]
    scale = (MLA_NOPE + MLA_ROPE) ** -0.5 * LOG2E
    cos4 = cos_ref[...]
    sin4 = sin_ref[...]
    lane = lax.broadcasted_iota(jnp.int32, (tm, LANES), 1)
    left = lane < MLA_ROPE

    qm = _dot(_rms(cq_ref[...], qg_ref[...]).astype(BF16), wq_ref[...])
    kvm = _dot(_rms(ckv_ref[...], kvg_ref[...]).astype(BF16), wkv_ref[...])

    kr = kr_ref[...]
    k_rope = (kr * jnp.where(left, cos4, sin4)
              + pltpu.roll(kr, MLA_ROPE, axis=1) * jnp.where(left, sin4, cos4)).astype(BF16)

    pe_off = H * MLA_NOPE
    rot_off = pe_off + H * MLA_ROPE
    for h in range(H):
        jb = (h // 2) * LANES
        pe = qm[:, pe_off + jb:pe_off + jb + LANES]
        rot = qm[:, rot_off + jb:rot_off + jb + LANES]
        roped = pe * cos4 + rot * sin4
        mine = left if h % 2 == 0 else jnp.logical_not(left)
        q_nope = qm[:, h * MLA_NOPE:(h + 1) * MLA_NOPE]
        qt_ref[h, 0, 0:LANES, :] = (q_nope * scale).T.astype(BF16)
        qt_ref[h, 0, LANES:2 * LANES, :] = (jnp.where(mine, roped, 0.0) * scale).T.astype(BF16)
        base = h * (MLA_NOPE + MLA_V)
        k_ref[h] = jnp.concatenate([kvm[:, base:base + MLA_NOPE].astype(BF16), k_rope], axis=1)
        vt_ref[h, 0, 0:MLA_V, :] = kvm[:, base + MLA_NOPE:base + MLA_NOPE + MLA_V].T.astype(BF16)
        ones_row = lax.broadcasted_iota(jnp.int32, (BF16_SUBLANES, tm), 0) == 0
        vt_ref[h, 0, MLA_V:V_ROWS, :] = jnp.where(ones_row, 1.0, 0.0).astype(BF16)


def _mla_prep(proj, cos4, sin4, qg, kvg, wq, wkv):
    T = proj.shape[0]
    H = MLA_HEADS
    tm = ATTN_BLOCK
    nt = T // tm
    full = lambda a: pl.BlockSpec(a.shape, lambda i: (0,) * a.ndim)
    rows = lambda w, idx: pl.BlockSpec((tm, w), lambda i: (i, idx))
    consts = [qg, kvg, wq, wkv]
    dq = 2 * LANES
    return pl.pallas_call(
        _mla_prep_kernel,
        grid=(nt,),
        in_specs=[rows(W_CQ, OFF_CQ // W_CQ), rows(W_CKV, OFF_CKV // W_CKV), rows(W_KR, OFF_KR // W_KR),
                  rows(LANES, 0), rows(LANES, 0)] + [full(a) for a in consts],
        out_specs=[pl.BlockSpec((H, 1, dq, tm), lambda i: (0, i, 0, 0)),
                   pl.BlockSpec((H, tm, dq), lambda i: (0, i, 0)),
                   pl.BlockSpec((H, 1, V_ROWS, tm), lambda i: (0, i, 0, 0))],
        out_shape=[jax.ShapeDtypeStruct((H, nt, dq, tm), BF16),
                   jax.ShapeDtypeStruct((H, T, dq), BF16),
                   jax.ShapeDtypeStruct((H, nt, V_ROWS, tm), BF16)],
        compiler_params=_params("parallel"),
        name="mla_prep",
    )(proj, proj, proj, cos4, sin4, *consts)


def _attn_kernel(qt_ref, k_ref, vt_ref, o_ref, m_ref, acc_ref):
    tb = ATTN_BLOCK
    hs = range(HEADS_PER_STEP)
    qi = pl.program_id(2)
    m_ref[...] = jnp.full(m_ref.shape, NEG_BIG, F32)
    acc_ref[...] = jnp.zeros(acc_ref.shape, F32)

    def update(kis, last_is_diagonal):
        nb = range(len(kis))
        s = [[_dot(k_ref[h, pl.ds(pl.multiple_of(ki * tb, tb), tb), :], qt_ref[h, 0]) for ki in kis]
             for h in hs]
        if last_is_diagonal:
            key = lax.broadcasted_iota(jnp.int32, (tb, tb), 0)
            qry = lax.broadcasted_iota(jnp.int32, (tb, tb), 1)
            for h in hs:
                s[h][-1] = jnp.where(key <= qry, s[h][-1], NEG_BIG)
        m_old = [m_ref[h] for h in hs]
        m_new = list(m_old)
        for h in hs:
            for j in nb:
                m_new[h] = jnp.maximum(m_new[h], jnp.max(s[h][j], axis=0, keepdims=True))
        p = [[jnp.exp2(s[h][j] - m_new[h]).astype(BF16) for j in nb] for h in hs]
        pv = [[_dot(vt_ref[h, kis[j]], p[h][j]) for j in nb] for h in hs]
        for h in hs:
            acc_ref[h] = jnp.exp2(m_old[h] - m_new[h]) * acc_ref[h] + sum(pv[h][1:], pv[h][0])
            m_ref[h] = m_new[h]

    def pair(i, carry):
        update([2 * i, 2 * i + 1], False)
        return carry

    lax.fori_loop(0, lax.shift_right_logical(qi, 1), pair, 0)
    odd = (qi & 1) == 1

    @pl.when(odd)
    def _():
        update([qi - 1, qi], True)

    @pl.when(jnp.logical_not(odd))
    def _():
        update([qi], True)

    for h in hs:
        acc = acc_ref[h]
        o_ref[:, h * MLA_V:(h + 1) * MLA_V] = (acc[0:MLA_V] / acc[MLA_V:MLA_V + 1]).T.astype(o_ref.dtype)


def _attention(q_t, k, v_t, B, S):
    H, T, dq = k.shape
    tb = ATTN_BLOCK
    nq = S // tb
    hp = HEADS_PER_STEP
    assert S % tb == 0 and H % hp == 0, (S, H)
    return pl.pallas_call(
        _attn_kernel,
        grid=(B, H // hp, nq),
        in_specs=[pl.BlockSpec((hp, 1, dq, tb), lambda b, h, i: (h, b * nq + i, 0, 0)),
                  pl.BlockSpec((hp, S, dq), lambda b, h, i: (h, b, 0)),
                  pl.BlockSpec((hp, nq, V_ROWS, tb), lambda b, h, i: (h, b, 0, 0))],
        out_specs=pl.BlockSpec((tb, hp * MLA_V), lambda b, h, i: (b * nq + i, h)),
        out_shape=jax.ShapeDtypeStruct((T, H * MLA_V), BF16),
        scratch_shapes=[pltpu.VMEM((hp, 1, tb), F32), pltpu.VMEM((hp, V_ROWS, tb), F32)],
        compiler_params=_params("parallel", "parallel", "arbitrary"),
        name="mla_attention",
    )(q_t, k, v_t)


def _merge_kernel(x_ref, ys_ref, ym_ref, yg_ref, g_ref, wgate_ref, ws_ref, wm_ref, wg_ref, wo_ref, h_ref):
    D = D_MODEL
    x = x_ref[...]
    xn = _rms(x, g_ref[...]).astype(BF16)
    mixed = None
    for idx, (y_ref, w_ref) in enumerate(((ys_ref, ws_ref), (ym_ref, wm_ref), (yg_ref, wg_ref))):
        gate = _sigmoid(_dot(xn, wgate_ref[:, idx * D:(idx + 1) * D]))
        term = gate * _dot(y_ref[...], w_ref[...])
        mixed = term if mixed is None else mixed + term
    h_ref[...] = x + _dot(mixed.astype(BF16), wo_ref[...])


def _merge(x2, ys, ym, yg, g, wgate, ws, wm, wg, wo):
    T, D = x2.shape
    tm = min(T, MERGE_ROWS)
    full = lambda a: pl.BlockSpec(a.shape, lambda i: (0,) * a.ndim)
    rows = pl.BlockSpec((tm, D), lambda i: (i, 0))
    consts = [g, wgate, ws, wm, wg, wo]
    return pl.pallas_call(
        _merge_kernel,
        grid=(T // tm,),
        in_specs=[rows] * 4 + [full(a) for a in consts],
        out_specs=rows,
        out_shape=jax.ShapeDtypeStruct((T, D), F32),
        compiler_params=_params("parallel"),
        name="merge",
    )(x2, ys, ym, yg, *consts)


def _ffn_kernel(h_ref, g_ref, wu_ref, wd_ref, fg_ref, o_ref, hn_ref, acc_ref, *, final_norm):
    j = pl.program_id(1)

    @pl.when(j == 0)
    def _():
        hn_ref[...] = _rms(h_ref[...], g_ref[...]).astype(BF16)
        acc_ref[...] = jnp.zeros(acc_ref.shape, F32)

    up = jnp.maximum(_dot(hn_ref[...], wu_ref[...]), 0.0)
    acc_ref[...] += _dot((up * up).astype(BF16), wd_ref[...])

    @pl.when(j == pl.num_programs(1) - 1)
    def _():
        out = h_ref[...] + acc_ref[...]
        if final_norm:
            out = _rms(out, fg_ref[...])
        o_ref[...] = out


def _ffn(h, g, wu, wd, fg, final_norm):
    T, D = h.shape
    F = wu.shape[1]
    tm = min(T, FFN_ROWS)
    tf = min(F, FFN_HIDDEN_TILE)
    return pl.pallas_call(
        functools.partial(_ffn_kernel, final_norm=final_norm),
        grid=(T // tm, F // tf),
        in_specs=[pl.BlockSpec((tm, D), lambda i, j: (i, 0)),
                  pl.BlockSpec((1, D), lambda i, j: (0, 0)),
                  pl.BlockSpec((D, tf), lambda i, j: (0, j)),
                  pl.BlockSpec((tf, D), lambda i, j: (j, 0)),
                  pl.BlockSpec((1, D), lambda i, j: (0, 0))],
        out_specs=pl.BlockSpec((tm, D), lambda i, j: (i, 0)),
        out_shape=jax.ShapeDtypeStruct((T, D), F32),
        scratch_shapes=[pltpu.VMEM((tm, D), BF16), pltpu.VMEM((tm, D), F32)],
        compiler_params=_params("parallel", "arbitrary"),
        name="ffn",
    )(h, g, wu, wd, fg)


def _rotate_half_cols(w):
    half = w.shape[-1] // 2
    return jnp.concatenate([-w[..., half:], w[..., :half]], axis=-1)


def _split_in_proj(w_in):
    sizes = (SSD_INNER, SSD_INNER, 2 * SSD_GROUPS * SSD_STATE, SSD_HEADS, MLA_Q_LORA, MLA_KV_LORA, MLA_ROPE,
             GDN_QKV, GDN_V_HEADS * GDN_HEAD, GDN_V_HEADS, GDN_V_HEADS, 3 * D_MODEL)
    offs = np.cumsum((0,) + sizes)
    return [w_in[:, int(offs[i]):int(offs[i + 1])] for i in range(len(sizes))]


def _layer_weights(w_in, w_uq):
    z, xs, bc, dt, cq, ckv, kr, gqkv, gz, gb, ga, gates = _split_in_proj(w_in)
    small = jnp.concatenate([dt, gb, ga, jnp.zeros((D_MODEL, W_SM - 32), w_in.dtype)], axis=1)
    w_a = jnp.concatenate([gqkv, z, xs, gz, bc, cq, ckv, kr, _rotate_half_cols(kr), small], axis=1)
    H = MLA_HEADS
    wq = w_uq.reshape(MLA_Q_LORA, H, MLA_NOPE + MLA_ROPE)
    nope = wq[:, :, :MLA_NOPE].reshape(MLA_Q_LORA, H * MLA_NOPE)
    pe = wq[:, :, MLA_NOPE:]
    w_q = jnp.concatenate([nope, pe.reshape(MLA_Q_LORA, H * MLA_ROPE),
                           _rotate_half_cols(pe).reshape(MLA_Q_LORA, H * MLA_ROPE)], axis=1)
    return w_a, gates, w_q


def _pad_lanes(v, n=LANES):
    return jnp.concatenate([v, jnp.zeros((n - v.shape[0],), v.dtype)])[None, :]


def kernel(x, positions, norm1_g, w_in, ssd_conv_w, ssd_conv_b, ssd_dt_bias, ssd_a_log, ssd_d, ssd_norm_g,
           mla_q_norm_g, mla_w_uq, mla_kv_norm_g, mla_w_ukv, gdn_conv_w, gdn_dt_bias, gdn_a_log, gdn_norm_g,
           w_ssd_out, w_mla_out, w_gdn_out, w_out, norm2_g, w_up, w_down, final_norm_g):
    B, S, D = x.shape
    T = B * S
    x2 = x.reshape(T, D)

    inv = ROPE_THETA ** (-jnp.arange(0, MLA_ROPE, 2, dtype=F32) / MLA_ROPE)
    inv4 = jnp.tile(inv, LANES // inv.shape[0])[None, :]
    cos4, sin4 = _rope_tables(positions.reshape(T, 1), inv4)

    shift = _shift_matrix()
    expand = (jnp.arange(LANES)[:, None] == (jnp.arange(SSD_INNER)[None, :] // SSD_HEAD_DIM)).astype(BF16)

    for l in range(DEPTH):
        w_a, w_gate, w_q = _layer_weights(w_in[l].astype(BF16), mla_w_uq[l].astype(BF16))
        proj = _inproj(x2, norm1_g[l][None, :], w_a)

        y_ssd = _ssd(proj, B, S, ssd_conv_w[l], ssd_conv_b[l][None, :], shift,
                     _pad_lanes(ssd_dt_bias[l]), _pad_lanes(ssd_a_log[l]),
                     jnp.repeat(ssd_d[l], SSD_HEAD_DIM)[None, :], ssd_norm_g[l][None, :], expand)

        q, k, v = _mla_prep(proj, cos4, sin4, mla_q_norm_g[l][None, :], mla_kv_norm_g[l][None, :],
                            w_q, mla_w_ukv[l].astype(BF16))
        y_mla = _attention(q, k, v, B, S)

        y_gdn = _gdn(proj, B, S, gdn_conv_w[l], shift,
                     jnp.broadcast_to(gdn_dt_bias[l][:, None], (GDN_V_HEADS, CHUNK)),
                     jnp.broadcast_to(gdn_a_log[l][:, None], (GDN_V_HEADS, CHUNK)),
                     gdn_norm_g[l][None, :])

        h = _merge(x2, y_ssd, y_mla, y_gdn, norm1_g[l][None, :], w_gate,
                   w_ssd_out[l].astype(BF16), w_mla_out[l].astype(BF16), w_gdn_out[l].astype(BF16),
                   w_out[l].astype(BF16))
        x2 = _ffn(h, norm2_g[l][None, :], w_up[l].astype(BF16), w_down[l].astype(BF16),
                  final_norm_g[None, :], final_norm=(l == DEPTH - 1))
    return x2.reshape(B, S, D)
```

```python
import functools
import math

import jax
import jax.numpy as jnp
import numpy as np
from jax import lax
from jax.experimental import pallas as pl
from jax.experimental.pallas import tpu as pltpu

F32 = jnp.float32
BF16 = jnp.bfloat16
HIGHEST = lax.Precision.HIGHEST

EPS = 1e-6
D_MODEL = 1024
DEPTH = 2
SSD_HEADS = 16
SSD_HEAD_DIM = 64
SSD_GROUPS = 2
SSD_STATE = 128
SSD_INNER = 1024
MLA_HEADS = 8
MLA_NOPE = 128
MLA_ROPE = 64
MLA_V = 128
MLA_Q_LORA = 512
MLA_KV_LORA = 256
ROPE_THETA = 10000.0
GDN_HEAD = 128
GDN_V_HEADS = 8
GDN_QK_HEADS = 4
GDN_QK = GDN_QK_HEADS * GDN_HEAD
GDN_QKV = 2 * GDN_QK + GDN_V_HEADS * GDN_HEAD
D_FF = 4096
CONV_K = 4

CHUNK = 128
HALO = 16
LANES = 128
VMEM_LIMIT = 56 * 1024 * 1024

ROPE_ROWS = 2048
INPROJ_ROWS = 1024
INPROJ_COL_STEPS = 4
MERGE_ROWS = 512
FFN_ROWS = 1024
FFN_HIDDEN_TILE = 2048

W_GQKV, W_Z, W_XS, W_GZ, W_BC, W_CQ, W_CKV, W_KR, W_SM = 2048, 1024, 1024, 1024, 512, 512, 256, 128, 128
OFF_GQKV, OFF_Z, OFF_XS, OFF_GZ, OFF_BC, OFF_CQ, OFF_CKV, OFF_KR, OFF_SM = (
    0, 2048, 3072, 4096, 5120, 5632, 6144, 6400, 6528)
N_PROJ = 6656
SM_DT, SM_B, SM_A = 0, 16, 24

NEG_BIG = -1e30


def _rms(x, g):
    return x * lax.rsqrt(jnp.mean(x * x, axis=-1, keepdims=True) + EPS) * g


def _sigmoid(x):
    return 1.0 / (1.0 + jnp.exp(-x))


def _silu(x):
    h = 0.5 * x
    return h + h * jnp.tanh(h)


def _softplus(x):
    return jnp.maximum(x, 0.0) + jnp.log1p(jnp.exp(-jnp.abs(x)))


def _dot(a, b):
    return jnp.dot(a, b, preferred_element_type=F32)


def _dot_nt(a, b):
    return lax.dot_general(a, b, (((1,), (1,)), ((), ())), preferred_element_type=F32)


def _params(*sem):
    return pltpu.CompilerParams(dimension_semantics=sem, vmem_limit_bytes=VMEM_LIMIT)


def _rope_kernel(pos_ref, inv_ref, cos_ref, sin_ref):
    ang = pos_ref[...].astype(F32) * inv_ref[...]
    cos_ref[...] = jnp.cos(ang)
    sin_ref[...] = jnp.sin(ang)


def _rope_tables(pos, inv4):
    R = pos.shape[0]
    tm = min(R, ROPE_ROWS)
    return pl.pallas_call(
        _rope_kernel,
        grid=(R // tm,),
        in_specs=[pl.BlockSpec((tm, 1), lambda i: (i, 0)),
                  pl.BlockSpec((1, LANES), lambda i: (0, 0))],
        out_specs=[pl.BlockSpec((tm, LANES), lambda i: (i, 0))] * 2,
        out_shape=[jax.ShapeDtypeStruct((R, LANES), F32)] * 2,
        compiler_params=_params("parallel"),
        name="rope_tables",
    )(pos, inv4)


def _inproj_kernel(x_ref, g_ref, w_ref, o_ref, xn_ref):
    @pl.when(pl.program_id(1) == 0)
    def _():
        xn_ref[...] = _rms(x_ref[...], g_ref[...]).astype(BF16)

    o_ref[...] = _dot(xn_ref[...], w_ref[...])


def _inproj(x2, g, w):
    T, D = x2.shape
    N = w.shape[1]
    tm = min(T, INPROJ_ROWS)
    tn = N // INPROJ_COL_STEPS
    return pl.pallas_call(
        _inproj_kernel,
        grid=(T // tm, N // tn),
        in_specs=[pl.BlockSpec((tm, D), lambda i, j: (i, 0)),
                  pl.BlockSpec((1, D), lambda i, j: (0, 0)),
                  pl.BlockSpec((D, tn), lambda i, j: (0, j))],
        out_specs=pl.BlockSpec((tm, tn), lambda i, j: (i, j)),
        out_shape=jax.ShapeDtypeStruct((T, N), F32),
        scratch_shapes=[pltpu.VMEM((tm, D), BF16)],
        compiler_params=_params("parallel", "arbitrary"),
        name="in_proj",
    )(x2, g, w)


def _shift_matrix():
    r = np.arange((CONV_K - 1) * CHUNK)[:, None]
    c = np.arange(HALO + CHUNK)[None, :]
    return jnp.asarray(c == HALO + (r % CHUNK) - (r // CHUNK + 1), BF16)


def _causal_conv(tail_ref, cur, w_ref, shift_ref):
    cur_b = cur.astype(BF16)
    delayed = _dot(shift_ref[...], jnp.concatenate([tail_ref[...], cur_b], axis=0))
    tail_ref[...] = cur_b[CHUNK - HALO:CHUNK, :]
    acc = w_ref[CONV_K - 1:CONV_K, :] * cur
    for j in range(1, CONV_K):
        acc = acc + w_ref[CONV_K - 1 - j:CONV_K - j, :] * delayed[(j - 1) * CHUNK:j * CHUNK]
    return acc


def _ssd_kernel(z_ref, xs_ref, bc_ref, sm_ref, cw_ref, cb_ref, shift_ref, dtb_ref,
                alog_ref, dsk_ref, ng_ref, e_ref, o_ref, tail_ref, h_ref):
    G, N, P = SSD_GROUPS, SSD_STATE, SSD_HEAD_DIM
    HG = SSD_HEADS // G
    GW = HG * P
    NP = SSD_HEADS // 2
    rows_ = range(z_ref.shape[0])
    gsl = [slice(g * GW, (g + 1) * GW) for g in range(G)]
    row_groups = [(r, g) for r in rows_ for g in range(G)]
    row_heads = [(r, e) for r in rows_ for e in range(SSD_HEADS)]
    row_pairs = [(r, p) for r in rows_ for p in range(NP)]

    @pl.when(pl.program_id(1) == 0)
    def _():
        tail_ref[...] = jnp.zeros(tail_ref.shape, tail_ref.dtype)
        h_ref[...] = jnp.zeros(h_ref.shape, F32)

    raw = [jnp.concatenate([xs_ref[r], bc_ref[r]], axis=1) for r in rows_]
    xbc = [_silu(_causal_conv(tail_ref.at[r], raw[r], cw_ref, shift_ref) + cb_ref[...]) for r in rows_]
    xs = [x[:, 0:SSD_INNER] for x in xbc]
    bc = [x[:, SSD_INNER:] for x in xbc]

    row = lax.broadcasted_iota(jnp.int32, (CHUNK, CHUNK), 0)
    col = lax.broadcasted_iota(jnp.int32, (CHUNK, CHUNK), 1)
    tril = row >= col
    tril_f = tril.astype(F32)
    lane = lax.broadcasted_iota(jnp.int32, (CHUNK, LANES), 1)
    left = lane < P

    neg_a = -jnp.exp(alog_ref[...])
    dt = [_softplus(sm_ref[r] + dtb_ref[...]) for r in rows_]
    a_cum = [jnp.dot(tril_f, d * neg_a, precision=HIGHEST, preferred_element_type=F32) for d in dt]
    a_cum_t = [a.T for a in a_cum]
    dt_t = [d.T for d in dt]
    stack = [jnp.concatenate([jnp.exp(a_cum[r]), dt[r] * jnp.exp(a_cum[r][CHUNK - 1:CHUNK, :] - a_cum[r])], axis=0)
             for r in rows_]
    hi = [x.astype(BF16) for x in stack]
    lo = [(x - h.astype(F32)).astype(BF16) for x, h in zip(stack, hi)]
    ex = [_dot2((h, l), e_ref[...]) for h, l in zip(hi, lo)]
    e_a_x = [x[0:CHUNK] for x in ex]
    sdec_x = [x[CHUNK:2 * CHUNK] for x in ex]

    b_g = [bc[r][:, g * N:(g + 1) * N] for r, g in row_groups]
    c_b = [bc[r][:, (G + g) * N:(G + g + 1) * N].astype(BF16) for r, g in row_groups]
    cb = [_dot_nt(c, b.astype(BF16)) for c, b in zip(c_b, b_g)]
    x_sd = [(xs[r][:, gsl[g]] * sdec_x[r][:, gsl[g]]).astype(BF16) for r, g in row_groups]
    st = [_dot(b.T.astype(BF16), x) for b, x in zip(b_g, x_sd)]
    h_prev = [h_ref[r, :, gsl[g]] for r, g in row_groups]
    y_off = [_dot(c_b[i], h_prev[i].astype(BF16)) * e_a_x[r][:, gsl[g]] for i, (r, g) in enumerate(row_groups)]
    for i, (r, g) in enumerate(row_groups):
        h_ref[r, :, gsl[g]] = h_prev[i] * e_a_x[r][CHUNK - 1:CHUNK, gsl[g]] + st[i]
    seg = [jnp.where(tril, a_cum[r][:, e:e + 1] - a_cum_t[r][e:e + 1, :], NEG_BIG) for r, e in row_heads]
    m = [cb[r * G + e // HG] * jnp.exp(seg[i]) * dt_t[r][e:e + 1, :] for i, (r, e) in enumerate(row_heads)]
    m2 = [jnp.concatenate([m[r * SSD_HEADS + 2 * p], m[r * SSD_HEADS + 2 * p + 1]], axis=1).astype(BF16)
          for r, p in row_pairs]
    xp = [xs[r][:, 2 * p * P:(2 * p + 2) * P] for r, p in row_pairs]
    bd = [jnp.concatenate([jnp.where(left, x, 0.0), jnp.where(left, 0.0, x)], axis=0).astype(BF16)
          for x in xp]
    diag = [_dot(a, b) for a, b in zip(m2, bd)]
    for r in rows_:
        y = (jnp.concatenate(diag[r * NP:(r + 1) * NP], axis=1) + jnp.concatenate(y_off[r * G:(r + 1) * G], axis=1)
             + dsk_ref[...] * xs[r])
        y = y * _silu(z_ref[r])
        o_ref[r] = jnp.concatenate([_rms(y[:, gs], ng_ref[:, gs]) for gs in gsl], axis=1).astype(o_ref.dtype)


SSD_BATCH_PER_STEP = 4


def _ssd(proj, B, S, cw, cb, shift, dtb, alog, dsk, ng, expand):
    nc = S // CHUNK
    nb = SSD_BATCH_PER_STEP
    assert S % CHUNK == 0 and B % nb == 0, (B, S)
    proj3 = proj.reshape(B, S, proj.shape[1])
    full = lambda a: pl.BlockSpec(a.shape, lambda b, c: (0,) * a.ndim)
    rows = lambda w, idx: pl.BlockSpec((nb, CHUNK, w), lambda b, c: (b, c, idx))
    consts = [cw, cb, shift, dtb, alog, dsk, ng, expand]
    out = pl.pallas_call(
        _ssd_kernel,
        grid=(B // nb, nc),
        in_specs=[rows(W_Z, OFF_Z // W_Z), rows(W_XS, OFF_XS // W_XS), rows(W_BC, OFF_BC // W_BC),
                  rows(W_SM, OFF_SM // W_SM)] + [full(a) for a in consts],
        out_specs=pl.BlockSpec((nb, CHUNK, SSD_INNER), lambda b, c: (b, c, 0)),
        out_shape=jax.ShapeDtypeStruct((B, S, SSD_INNER), BF16),
        scratch_shapes=[pltpu.VMEM((nb, HALO, W_XS + W_BC), BF16),
                        pltpu.VMEM((nb, SSD_STATE, SSD_INNER), F32)],
        compiler_params=_params("parallel", "arbitrary"),
        name="ssd_mixer",
    )(proj3, proj3, proj3, proj3, *consts)
    return out.reshape(B * S, SSD_INNER)


def _l2norm(x):
    return x * lax.rsqrt(jnp.sum(x * x, axis=-1, keepdims=True) + EPS)


def _split(x):
    hi = x.astype(BF16)
    return hi, (x - hi.astype(F32)).astype(BF16)


def _dot3(a, b):
    (ah, al), (bh, bl) = a, b
    return _dot(jnp.concatenate([ah, al, ah], axis=1), jnp.concatenate([bh, bh, bl], axis=0))


def _dot2(a, b):
    return _dot(jnp.concatenate(a, axis=1), jnp.concatenate([b, b], axis=0))


def _dot2r(a, b):
    return _dot(jnp.concatenate([a, a], axis=1), jnp.concatenate(b, axis=0))


SOLVE_BLOCK_LOG2 = 3


def _block_masks(row, col):
    n_levels = int(math.log2(CHUNK)) - SOLVE_BLOCK_LOG2
    diag = (row >> SOLVE_BLOCK_LOG2) == (col >> SOLVE_BLOCK_LOG2)
    merges = []
    for lv in range(n_levels):
        s = SOLVE_BLOCK_LOG2 + lv
        same = (row >> (s + 1)) == (col >> (s + 1))
        lower_left = jnp.where(same, ((row >> s) & 1) - ((col >> s) & 1), 0) == 1
        merges.append(lower_left)
    return diag, merges


def _unit_lower_inverses(mats, eye, diag_mask, merge_masks):
    def as_factor(mask):
        return jnp.where(mask, 1.0, 0.0).astype(BF16)

    eye_b = eye.astype(BF16)
    diag_b = as_factor(diag_mask)
    d1 = [a * diag_b for a in mats]
    d2 = [_split(_dot(d, d)) for d in d1]
    t0 = [eye_b - d for d in d1]
    ts = [t.astype(F32) + _dot2r(t, x) for t, x in zip(t0, d2)]
    d4 = [_split(_dot3(x, x)) for x in d2]
    ts = [t + _dot3(_split(t), x) for t, x in zip(ts, d4)]
    for lv, m in enumerate(merge_masks):
        n = 1 << (SOLVE_BLOCK_LOG2 + lv)
        pairs = range(CHUNK // (2 * n))

        def take(x):
            return jnp.concatenate([x[(2 * b + 1) * n:(2 * b + 2) * n] for b in pairs], axis=0)

        def put(x, part):
            pieces = []
            for b in pairs:
                pieces += [x[2 * b * n:(2 * b + 1) * n], part[b * n:(b + 1) * n]]
            return jnp.concatenate(pieces, axis=0)

        tsp = [_split(t) for t in ts]
        mb = as_factor(m)
        low = [take(t) for t in ts]
        mid = [_split(_dot2(_split(x), a * mb)) for x, a in zip(low, mats)]
        ts = [put(t, x - _dot3(md, tp)) for t, x, md, tp in zip(ts, low, mid, tsp)]
    return ts


def _gdn_kernel(qkv_ref, gz_ref, sm_ref, cw_ref, shift_ref, dtb_ref, alog_ref, ng_ref, o_ref, tail_ref, s_ref):
    HD = GDN_HEAD
    NH = GDN_V_HEADS
    NB = qkv_ref.shape[0]
    rows_ = range(NB)
    items = [(r, h) for r in rows_ for h in range(NH)]

    @pl.when(pl.program_id(1) == 0)
    def _():
        tail_ref[...] = jnp.zeros(tail_ref.shape, tail_ref.dtype)
        s_ref[...] = jnp.zeros(s_ref.shape, F32)

    qkv = [_silu(_causal_conv(tail_ref.at[r], qkv_ref[r], cw_ref, shift_ref)) for r in rows_]

    row = lax.broadcasted_iota(jnp.int32, (CHUNK, CHUNK), 0)
    col = lax.broadcasted_iota(jnp.int32, (CHUNK, CHUNK), 1)
    incl = row >= col
    strict = row > col
    eye = (row == col).astype(F32)
    diag_mask, merge_masks = _block_masks(row, col)
    upper = (row <= col).astype(F32)
    pad = jnp.zeros((CHUNK - 2 * NH, CHUNK), F32)

    sm_t = [sm_ref[r].T for r in rows_]
    beta_t = [_sigmoid(x[SM_B:SM_B + NH, :]) for x in sm_t]
    g_t = [-jnp.exp(alog_ref[...]) * _softplus(x[SM_A:SM_A + NH, :] + dtb_ref[...]) for x in sm_t]
    gc_t = [jnp.dot(x, upper, precision=HIGHEST, preferred_element_type=F32) for x in g_t]
    cols = [jnp.concatenate([beta_t[r], gc_t[r], pad], axis=0).T for r in rows_]

    q = [[_l2norm(qkv[r][:, j * HD:(j + 1) * HD]) * (HD ** -0.5) for j in range(GDN_QK_HEADS)] for r in rows_]
    k = [[_l2norm(qkv[r][:, GDN_QK + j * HD:GDN_QK + (j + 1) * HD]) for j in range(GDN_QK_HEADS)] for r in rows_]
    k_t = [[x.T.astype(BF16) for x in k[r]] for r in rows_]
    qk_raw = [[_dot(a.astype(BF16), b) for a, b in zip(q[r], k_t[r])] for r in rows_]

    b_col = [cols[r][:, h:h + 1] for r, h in items]
    g_col = [cols[r][:, NH + h:NH + h + 1] for r, h in items]
    g_last = [gc_t[r][h:h + 1, CHUNK - 1:CHUNK] for r, h in items]
    decay = [jnp.exp(jnp.where(incl, g_col[i] - gc_t[r][h:h + 1, :], NEG_BIG)) for i, (r, h) in enumerate(items)]
    kb = [k[r][h // 2] * b_col[i] for i, (r, h) in enumerate(items)]
    a_mat = [jnp.where(strict, _dot(kb[i].astype(BF16), k_t[r][h // 2]) * decay[i], 0.0).astype(BF16)
             for i, (r, h) in enumerate(items)]
    t_inv = _unit_lower_inverses(a_mat, eye, diag_mask, merge_masks)

    e_g = [jnp.exp(x) for x in g_col]
    v = [qkv[r][:, 2 * GDN_QK + h * HD:2 * GDN_QK + (h + 1) * HD] for r, h in items]
    rhs = [jnp.concatenate([v[i] * b_col[i], kb[i] * e_g[i]], axis=1).astype(BF16) for i in range(len(items))]
    sol = [_dot2(_split(t_inv[i]), rhs[i]) for i in range(len(items))]
    state = [s_ref[i] for i in range(len(items))]
    ws = [_dot(jnp.concatenate([sol[i][:, HD:2 * HD], q[r][h // 2] * e_g[i]], axis=0).astype(BF16),
               state[i].astype(BF16)) for i, (r, h) in enumerate(items)]
    v_new_b = [(sol[i][:, 0:HD] - ws[i][0:CHUNK]).astype(BF16) for i in range(len(items))]
    o = [ws[i][CHUNK:2 * CHUNK] + _dot((qk_raw[r][h // 2] * decay[i]).astype(BF16), v_new_b[i])
         for i, (r, h) in enumerate(items)]
    k_dec_t = [(k[r][h // 2] * jnp.exp(g_last[i] - g_col[i])).T.astype(BF16) for i, (r, h) in enumerate(items)]
    for i in range(len(items)):
        s_ref[i] = state[i] * jnp.exp(g_last[i]) + _dot(k_dec_t[i], v_new_b[i])
    for i, (r, h) in enumerate(items):
        out = _rms(o[i], ng_ref[...]) * _silu(gz_ref[r, :, h * HD:(h + 1) * HD])
        o_ref[r, :, h * HD:(h + 1) * HD] = out.astype(o_ref.dtype)


GDN_BATCH_PER_STEP = 4


def _gdn(proj, B, S, cw, shift, dtb, alog, ng):
    nc = S // CHUNK
    nb = GDN_BATCH_PER_STEP
    assert S % CHUNK == 0 and B % nb == 0, (B, S)
    width = GDN_V_HEADS * GDN_HEAD
    proj3 = proj.reshape(B, S, proj.shape[1])
    full = lambda a: pl.BlockSpec(a.shape, lambda b, c: (0,) * a.ndim)
    rows = lambda w, idx: pl.BlockSpec((nb, CHUNK, w), lambda b, c: (b, c, idx))
    consts = [cw, shift, dtb, alog, ng]
    out = pl.pallas_call(
        _gdn_kernel,
        grid=(B // nb, nc),
        in_specs=[rows(W_GQKV, OFF_GQKV // W_GQKV), rows(W_GZ, OFF_GZ // W_GZ),
                  rows(W_SM, OFF_SM // W_SM)] + [full(a) for a in consts],
        out_specs=pl.BlockSpec((nb, CHUNK, width), lambda b, c: (b, c, 0)),
        out_shape=jax.ShapeDtypeStruct((B, S, width), BF16),
        scratch_shapes=[pltpu.VMEM((nb, HALO, GDN_QKV), BF16),
                        pltpu.VMEM((nb * GDN_V_HEADS, GDN_HEAD, GDN_HEAD), F32)],
        compiler_params=_params("parallel", "arbitrary"),
        name="gdn_mixer",
    )(proj3, proj3, proj3, *consts)
    return out.reshape(B * S, width)


ATTN_BLOCK = 512
LOG2E = math.log2(math.e)
BF16_SUBLANES = 16
V_ROWS = MLA_V + BF16_SUBLANES
HEADS_PER_STEP = 4


def _mla_prep_kernel(cq_ref, ckv_ref, kr_ref, cos_ref, sin_ref, qg_ref, kvg_ref, wq_ref, wkv_ref,
                     qt_ref, k_ref, vt_ref):
    H = MLA_HEADS
    tm = cq_ref.shape[0]
    scale = (MLA_NOPE + MLA_ROPE) ** -0.5 * LOG2E
    cos4 = cos_ref[...]
    sin4 = sin_ref[...]
    lane = lax.broadcasted_iota(jnp.int32, (tm, LANES), 1)
    left = lane < MLA_ROPE

    qm = _dot(_rms(cq_ref[...], qg_ref[...]).astype(BF16), wq_ref[...])
    kvm = _dot(_rms(ckv_ref[...], kvg_ref[...]).astype(BF16), wkv_ref[...])

    kr = kr_ref[...]
    k_rope = (kr * jnp.where(left, cos4, sin4)
              + pltpu.roll(kr, MLA_ROPE, axis=1) * jnp.where(left, sin4, cos4)).astype(BF16)

    pe_off = H * MLA_NOPE
    rot_off = pe_off + H * MLA_ROPE
    for h in range(H):
        jb = (h // 2) * LANES
        pe = qm[:, pe_off + jb:pe_off + jb + LANES]
        rot = qm[:, rot_off + jb:rot_off + jb + LANES]
        roped = pe * cos4 + rot * sin4
        mine = left if h % 2 == 0 else jnp.logical_not(left)
        q_nope = qm[:, h * MLA_NOPE:(h + 1) * MLA_NOPE]
        qt_ref[h, 0, 0:LANES, :] = (q_nope * scale).T.astype(BF16)
        qt_ref[h, 0, LANES:2 * LANES, :] = (jnp.where(mine, roped, 0.0) * scale).T.astype(BF16)
        base = h * (MLA_NOPE + MLA_V)
        k_ref[h] = jnp.concatenate([kvm[:, base:base + MLA_NOPE].astype(BF16), k_rope], axis=1)
        vt_ref[h, 0, 0:MLA_V, :] = kvm[:, base + MLA_NOPE:base + MLA_NOPE + MLA_V].T.astype(BF16)
        ones_row = lax.broadcasted_iota(jnp.int32, (BF16_SUBLANES, tm), 0) == 0
        vt_ref[h, 0, MLA_V:V_ROWS, :] = jnp.where(ones_row, 1.0, 0.0).astype(BF16)


def _mla_prep(proj, cos4, sin4, qg, kvg, wq, wkv):
    T = proj.shape[0]
    H = MLA_HEADS
    tm = ATTN_BLOCK
    nt = T // tm
    full = lambda a: pl.BlockSpec(a.shape, lambda i: (0,) * a.ndim)
    rows = lambda w, idx: pl.BlockSpec((tm, w), lambda i: (i, idx))
    consts = [qg, kvg, wq, wkv]
    dq = 2 * LANES
    return pl.pallas_call(
        _mla_prep_kernel,
        grid=(nt,),
        in_specs=[rows(W_CQ, OFF_CQ // W_CQ), rows(W_CKV, OFF_CKV // W_CKV), rows(W_KR, OFF_KR // W_KR),
                  rows(LANES, 0), rows(LANES, 0)] + [full(a) for a in consts],
        out_specs=[pl.BlockSpec((H, 1, dq, tm), lambda i: (0, i, 0, 0)),
                   pl.BlockSpec((H, tm, dq), lambda i: (0, i, 0)),
                   pl.BlockSpec((H, 1, V_ROWS, tm), lambda i: (0, i, 0, 0))],
        out_shape=[jax.ShapeDtypeStruct((H, nt, dq, tm), BF16),
                   jax.ShapeDtypeStruct((H, T, dq), BF16),
                   jax.ShapeDtypeStruct((H, nt, V_ROWS, tm), BF16)],
        compiler_params=_params("parallel"),
        name="mla_prep",
    )(proj, proj, proj, cos4, sin4, *consts)


def _attn_kernel(qt_ref, k_ref, vt_ref, o_ref, m_ref, acc_ref):
    tb = ATTN_BLOCK
    hs = range(HEADS_PER_STEP)
    qi = pl.program_id(2)
    m_ref[...] = jnp.full(m_ref.shape, NEG_BIG, F32)
    acc_ref[...] = jnp.zeros(acc_ref.shape, F32)

    def update(kis, last_is_diagonal):
        nb = range(len(kis))
        s = [[_dot(k_ref[h, pl.ds(pl.multiple_of(ki * tb, tb), tb), :], qt_ref[h, 0]) for ki in kis]
             for h in hs]
        if last_is_diagonal:
            key = lax.broadcasted_iota(jnp.int32, (tb, tb), 0)
            qry = lax.broadcasted_iota(jnp.int32, (tb, tb), 1)
            for h in hs:
                s[h][-1] = jnp.where(key <= qry, s[h][-1], NEG_BIG)
        m_old = [m_ref[h] for h in hs]
        m_new = list(m_old)
        for h in hs:
            for j in nb:
                m_new[h] = jnp.maximum(m_new[h], jnp.max(s[h][j], axis=0, keepdims=True))
        p = [[jnp.exp2(s[h][j] - m_new[h]).astype(BF16) for j in nb] for h in hs]
        pv = [[_dot(vt_ref[h, kis[j]], p[h][j]) for j in nb] for h in hs]
        for h in hs:
            acc_ref[h] = jnp.exp2(m_old[h] - m_new[h]) * acc_ref[h] + sum(pv[h][1:], pv[h][0])
            m_ref[h] = m_new[h]

    def pair(i, carry):
        update([2 * i, 2 * i + 1], False)
        return carry

    lax.fori_loop(0, lax.shift_right_logical(qi, 1), pair, 0)
    odd = (qi & 1) == 1

    @pl.when(odd)
    def _():
        update([qi - 1, qi], True)

    @pl.when(jnp.logical_not(odd))
    def _():
        update([qi], True)

    for h in hs:
        acc = acc_ref[h]
        o_ref[:, h * MLA_V:(h + 1) * MLA_V] = (acc[0:MLA_V] / acc[MLA_V:MLA_V + 1]).T.astype(o_ref.dtype)


def _attention(q_t, k, v_t, B, S):
    H, T, dq = k.shape
    tb = ATTN_BLOCK
    nq = S // tb
    hp = HEADS_PER_STEP
    assert S % tb == 0 and H % hp == 0, (S, H)
    return pl.pallas_call(
        _attn_kernel,
        grid=(B, H // hp, nq),
        in_specs=[pl.BlockSpec((hp, 1, dq, tb), lambda b, h, i: (h, b * nq + i, 0, 0)),
                  pl.BlockSpec((hp, S, dq), lambda b, h, i: (h, b, 0)),
                  pl.BlockSpec((hp, nq, V_ROWS, tb), lambda b, h, i: (h, b, 0, 0))],
        out_specs=pl.BlockSpec((tb, hp * MLA_V), lambda b, h, i: (b * nq + i, h)),
        out_shape=jax.ShapeDtypeStruct((T, H * MLA_V), BF16),
        scratch_shapes=[pltpu.VMEM((hp, 1, tb), F32), pltpu.VMEM((hp, V_ROWS, tb), F32)],
        compiler_params=_params("parallel", "parallel", "arbitrary"),
        name="mla_attention",
    )(q_t, k, v_t)


def _merge_kernel(x_ref, ys_ref, ym_ref, yg_ref, g_ref, wgate_ref, ws_ref, wm_ref, wg_ref, wo_ref, g2_ref,
                  h_ref, hn_ref):
    D = D_MODEL
    x = x_ref[...]
    xn = _rms(x, g_ref[...]).astype(BF16)
    mixed = None
    for idx, (y_ref, w_ref) in enumerate(((ys_ref, ws_ref), (ym_ref, wm_ref), (yg_ref, wg_ref))):
        gate = _sigmoid(_dot(xn, wgate_ref[:, idx * D:(idx + 1) * D]))
        term = gate * _dot(y_ref[...], w_ref[...])
        mixed = term if mixed is None else mixed + term
    h = x + _dot(mixed.astype(BF16), wo_ref[...])
    h_ref[...] = h
    hn_ref[...] = _rms(h, g2_ref[...]).astype(BF16)


def _merge(x2, ys, ym, yg, g, wgate, ws, wm, wg, wo, g2):
    T, D = x2.shape
    tm = min(T, MERGE_ROWS)
    full = lambda a: pl.BlockSpec(a.shape, lambda i: (0,) * a.ndim)
    rows = pl.BlockSpec((tm, D), lambda i: (i, 0))
    consts = [g, wgate, ws, wm, wg, wo, g2]
    return pl.pallas_call(
        _merge_kernel,
        grid=(T // tm,),
        in_specs=[rows] * 4 + [full(a) for a in consts],
        out_specs=[rows, rows],
        out_shape=[jax.ShapeDtypeStruct((T, D), F32), jax.ShapeDtypeStruct((T, D), BF16)],
        compiler_params=_params("parallel"),
        name="merge",
    )(x2, ys, ym, yg, *consts)


def _ffn_kernel(h_ref, hn_ref, wu_ref, wd_ref, fg_ref, o_ref, acc_ref, *, final_norm):
    j = pl.program_id(1)

    @pl.when(j == 0)
    def _():
        acc_ref[...] = jnp.zeros(acc_ref.shape, F32)

    up = jnp.maximum(_dot(hn_ref[...], wu_ref[...]), 0.0)
    acc_ref[...] += _dot((up * up).astype(BF16), wd_ref[...])

    @pl.when(j == pl.num_programs(1) - 1)
    def _():
        out = h_ref[...] + acc_ref[...]
        if final_norm:
            out = _rms(out, fg_ref[...])
        o_ref[...] = out


def _ffn(h, hn, wu, wd, fg, final_norm):
    T, D = h.shape
    F = wu.shape[1]
    tm = min(T, FFN_ROWS)
    tf = min(F, FFN_HIDDEN_TILE)
    return pl.pallas_call(
        functools.partial(_ffn_kernel, final_norm=final_norm),
        grid=(T // tm, F // tf),
        in_specs=[pl.BlockSpec((tm, D), lambda i, j: (i, 0)),
                  pl.BlockSpec((tm, D), lambda i, j: (i, 0)),
                  pl.BlockSpec((D, tf), lambda i, j: (0, j)),
                  pl.BlockSpec((tf, D), lambda i, j: (j, 0)),
                  pl.BlockSpec((1, D), lambda i, j: (0, 0))],
        out_specs=pl.BlockSpec((tm, D), lambda i, j: (i, 0)),
        out_shape=jax.ShapeDtypeStruct((T, D), F32),
        scratch_shapes=[pltpu.VMEM((tm, D), F32)],
        compiler_params=_params("parallel", "arbitrary"),
        name="ffn",
    )(h, hn, wu, wd, fg)


def _rotate_half_cols(w):
    half = w.shape[-1] // 2
    return jnp.concatenate([-w[..., half:], w[..., :half]], axis=-1)


def _split_in_proj(w_in):
    sizes = (SSD_INNER, SSD_INNER, 2 * SSD_GROUPS * SSD_STATE, SSD_HEADS, MLA_Q_LORA, MLA_KV_LORA, MLA_ROPE,
             GDN_QKV, GDN_V_HEADS * GDN_HEAD, GDN_V_HEADS, GDN_V_HEADS, 3 * D_MODEL)
    offs = np.cumsum((0,) + sizes)
    return [w_in[:, int(offs[i]):int(offs[i + 1])] for i in range(len(sizes))]


def _layer_weights(w_in, w_uq):
    z, xs, bc, dt, cq, ckv, kr, gqkv, gz, gb, ga, gates = _split_in_proj(w_in)
    small = jnp.concatenate([dt, gb, ga, jnp.zeros((D_MODEL, W_SM - 32), w_in.dtype)], axis=1)
    w_a = jnp.concatenate([gqkv, z, xs, gz, bc, cq, ckv, kr, _rotate_half_cols(kr), small], axis=1)
    H = MLA_HEADS
    wq = w_uq.reshape(MLA_Q_LORA, H, MLA_NOPE + MLA_ROPE)
    nope = wq[:, :, :MLA_NOPE].reshape(MLA_Q_LORA, H * MLA_NOPE)
    pe = wq[:, :, MLA_NOPE:]
    w_q = jnp.concatenate([nope, pe.reshape(MLA_Q_LORA, H * MLA_ROPE),
                           _rotate_half_cols(pe).reshape(MLA_Q_LORA, H * MLA_ROPE)], axis=1)
    return w_a, gates, w_q


def _pad_lanes(v, n=LANES):
    return jnp.concatenate([v, jnp.zeros((n - v.shape[0],), v.dtype)])[None, :]


def kernel(x, positions, norm1_g, w_in, ssd_conv_w, ssd_conv_b, ssd_dt_bias, ssd_a_log, ssd_d, ssd_norm_g,
           mla_q_norm_g, mla_w_uq, mla_kv_norm_g, mla_w_ukv, gdn_conv_w, gdn_dt_bias, gdn_a_log, gdn_norm_g,
           w_ssd_out, w_mla_out, w_gdn_out, w_out, norm2_g, w_up, w_down, final_norm_g):
    B, S, D = x.shape
    T = B * S
    x2 = x.reshape(T, D)

    inv = ROPE_THETA ** (-jnp.arange(0, MLA_ROPE, 2, dtype=F32) / MLA_ROPE)
    inv4 = jnp.tile(inv, LANES // inv.shape[0])[None, :]
    cos4, sin4 = _rope_tables(positions.reshape(T, 1), inv4)

    shift = _shift_matrix()
    expand = (jnp.arange(LANES)[:, None] == (jnp.arange(SSD_INNER)[None, :] // SSD_HEAD_DIM)).astype(BF16)

    for l in range(DEPTH):
        w_a, w_gate, w_q = _layer_weights(w_in[l].astype(BF16), mla_w_uq[l].astype(BF16))
        proj = _inproj(x2, norm1_g[l][None, :], w_a)

        y_ssd = _ssd(proj, B, S, ssd_conv_w[l], ssd_conv_b[l][None, :], shift,
                     _pad_lanes(ssd_dt_bias[l]), _pad_lanes(ssd_a_log[l]),
                     jnp.repeat(ssd_d[l], SSD_HEAD_DIM)[None, :], ssd_norm_g[l][None, :], expand)

        q, k, v = _mla_prep(proj, cos4, sin4, mla_q_norm_g[l][None, :], mla_kv_norm_g[l][None, :],
                            w_q, mla_w_ukv[l].astype(BF16))
        y_mla = _attention(q, k, v, B, S)

        y_gdn = _gdn(proj, B, S, gdn_conv_w[l], shift,
                     jnp.broadcast_to(gdn_dt_bias[l][:, None], (GDN_V_HEADS, CHUNK)),
                     jnp.broadcast_to(gdn_a_log[l][:, None], (GDN_V_HEADS, CHUNK)),
                     gdn_norm_g[l][None, :])

        h, hn = _merge(x2, y_ssd, y_mla, y_gdn, norm1_g[l][None, :], w_gate,
                       w_ssd_out[l].astype(BF16), w_mla_out[l].astype(BF16), w_gdn_out[l].astype(BF16),
                       w_out[l].astype(BF16), norm2_g[l][None, :])
        x2 = _ffn(h, hn, w_up[l].astype(BF16), w_down[l].astype(BF16),
                  final_norm_g[None, :], final_norm=(l == DEPTH - 1))
    return x2.reshape(B, S, D)
```
